```python
import math
import jax, jax.numpy as jnp
from jax import lax
import numpy as np

D_MODEL = 4096
BATCH = 32
SEQ = 256
DEPTH = 2
DEC_BATCH = 8
DEC_SEQ = 1024
PAST_LEN = 512

GRID_W = 64
WIN_H = 8
WIN_W = 16
ATTN_WIDTH = D_MODEL // 2
HYENA_WIDTH = D_MODEL // 2
HEAD_DIM = 128
N_HEADS_A = ATTN_WIDTH // HEAD_DIM
SHORT_CONV = 3
FILTER_EMB = 33
FILTER_HIDDEN = 64
HYENA_ORDER = 2
DECAY_TARGET = 1e-2
FAST_DECAY_PCT = 0.3
SLOW_DECAY_PCT = 1.5
FNET_GROUPS = 8
D_FF = 11008
N_EVEN = (DEPTH + 1) // 2
N_ODD = DEPTH // 2
N_MOD = 9
CTX_BLOCK = 128
EPS = 1e-6
NEG_INF = -1e30

kernel_name = "hybrid_natten_hyena_fnet_prefix_step"


def _rmsnorm(x, g):
    xf = x.astype(jnp.float32)
    y = xf * lax.rsqrt(jnp.mean(xf * xf, axis=-1, keepdims=True) + EPS)
    return (y * g.astype(jnp.float32)).astype(x.dtype)


def _adaln(cond, w_mod, b_mod):
    m = jax.nn.silu(cond) @ w_mod + b_mod
    return m.reshape(cond.shape[0], N_MOD, cond.shape[-1])


def _mod_norm(x, g, mod, i):
    return _rmsnorm(x, g) * (1.0 + mod[:, None, i + 1]) + mod[:, None, i]


def _swiglu(h, w1, w3, w2):
    return (jax.nn.silu(h @ w1) * (h @ w3)) @ w2


def _ffn_half(x, g, mod, i, w1, w3, w2):
    h = _mod_norm(x, g, mod, i)
    return x + 0.5 * mod[:, None, i + 2] * _swiglu(h, w1, w3, w2)


def _heads(p, i):
    b, l, _ = p.shape
    return p[..., i * ATTN_WIDTH:(i + 1) * ATTN_WIDTH].reshape(b, l, N_HEADS_A, HEAD_DIM)


def _ctx_attention(q, k, v):
    b, lc, h, dh = q.shape
    nb = lc // CTX_BLOCK
    qb = jnp.moveaxis(q.reshape(b, nb, CTX_BLOCK, h, dh), 1, 0)
    scale = dh ** -0.5

    def block(q_blk):
        s = jnp.einsum("bqhd,bkhd->bhqk", q_blk, k, preferred_element_type=jnp.float32) * scale
        p = jax.nn.softmax(s, axis=-1).astype(v.dtype)
        return jnp.einsum("bhqk,bkhd->bqhd", p, v)

    o = lax.map(block, qb)
    return jnp.moveaxis(o, 0, 1).reshape(b, lc, h * dh)


def _na_latent(q, k, v, k_ctx, v_ctx, rpb):
    b, l, h, dh = q.shape
    rows = l // GRID_W
    kh = min(WIN_H, rows)
    n_lat = kh * GRID_W
    scale = dh ** -0.5
    qg = jnp.moveaxis(q.reshape(b, rows, GRID_W, h, dh), 1, 0)
    kg = k.reshape(b, rows, GRID_W, h, dh)
    vg = v.reshape(b, rows, GRID_W, h, dh)
    col = jnp.arange(GRID_W)
    col_start = jnp.clip(col - WIN_W // 2, 0, GRID_W - WIN_W)
    col_ok = (col[None, :] >= col_start[:, None]) & (col[None, :] < col_start[:, None] + WIN_W)
    dc_idx = jnp.clip(col[None, :] - col[:, None] + WIN_W - 1, 0, 2 * WIN_W - 2)
    rpb32 = rpb.astype(jnp.float32)

    def row_block(args):
        r, q_r = args
        start = jnp.clip(r - kh // 2, 0, rows - kh)
        k_r = lax.dynamic_slice_in_dim(kg, start, kh, axis=1)
        v_r = lax.dynamic_slice_in_dim(vg, start, kh, axis=1)
        dr_idx = start + jnp.arange(kh) - r + WIN_H - 1
        bias = jnp.transpose(rpb32[:, dr_idx][:, :, dc_idx], (0, 2, 1, 3))
        s_lat = jnp.einsum("bqhd,bikhd->bhqik", q_r, k_r, preferred_element_type=jnp.float32) * scale + bias
        s_lat = jnp.where(col_ok[:, None, :], s_lat, NEG_INF).reshape(b, h, GRID_W, n_lat)
        s_ctx = jnp.einsum("bqhd,bchd->bhqc", q_r, k_ctx, preferred_element_type=jnp.float32) * scale
        p = jax.nn.softmax(jnp.concatenate([s_lat, s_ctx], axis=-1), axis=-1).astype(v.dtype)
        o_lat = jnp.einsum("bhqn,bnhd->bqhd", p[..., :n_lat], v_r.reshape(b, n_lat, h, dh))
        o_ctx = jnp.einsum("bhqc,bchd->bqhd", p[..., n_lat:], v_ctx)
        return o_lat + o_ctx

    o = lax.map(row_block, (jnp.arange(rows), qg))
    return jnp.moveaxis(o, 0, 1).reshape(b, l, h * dh)


def _hyena_filter_freq(l, w1, b1, w2, b2, w3, freq):
    f32 = jnp.float32
    n_bands = (FILTER_EMB - 1) // 2
    t = jnp.linspace(0.0, 1.0, l, dtype=f32)[:, None]
    w = (2.0 * math.pi / l) * jnp.arange(l, dtype=f32)[:, None]
    bands = jnp.linspace(1e-4, n_bands - 1, n_bands, dtype=f32)[None, :]
    z = jnp.concatenate([t, jnp.cos(bands * w), -jnp.sin(bands * w)], axis=-1)
    hf = jnp.sin(freq[0].astype(f32) * (z @ w1.astype(f32) + b1.astype(f32)))
    hf = jnp.sin(freq[1].astype(f32) * (hf @ w2.astype(f32) + b2.astype(f32)))
    hf = (hf @ w3.astype(f32)).reshape(l, HYENA_ORDER, 2, HYENA_WIDTH)
    deltas = jnp.abs(jnp.linspace(math.log(DECAY_TARGET) / FAST_DECAY_PCT,
                                  math.log(DECAY_TARGET) / SLOW_DECAY_PCT, HYENA_WIDTH, dtype=f32))
    hf = hf * jnp.exp(-t[:, :, None, None] * deltas)
    h_fwd, h_bwd = hf[:, :, 0], hf[:, :, 1]
    filt = jnp.concatenate([h_fwd, jnp.zeros_like(h_fwd[:1]), h_bwd[1:][::-1]], axis=0)
    filt = filt * lax.rsqrt(jnp.sum(filt * filt, axis=0, keepdims=True) + EPS)
    return jnp.fft.rfft(filt, axis=0)


def _long_conv(u, k_f, bias):
    l = u.shape[1]
    y = jnp.fft.irfft(jnp.fft.rfft(u, n=2 * l, axis=1) * k_f, n=2 * l, axis=1)[:, :l]
    return y + u * bias


def _short_conv(u, w, b):
    l = u.shape[1]
    pad = SHORT_CONV // 2
    up = jnp.pad(u, ((0, 0), (pad, pad), (0, 0)))
    out = b + up[:, 0:l] * w[0]
    for j in range(1, SHORT_CONV):
        out = out + up[:, j:j + l] * w[j]
    return out


def _hyena(p, w_short, b_short, fw1, fb1, fw2, fb2, fw3, ffreq, fbias):
    l = p.shape[1]
    u = _short_conv(p, w_short, b_short).astype(jnp.float32)
    v, x1, x2 = jnp.split(u, 3, axis=-1)
    k_f = _hyena_filter_freq(l, fw1, fb1, fw2, fb2, fw3, ffreq)
    bias = fbias.astype(jnp.float32)
    z = x1 * _long_conv(v, k_f[:, 0], bias[0])
    z = x2 * _long_conv(z, k_f[:, 1], bias[1])
    return z.astype(p.dtype)


def _fnet(h, w_c):
    b, l, d = h.shape
    hg = h.astype(jnp.float32).reshape(b, l, FNET_GROUPS, d // FNET_GROUPS)
    mixed = jnp.real(jnp.fft.fft2(hg, axes=(1, 3), norm="ortho")).reshape(b, l, d)
    return mixed.astype(h.dtype) @ w_c


def setup_inputs(seed: int = 0) -> dict:
    key = jax.random.key(seed)
    ks = jax.random.split(key, 32)
    f32 = jnp.float32
    D = D_MODEL
    P = 3 * ATTN_WIDTH + 3 * HYENA_WIDTH

    def nrm(k, shape, s):
        return jax.random.normal(k, shape, f32) * s

    return {
        "x_prompt": nrm(ks[0], (BATCH, SEQ, D), 1.0),
        "x_sample": nrm(ks[1], (DEC_BATCH, DEC_SEQ, D), 1.0),
        "cache_k": nrm(ks[2], (DEC_BATCH, N_EVEN, PAST_LEN, N_HEADS_A, HEAD_DIM), 1.0),
        "cache_v": nrm(ks[3], (DEC_BATCH, N_EVEN, PAST_LEN, N_HEADS_A, HEAD_DIM), 1.0),
        "c": nrm(ks[4], (DEC_BATCH, D), 1.0),
        "c_ctx": nrm(ks[5], (D,), 1.0),
        "w_mod": nrm(ks[6], (DEPTH, D, N_MOD * D), 0.5 * D ** -0.5),
        "b_mod": nrm(ks[7], (DEPTH, N_MOD * D), 0.02),
        "norm_g": 1.0 + nrm(ks[8], (DEPTH, 3, D), 0.02),
        "ffn_w1": nrm(ks[9], (DEPTH, 2, D, D_FF), D ** -0.5),
        "ffn_w3": nrm(ks[10], (DEPTH, 2, D, D_FF), D ** -0.5),
        "ffn_w2": nrm(ks[11], (DEPTH, 2, D_FF, D), D_FF ** -0.5),
        "w_in": nrm(ks[12], (N_EVEN, D, P), D ** -0.5),
        "w_out": nrm(ks[13], (N_EVEN, ATTN_WIDTH + HYENA_WIDTH, D), (ATTN_WIDTH + HYENA_WIDTH) ** -0.5),
        "rpb": nrm(ks[14], (N_EVEN, N_HEADS_A, 2 * WIN_H - 1, 2 * WIN_W - 1), 0.02),
        "w_short": nrm(ks[15], (N_EVEN, SHORT_CONV, 3 * HYENA_WIDTH), SHORT_CONV ** -0.5),
        "b_short": nrm(ks[16], (N_EVEN, 3 * HYENA_WIDTH), 0.02),
        "filt_w1": nrm(ks[17], (N_EVEN, FILTER_EMB, FILTER_HIDDEN), FILTER_EMB ** -0.5),
        "filt_b1": nrm(ks[18], (N_EVEN, FILTER_HIDDEN), 0.02),
        "filt_w2": nrm(ks[19], (N_EVEN, FILTER_HIDDEN, FILTER_HIDDEN), FILTER_HIDDEN ** -0.5),
        "filt_b2": nrm(ks[20], (N_EVEN, FILTER_HIDDEN), 0.02),
        "filt_w3": nrm(ks[21], (N_EVEN, FILTER_HIDDEN, HYENA_ORDER * 2 * HYENA_WIDTH), FILTER_HIDDEN ** -0.5),
        "filt_freq": 1.0 + nrm(ks[22], (N_EVEN, 2, FILTER_HIDDEN), 0.02),
        "filt_bias": nrm(ks[23], (N_EVEN, HYENA_ORDER, HYENA_WIDTH), 0.5),
        "w_fnet": nrm(ks[24], (N_ODD, D, D), D ** -0.5),
        "final_g": 1.0 + nrm(ks[25], (D,), 0.02),
    }


def reference(x_prompt, x_sample, cache_k, cache_v, c, c_ctx, w_mod, b_mod, norm_g, ffn_w1, ffn_w3, ffn_w2,
              w_in, w_out, rpb, w_short, b_short, filt_w1, filt_b1, filt_w2, filt_b2, filt_w3, filt_freq,
              filt_bias, w_fnet, final_g):
    xp, xs = x_prompt, x_sample
    new_k, new_v = [], []
    for l in range(DEPTH):
        mc = _adaln(c_ctx[None, :], w_mod[l], b_mod[l])
        ml = _adaln(c, w_mod[l], b_mod[l])
        xp = _ffn_half(xp, norm_g[l, 0], mc, 0, ffn_w1[l, 0], ffn_w3[l, 0], ffn_w2[l, 0])
        xs = _ffn_half(xs, norm_g[l, 0], ml, 0, ffn_w1[l, 0], ffn_w3[l, 0], ffn_w2[l, 0])
        hp = _mod_norm(xp, norm_g[l, 1], mc, 3)
        hs = _mod_norm(xs, norm_g[l, 1], ml, 3)
        if l % 2 == 0:
            e = l // 2
            pp = hp @ w_in[e]
            ps = hs @ w_in[e]
            kp, vp = _heads(pp, 1), _heads(pp, 2)
            a_p = _ctx_attention(_heads(pp, 0), kp, vp)
            a_s = _na_latent(_heads(ps, 0), _heads(ps, 1), _heads(ps, 2), cache_k[:, e], cache_v[:, e], rpb[e])
            hy_p = _hyena(pp[..., 3 * ATTN_WIDTH:], w_short[e], b_short[e], filt_w1[e], filt_b1[e],
                          filt_w2[e], filt_b2[e], filt_w3[e], filt_freq[e], filt_bias[e])
            hy_s = _hyena(ps[..., 3 * ATTN_WIDTH:], w_short[e], b_short[e], filt_w1[e], filt_b1[e],
                          filt_w2[e], filt_b2[e], filt_w3[e], filt_freq[e], filt_bias[e])
            mix_p = jnp.concatenate([a_p, hy_p], axis=-1) @ w_out[e]
            mix_s = jnp.concatenate([a_s, hy_s], axis=-1) @ w_out[e]
            new_k.append(kp)
            new_v.append(vp)
        else:
            o = l // 2
            mix_p = _fnet(hp, w_fnet[o])
            mix_s = _fnet(hs, w_fnet[o])
        xp = xp + mc[:, None, 5] * mix_p
        xs = xs + ml[:, None, 5] * mix_s
        xp = _ffn_half(xp, norm_g[l, 2], mc, 6, ffn_w1[l, 1], ffn_w3[l, 1], ffn_w2[l, 1])
        xs = _ffn_half(xs, norm_g[l, 2], ml, 6, ffn_w1[l, 1], ffn_w3[l, 1], ffn_w2[l, 1])
    y_prompt = _rmsnorm(xp, final_g)
    y_sample = _rmsnorm(xs, final_g)
    return (y_prompt, y_sample, jnp.stack(new_k, axis=1), jnp.stack(new_v, axis=1))
```

```python
import functools
import math

import jax
import jax.numpy as jnp
from jax import lax
from jax.experimental import pallas as pl
from jax.experimental.pallas import tpu as pltpu

F32 = jnp.float32
BF16 = jnp.bfloat16

N_MOD = 9
GRID_W = 64
WIN_H = 8
WIN_W = 16
HEAD_DIM = 128
SHORT_CONV = 3
FILTER_EMB = 33
HYENA_ORDER = 2
DECAY_TARGET = 1e-2
FAST_DECAY_PCT = 0.3
SLOW_DECAY_PCT = 1.5
FNET_GROUPS = 8
EPS = 1e-6
NEG_INF = -1e30

V7X_VMEM_BYTES = 64 * 2**20
VMEM_LIMIT = V7X_VMEM_BYTES - 8 * 2**20
LANE = 128
BF16_SUBLANES = 16
COND_ROWS = 16


def _tile(n, pref, unit=LANE):
    if n <= pref:
        return n
    t = (pref // unit) * unit
    while t > unit and n % t:
        t -= unit
    assert n % t == 0, (n, pref)
    return t


def _params(*sem):
    return pltpu.CompilerParams(dimension_semantics=sem, vmem_limit_bytes=VMEM_LIMIT)


class _Tokens:
    def __init__(self, b_p, l_p, b_s, l_s):
        self.b_p, self.l_p, self.b_s, self.l_s = b_p, l_p, b_s, l_s
        self.n_p, self.n_s = b_p * l_p, b_s * l_s
        self.n = self.n_p + self.n_s

    def row_tile(self, pref):
        return _tile(math.gcd(self.n_p, self.l_s), pref, BF16_SUBLANES)

    def cond(self, i, tm):
        tiles_p, per = self.n_p // tm, self.l_s // tm
        return jnp.where(i < tiles_p, 0, 1 + (i - tiles_p) // per)


def _mod_norm_rows(x, g, m, mi):
    y = x * lax.rsqrt(jnp.mean(x * x, axis=-1, keepdims=True) + EPS) * g
    return y * (1.0 + m[mi + 1:mi + 2]) + m[mi:mi + 1]


def _fill_mod_norm(x_ref, g_ref, mod_ref, h_ref, mi):
    rows = BF16_SUBLANES
    g = g_ref[...]
    m = mod_ref[0]

    def body(c, carry):
        r = pl.multiple_of(c * rows, rows)
        h_ref[pl.ds(r, rows), :] = _mod_norm_rows(x_ref[pl.ds(r, rows), :], g, m, mi).astype(h_ref.dtype)
        return carry

    lax.fori_loop(0, x_ref.shape[0] // rows, body, 0)


def _dot(a, b):
    return jnp.dot(a, b, preferred_element_type=F32)


def _dot_nt(a, b):
    return lax.dot_general(a, b, (((1,), (1,)), ((), ())), preferred_element_type=F32)


def _adaln_kernel(c_ref, w_ref, b_ref, o_ref):
    c = c_ref[...]
    s = c * (1.0 / (1.0 + jnp.exp(-c)))
    o_ref[...] = jnp.dot(s, w_ref[...], preferred_element_type=F32, precision=lax.Precision.HIGHEST) + b_ref[...]


def _adaln(conds, w_mod, b_mod, l):
    d = conds.shape[1]
    n = w_mod.shape[-1]
    tn = _tile(n, 1024)
    out = pl.pallas_call(
        _adaln_kernel,
        grid=(n // tn,),
        in_specs=[
            pl.BlockSpec((COND_ROWS, d), lambda j: (0, 0)),
            pl.BlockSpec((None, d, tn), lambda j: (l, 0, j)),
            pl.BlockSpec((None, 1, tn), lambda j: (l, 0, j)),
        ],
        out_specs=pl.BlockSpec((COND_ROWS, tn), lambda j: (0, j)),
        out_shape=jax.ShapeDtypeStruct((COND_ROWS, n), F32),
        compiler_params=_params("arbitrary"),
        name="adaln",
    )(conds, w_mod, b_mod.reshape(b_mod.shape[0], 1, n))
    return out.reshape(COND_ROWS, N_MOD, d)


def _ffn_kernel(x_ref, mod_ref, g_ref, w1_ref, w3_ref, w2_ref, o_ref, h_ref, *, mi):
    j = pl.program_id(1)

    @pl.when(j == 0)
    def _():
        _fill_mod_norm(x_ref, g_ref, mod_ref, h_ref, mi)
        o_ref[...] = jnp.zeros_like(o_ref)

    h = h_ref[...]
    a = _dot(h, w1_ref[...])
    b = _dot(h, w3_ref[...])
    act = (a * (1.0 / (1.0 + jnp.exp(-a))) * b).astype(BF16)
    o_ref[...] += _dot(act, w2_ref[...])

    @pl.when(j == pl.num_programs(1) - 1)
    def _():
        rows = BF16_SUBLANES
        half_gate = 0.5 * mod_ref[0][mi + 2:mi + 3]

        def body(c, carry):
            r = pl.multiple_of(c * rows, rows)
            o_ref[pl.ds(r, rows), :] = x_ref[pl.ds(r, rows), :] + half_gate * o_ref[pl.ds(r, rows), :]
            return carry

        lax.fori_loop(0, o_ref.shape[0] // rows, body, 0)


def _ffn(tok, x, mods, g, w1, w3, w2, l, s, mi):
    m, d = x.shape
    f = w1.shape[-1]
    tm = tok.row_tile(512)
    tf = _tile(f, 256)
    return pl.pallas_call(
        functools.partial(_ffn_kernel, mi=mi),
        grid=(m // tm, f // tf),
        in_specs=[
            pl.BlockSpec((tm, d), lambda i, j: (i, 0)),
            pl.BlockSpec((1, N_MOD, d), lambda i, j: (tok.cond(i, tm), 0, 0)),
            pl.BlockSpec((1, d), lambda i, j: (0, 0)),
            pl.BlockSpec((None, None, d, tf), lambda i, j: (l, s, 0, j)),
            pl.BlockSpec((None, None, d, tf), lambda i, j: (l, s, 0, j)),
            pl.BlockSpec((None, None, tf, d), lambda i, j: (l, s, j, 0)),
        ],
        out_specs=pl.BlockSpec((tm, d), lambda i, j: (i, 0)),
        out_shape=jax.ShapeDtypeStruct((m, d), F32),
        scratch_shapes=[pltpu.VMEM((tm, d), BF16)],
        compiler_params=_params("parallel", "arbitrary"),
        name="ffn",
    )(x, mods, g.reshape(1, d), w1, w3, w2)


def _norm_mm_kernel(x_ref, mod_ref, g_ref, w_ref, o_ref, h_ref, *, mi):
    @pl.when(pl.program_id(1) == 0)
    def _():
        _fill_mod_norm(x_ref, g_ref, mod_ref, h_ref, mi)

    o_ref[...] = _dot(h_ref[...], w_ref[...])


def _norm_mm(tok, x, mods, g, w, e, mi):
    m, d = x.shape
    n = w.shape[-1]
    tm = tok.row_tile(512)
    tn = _tile(n, 512)
    return pl.pallas_call(
        functools.partial(_norm_mm_kernel, mi=mi),
        grid=(m // tm, n // tn),
        in_specs=[
            pl.BlockSpec((tm, d), lambda i, j: (i, 0)),
            pl.BlockSpec((1, N_MOD, d), lambda i, j: (tok.cond(i, tm), 0, 0)),
            pl.BlockSpec((1, d), lambda i, j: (0, 0)),
            pl.BlockSpec((None, d, tn), lambda i, j: (e, 0, j)),
        ],
        out_specs=pl.BlockSpec((tm, tn), lambda i, j: (i, j)),
        out_shape=jax.ShapeDtypeStruct((m, n), F32),
        scratch_shapes=[pltpu.VMEM((tm, d), BF16)],
        compiler_params=_params("parallel", "arbitrary"),
        name="in_proj",
    )(x, mods, g.reshape(1, d), w)


def _mod_norm_kernel(x_ref, mod_ref, g_ref, h_ref, *, mi):
    _fill_mod_norm(x_ref, g_ref, mod_ref, h_ref, mi)


def _mod_norm(tok, x, mods, g, mi):
    m, d = x.shape
    tm = tok.row_tile(256)
    return pl.pallas_call(
        functools.partial(_mod_norm_kernel, mi=mi),
        grid=(m // tm,),
        in_specs=[
            pl.BlockSpec((tm, d), lambda i: (i, 0)),
            pl.BlockSpec((1, N_MOD, d), lambda i: (tok.cond(i, tm), 0, 0)),
            pl.BlockSpec((1, d), lambda i: (0, 0)),
        ],
        out_specs=pl.BlockSpec((tm, d), lambda i: (i, 0)),
        out_shape=jax.ShapeDtypeStruct((m, d), BF16),
        compiler_params=_params("parallel"),
        name="mod_norm",
    )(x, mods, g.reshape(1, d))


def _mm_res_kernel(a_ref, w_ref, res_ref, mod_ref, o_ref, *, gi):
    o_ref[...] = res_ref[...] + mod_ref[0][gi:gi + 1] * _dot(a_ref[...], w_ref[...])


def _mm_res(tok, a, w, e, res, mods, gi):
    m, k = a.shape
    n = w.shape[-1]
    tm = tok.row_tile(512)
    tn = _tile(n, 1024)
    return pl.pallas_call(
        functools.partial(_mm_res_kernel, gi=gi),
        grid=(m // tm, n // tn),
        in_specs=[
            pl.BlockSpec((tm, k), lambda i, j: (i, 0)),
            pl.BlockSpec((None, k, tn), lambda i, j: (e, 0, j)),
            pl.BlockSpec((tm, tn), lambda i, j: (i, j)),
            pl.BlockSpec((1, N_MOD, tn), lambda i, j: (tok.cond(i, tm), 0, j)),
        ],
        out_specs=pl.BlockSpec((tm, tn), lambda i, j: (i, j)),
        out_shape=jax.ShapeDtypeStruct((m, n), F32),
        compiler_params=_params("parallel", "arbitrary"),
        name="out_proj",
    )(a, w, res, mods)


def _softmax_rows(s):
    e = jnp.exp(s - jnp.max(s, axis=-1, keepdims=True))
    return e / jnp.sum(e, axis=-1, keepdims=True)


def _ctx_attn_kernel(q_ref, k_ref, v_ref, o_ref, *, n_heads, scale):
    for h in range(n_heads):
        sl = slice(h * HEAD_DIM, (h + 1) * HEAD_DIM)
        q = q_ref[:, sl].astype(BF16)
        k = k_ref[:, sl].astype(BF16)
        v = v_ref[:, sl].astype(BF16)
        p = _softmax_rows(_dot_nt(q, k) * scale)
        o_ref[:, sl] = _dot(p.astype(BF16), v).astype(o_ref.dtype)


def _ctx_attn(tok, proj, width):
    l = tok.l_p
    return pl.pallas_call(
        functools.partial(_ctx_attn_kernel, n_heads=width // HEAD_DIM, scale=HEAD_DIM ** -0.5),
        grid=(tok.b_p,),
        in_specs=[pl.BlockSpec((l, width), lambda b, part=part: (b, part)) for part in range(3)],
        out_specs=pl.BlockSpec((l, width), lambda b: (b, 0)),
        out_shape=jax.ShapeDtypeStruct((tok.n_p, width), BF16),
        compiler_params=_params("parallel"),
        name="ctx_attn",
    )(proj, proj, proj)


def _na_attn_kernel(q_ref, k_ref, v_ref, ck_ref, cv_ref, bias_ref, o_ref, *, rows, kh, scale):
    ck = ck_ref[...].astype(BF16)
    cv = cv_ref[...].astype(BF16)
    for r in range(rows):
        start = min(max(r - kh // 2, 0), rows - kh)
        q = q_ref[r * GRID_W:(r + 1) * GRID_W, :].astype(BF16)
        kw = k_ref[start * GRID_W:(start + kh) * GRID_W, :].astype(BF16)
        vw = v_ref[start * GRID_W:(start + kh) * GRID_W, :].astype(BF16)
        s_lat = _dot_nt(q, kw) * scale + bias_ref[0, r]
        s_ctx = _dot_nt(q, ck) * scale
        mx = jnp.maximum(jnp.max(s_lat, axis=-1, keepdims=True), jnp.max(s_ctx, axis=-1, keepdims=True))
        e_lat = jnp.exp(s_lat - mx)
        e_ctx = jnp.exp(s_ctx - mx)
        den = jnp.sum(e_lat, axis=-1, keepdims=True) + jnp.sum(e_ctx, axis=-1, keepdims=True)
        o = _dot((e_lat / den).astype(BF16), vw) + _dot((e_ctx / den).astype(BF16), cv)
        o_ref[r * GRID_W:(r + 1) * GRID_W, :] = o.astype(o_ref.dtype)


def _na_bias(rpb, rows, kh):
    col = jnp.arange(GRID_W)
    col_start = jnp.clip(col - WIN_W // 2, 0, GRID_W - WIN_W)
    col_ok = (col[None, :] >= col_start[:, None]) & (col[None, :] < col_start[:, None] + WIN_W)
    dc_idx = jnp.clip(col[None, :] - col[:, None] + WIN_W - 1, 0, 2 * WIN_W - 2)
    r = jnp.arange(rows)
    start = jnp.clip(r - kh // 2, 0, rows - kh)
    dr_idx = start[:, None] + jnp.arange(kh)[None, :] - r[:, None] + WIN_H - 1
    b = rpb.astype(F32)[:, dr_idx][..., dc_idx]
    b = jnp.where(col_ok, b, NEG_INF)
    return jnp.transpose(b, (0, 1, 3, 2, 4)).reshape(rpb.shape[0], rows, GRID_W, kh * GRID_W)


def _na_attn(tok, proj, width, cache_k, cache_v, rpb):
    l = tok.l_s
    n_heads = width // HEAD_DIM
    rows = l // GRID_W
    kh = min(WIN_H, rows)
    past = cache_k.shape[0] // tok.b_s
    row0 = tok.n_p // l
    bias = _na_bias(rpb, rows, kh)
    qkv_specs = [pl.BlockSpec((l, HEAD_DIM), lambda b, h, part=part: (row0 + b, part * n_heads + h))
                 for part in range(3)]
    return pl.pallas_call(
        functools.partial(_na_attn_kernel, rows=rows, kh=kh, scale=HEAD_DIM ** -0.5),
        grid=(tok.b_s, n_heads),
        in_specs=qkv_specs + [
            pl.BlockSpec((past, HEAD_DIM), lambda b, h: (b, h)),
            pl.BlockSpec((past, HEAD_DIM), lambda b, h: (b, h)),
            pl.BlockSpec((1, rows, GRID_W, kh * GRID_W), lambda b, h: (h, 0, 0, 0)),
        ],
        out_specs=pl.BlockSpec((l, HEAD_DIM), lambda b, h: (b, h)),
        out_shape=jax.ShapeDtypeStruct((tok.n_s, width), BF16),
        compiler_params=_params("parallel", "arbitrary"),
        name="na_attn",
    )(proj, proj, proj, cache_k, cache_v, bias)


def _cos_sin(n_rows, n_cols, period):
    r = lax.broadcasted_iota(jnp.int32, (n_rows, n_cols), 0)
    c = lax.broadcasted_iota(jnp.int32, (n_rows, n_cols), 1)
    ang = ((r * c) % period).astype(F32) * (2.0 * math.pi / period)
    return jnp.cos(ang), jnp.sin(ang)


def _rfft_matrices(l):
    n = 2 * l
    cos, sin = _cos_sin(l, l, n)
    alt = jnp.where(lax.broadcasted_iota(jnp.int32, (l, l), 1) % 2 == 0, 1.0, -1.0).astype(F32)
    k_is0 = lax.broadcasted_iota(jnp.int32, (l, l), 0) == 0
    fwd_re = cos
    fwd_im = jnp.where(k_is0, alt, -sin)
    bin_is0 = lax.broadcasted_iota(jnp.int32, (l, l), 1) == 0
    inv_re = jnp.where(bin_is0, 1.0 / n, 2.0 / n) * cos
    inv_im = jnp.where(bin_is0, alt.T / n, (-2.0 / n) * sin)
    return [a.astype(BF16) for a in (fwd_re, fwd_im, inv_re, inv_im)]


def _hyena_filters(l, w1, b1, w2, b2, w3, freq):
    hp = lax.Precision.HIGHEST
    width = w3.shape[-1] // (HYENA_ORDER * 2)
    n_bands = (FILTER_EMB - 1) // 2
    t = jnp.linspace(0.0, 1.0, l, dtype=F32)[:, None]
    w = (2.0 * math.pi / l) * jnp.arange(l, dtype=F32)[:, None]
    bands = jnp.linspace(1e-4, n_bands - 1, n_bands, dtype=F32)[None, :]
    z = jnp.concatenate([t, jnp.cos(bands * w), -jnp.sin(bands * w)], axis=-1)
    hf = jnp.sin(freq[0].astype(F32) * (jnp.dot(z, w1.astype(F32), precision=hp) + b1.astype(F32)))
    hf = jnp.sin(freq[1].astype(F32) * (jnp.dot(hf, w2.astype(F32), precision=hp) + b2.astype(F32)))
    hf = jnp.dot(hf, w3.astype(F32), precision=hp).reshape(l, HYENA_ORDER, 2, width)
    deltas = jnp.abs(jnp.linspace(math.log(DECAY_TARGET) / FAST_DECAY_PCT,
                                  math.log(DECAY_TARGET) / SLOW_DECAY_PCT, width, dtype=F32))
    hf = hf * jnp.exp(-t[:, :, None, None] * deltas)
    h_fwd, h_bwd = hf[:, :, 0], hf[:, :, 1]
    filt = jnp.concatenate([h_fwd, jnp.zeros_like(h_fwd[:1]), h_bwd[1:][::-1]], axis=0)
    filt = filt * lax.rsqrt(jnp.sum(filt * filt, axis=0, keepdims=True) + EPS)
    k_f = jnp.fft.rfft(filt, axis=0)
    re, im = jnp.real(k_f), jnp.imag(k_f)
    first = (jnp.arange(l) == 0)[:, None, None]
    k_re = re[:l]
    k_im = jnp.where(first, 0.0, im[:l])
    k_re_nyq = jnp.where(first, re[l:l + 1], re[:l])
    return [(k_re[:, o], k_im[:, o], k_re_nyq[:, o]) for o in range(HYENA_ORDER)]


def _hyena_kernel(pv_ref, p1_ref, p2_ref, wv_ref, w1_ref, w2_ref, bv_ref, b1_ref, b2_ref,
                  fre_ref, fim_ref, gre_ref, gim_ref,
                  kr1_ref, ki1_ref, kn1_ref, kr2_ref, ki2_ref, kn2_ref, fbias_ref, o_ref):
    l, tc = o_ref.shape
    row = lax.broadcasted_iota(jnp.int32, (l, tc), 0)

    def short_conv(p_ref, w_ref, b_ref):
        p = p_ref[...]
        prev = jnp.where(row == 0, 0.0, pltpu.roll(p, 1, 0))
        nxt = jnp.where(row == l - 1, 0.0, pltpu.roll(p, l - 1, 0))
        w = w_ref[...]
        out = b_ref[...] + prev * w[0:1]
        out = out + p * w[1:2]
        return out + nxt * w[2:3]

    def long_conv(u, kr_ref, ki_ref, kn_ref, bias):
        ub = u.astype(BF16)
        u_re = _dot(fre_ref[...], ub)
        u_im = _dot(fim_ref[...], ub)
        kr, ki, kn = kr_ref[...], ki_ref[...], kn_ref[...]
        w_re = (u_re * kr - u_im * ki).astype(BF16)
        w_im = (u_re * ki + u_im * kn).astype(BF16)
        y = _dot(gre_ref[...], w_re) + _dot(gim_ref[...], w_im)
        return y + u * bias

    fbias = fbias_ref[...]
    v = short_conv(pv_ref, wv_ref, bv_ref)
    z = short_conv(p1_ref, w1_ref, b1_ref) * long_conv(v, kr1_ref, ki1_ref, kn1_ref, fbias[0:1])
    z = short_conv(p2_ref, w2_ref, b2_ref) * long_conv(z, kr2_ref, ki2_ref, kn2_ref, fbias[1:2])
    o_ref[...] = z.astype(o_ref.dtype)


def _hyena(proj, row0, n_seq, l, col0, width, e, w_short, b_short, filters, fbias):
    tc = _tile(width, 1024 if l <= 256 else 256)
    nc = width // tc
    c0 = col0 // tc
    mats = _rfft_matrices(l)
    p_specs = [pl.BlockSpec((l, tc), lambda c, b, part=part: (row0 + b, c0 + part * nc + c)) for part in range(3)]
    w_specs = [pl.BlockSpec((None, SHORT_CONV, tc), lambda c, b, part=part: (e, 0, part * nc + c))
               for part in range(3)]
    b_specs = [pl.BlockSpec((None, 1, tc), lambda c, b, part=part: (e, 0, part * nc + c)) for part in range(3)]
    mat_specs = [pl.BlockSpec((l, l), lambda c, b: (0, 0))] * 4
    k_specs = [pl.BlockSpec((l, tc), lambda c, b: (0, c))] * (3 * HYENA_ORDER)
    flat_filters = [a for per_order in filters for a in per_order]
    return pl.pallas_call(
        _hyena_kernel,
        grid=(nc, n_seq),
        in_specs=p_specs + w_specs + b_specs + mat_specs + k_specs
        + [pl.BlockSpec((None, HYENA_ORDER, tc), lambda c, b: (e, 0, c))],
        out_specs=pl.BlockSpec((l, tc), lambda c, b: (b, c)),
        out_shape=jax.ShapeDtypeStruct((n_seq * l, width), BF16),
        compiler_params=_params("parallel", "arbitrary"),
        name="hyena",
    )(proj, proj, proj, w_short, w_short, w_short, *([b_short.reshape(b_short.shape[0], 1, -1)] * 3),
      *mats, *flat_filters, fbias)


def _seq_dft_kernel(cos_ref, sin_ref, h_ref, a_ref, b_ref):
    h = h_ref[...]
    a_ref[...] = _dot(cos_ref[...], h).astype(a_ref.dtype)
    b_ref[...] = _dot(sin_ref[...], h).astype(b_ref.dtype)


def _seq_dft(h, row0, n_seq, l):
    d = h.shape[1]
    tn = _tile(d, 2048 if l <= 256 else 512)
    cos, sin = _cos_sin(l, l, l)
    cos = (cos * l ** -0.5).astype(BF16)
    sin = (sin * l ** -0.5).astype(BF16)
    out = jax.ShapeDtypeStruct((n_seq * l, d), BF16)
    return pl.pallas_call(
        _seq_dft_kernel,
        grid=(n_seq, d // tn),
        in_specs=[pl.BlockSpec((l, l), lambda b, j: (0, 0))] * 2 + [pl.BlockSpec((l, tn), lambda b, j: (row0 + b, j))],
        out_specs=[pl.BlockSpec((l, tn), lambda b, j: (b, j))] * 2,
        out_shape=[out, out],
        compiler_params=_params("parallel", "arbitrary"),
        name="seq_dft",
    )(cos, sin, h)


def _fnet_out_kernel(a_ref, b_ref, cos_ref, sin_ref, w_ref, res_ref, mod_ref, o_ref, mix_ref, *, gi, gw):
    @pl.when(pl.program_id(1) == 0)
    def _():
        cos = cos_ref[...]
        sin = sin_ref[...]
        for g in range(a_ref.shape[1] // gw):
            sl = slice(g * gw, (g + 1) * gw)
            mix_ref[:, sl] = (_dot(a_ref[:, sl], cos) - _dot(b_ref[:, sl], sin)).astype(mix_ref.dtype)

    o_ref[...] = res_ref[...] + mod_ref[0][gi:gi + 1] * _dot(mix_ref[...], w_ref[...])


def _fnet_out(tok, a, b, w, o, res, mods, gi):
    m, d = a.shape
    gw = d // FNET_GROUPS
    tm = tok.row_tile(512)
    tn = _tile(d, 512)
    cos, sin = _cos_sin(gw, gw, gw)
    cos = (cos * gw ** -0.5).astype(BF16)
    sin = (sin * gw ** -0.5).astype(BF16)
    return pl.pallas_call(
        functools.partial(_fnet_out_kernel, gi=gi, gw=gw),
        grid=(m // tm, d // tn),
        in_specs=[
            pl.BlockSpec((tm, d), lambda i, j: (i, 0)),
            pl.BlockSpec((tm, d), lambda i, j: (i, 0)),
            pl.BlockSpec((gw, gw), lambda i, j: (0, 0)),
            pl.BlockSpec((gw, gw), lambda i, j: (0, 0)),
            pl.BlockSpec((None, d, tn), lambda i, j: (o, 0, j)),
            pl.BlockSpec((tm, tn), lambda i, j: (i, j)),
            pl.BlockSpec((1, N_MOD, tn), lambda i, j: (tok.cond(i, tm), 0, j)),
        ],
        out_specs=pl.BlockSpec((tm, tn), lambda i, j: (i, j)),
        out_shape=jax.ShapeDtypeStruct((m, d), F32),
        scratch_shapes=[pltpu.VMEM((tm, d), BF16)],
        compiler_params=_params("parallel", "arbitrary"),
        name="fnet_out",
    )(a, b, cos, sin, w, res, mods)


def _rms_kernel(x_ref, g_ref, o_ref):
    rows = BF16_SUBLANES
    g = g_ref[...]

    def body(c, carry):
        r = pl.multiple_of(c * rows, rows)
        x = x_ref[pl.ds(r, rows), :]
        o_ref[pl.ds(r, rows), :] = x * lax.rsqrt(jnp.mean(x * x, axis=-1, keepdims=True) + EPS) * g
        return carry

    lax.fori_loop(0, x_ref.shape[0] // rows, body, 0)


def _rms_norm(x, g, row0, n_rows):
    d = x.shape[1]
    tm = _tile(math.gcd(row0, n_rows) if row0 else n_rows, 256, BF16_SUBLANES)
    return pl.pallas_call(
        _rms_kernel,
        grid=(n_rows // tm,),
        in_specs=[pl.BlockSpec((tm, d), lambda i: (row0 // tm + i, 0)), pl.BlockSpec((1, d), lambda i: (0, 0))],
        out_specs=pl.BlockSpec((tm, d), lambda i: (i, 0)),
        out_shape=jax.ShapeDtypeStruct((n_rows, d), F32),
        compiler_params=_params("parallel"),
        name="final_norm",
    )(x, g.reshape(1, d))


def kernel(x_prompt, x_sample, cache_k, cache_v, c, c_ctx, w_mod, b_mod, norm_g, ffn_w1, ffn_w3, ffn_w2,
           w_in, w_out, rpb, w_short, b_short, filt_w1, filt_b1, filt_w2, filt_b2, filt_w3, filt_freq,
           filt_bias, w_fnet, final_g):
    b_p, l_p, d = x_prompt.shape
    b_s, l_s, _ = x_sample.shape
    depth = w_mod.shape[0]
    tok = _Tokens(b_p, l_p, b_s, l_s)
    attn_w = cache_k.shape[-2] * cache_k.shape[-1]
    hyena_w = filt_bias.shape[-1]
    n_heads = attn_w // HEAD_DIM
    past = cache_k.shape[2]

    x = jnp.concatenate([x_prompt.reshape(tok.n_p, d), x_sample.reshape(tok.n_s, d)], axis=0)
    conds = jnp.concatenate([c_ctx[None, :], c, jnp.zeros((COND_ROWS - 1 - b_s, d), F32)], axis=0)
    w1b, w3b, w2b = ffn_w1.astype(BF16), ffn_w3.astype(BF16), ffn_w2.astype(BF16)
    w_in_b, w_out_b, w_fnet_b = w_in.astype(BF16), w_out.astype(BF16), w_fnet.astype(BF16)

    new_k, new_v = [], []
    for l in range(depth):
        mods = _adaln(conds, w_mod, b_mod, l)
        x = _ffn(tok, x, mods, norm_g[l, 0], w1b, w3b, w2b, l, 0, 0)
        if l % 2 == 0:
            e = l // 2
            proj = _norm_mm(tok, x, mods, norm_g[l, 1], w_in_b, e, 3)
            a_p = _ctx_attn(tok, proj, attn_w)
            a_s = _na_attn(tok, proj, attn_w, cache_k[:, e].reshape(b_s * past, attn_w),
                           cache_v[:, e].reshape(b_s * past, attn_w), rpb[e])
            hy = []
            for row0, n_seq, ln in ((0, b_p, l_p), (tok.n_p // l_s, b_s, l_s)):
                filters = _hyena_filters(ln, filt_w1[e], filt_b1[e], filt_w2[e], filt_b2[e], filt_w3[e],
                                         filt_freq[e])
                hy.append(_hyena(proj, row0, n_seq, ln, 3 * attn_w, hyena_w, e, w_short, b_short, filters,
                                 filt_bias))
            mixed = jnp.concatenate([jnp.concatenate([a_p, a_s], axis=0), jnp.concatenate(hy, axis=0)], axis=1)
            x = _mm_res(tok, mixed, w_out_b, e, x, mods, 5)
            new_k.append(proj[:tok.n_p, attn_w:2 * attn_w].reshape(b_p, l_p, n_heads, HEAD_DIM))
            new_v.append(proj[:tok.n_p, 2 * attn_w:3 * attn_w].reshape(b_p, l_p, n_heads, HEAD_DIM))
        else:
            o = l // 2
            h = _mod_norm(tok, x, mods, norm_g[l, 1], 3)
            a_p, s_p = _seq_dft(h, 0, b_p, l_p)
            a_s, s_s = _seq_dft(h, tok.n_p // l_s, b_s, l_s)
            x = _fnet_out(tok, jnp.concatenate([a_p, a_s], axis=0), jnp.concatenate([s_p, s_s], axis=0),
                          w_fnet_b, o, x, mods, 5)
        x = _ffn(tok, x, mods, norm_g[l, 2], w1b, w3b, w2b, l, 1, 6)

    y_prompt = _rms_norm(x, final_g, 0, tok.n_p).reshape(b_p, l_p, d)
    y_sample = _rms_norm(x, final_g, tok.n_p, tok.n_s).reshape(b_s, l_s, d)
    return y_prompt, y_sample, jnp.stack(new_k, axis=1), jnp.stack(new_v, axis=1)
```

```python
import dataclasses
import functools
import math

import jax
import jax.numpy as jnp
from jax import lax
from jax.experimental import pallas as pl
from jax.experimental.pallas import tpu as pltpu

F32 = jnp.float32
BF16 = jnp.bfloat16
HIGHEST = lax.Precision.HIGHEST

N_MOD = 9
GRID_W = 64
WIN_H = 8
WIN_W = 16
HEAD_DIM = 128
SHORT_CONV = 3
FILTER_EMB = 33
HYENA_ORDER = 2
DECAY_TARGET = 1e-2
FAST_DECAY_PCT = 0.3
SLOW_DECAY_PCT = 1.5
FNET_GROUPS = 8
EPS = 1e-6
NEG_INF = -1e30

V7X_VMEM_BYTES = 64 * 2**20
VMEM_LIMIT = V7X_VMEM_BYTES - 8 * 2**20
LANE = 128
BF16_SUBLANES = 16
COND_ROWS = 16


def _tile(n, pref, unit=LANE):
    if n <= pref:
        return n
    t = (pref // unit) * unit
    while t > unit and n % t:
        t -= unit
    assert n % t == 0, (n, pref)
    return t


def _params(*sem):
    return pltpu.CompilerParams(dimension_semantics=sem, vmem_limit_bytes=VMEM_LIMIT)


@dataclasses.dataclass(frozen=True)
class _Stream:
    n_seq: int
    l: int
    cond0: int
    shared: bool

    @property
    def n(self):
        return self.n_seq * self.l

    def row_tile(self, pref):
        return _tile(self.n if self.shared else self.l, pref, BF16_SUBLANES)

    def cond(self, i, tm):
        return self.cond0 if self.shared else self.cond0 + i // (self.l // tm)


def _fill_mod_norm(x_ref, g_ref, mod_ref, h_ref, rs_ref, mi):
    rows = BF16_SUBLANES
    tm, d = x_ref.shape
    cw = _tile(d, 512)

    def stats(c, carry):
        r = pl.multiple_of(c * rows, rows)
        x = x_ref[pl.ds(r, rows), :]
        rs = lax.rsqrt(jnp.mean(x * x, axis=-1, keepdims=True) + EPS)
        rs_ref[pl.ds(r, rows), :] = jnp.broadcast_to(rs, (rows, LANE))
        return carry

    lax.fori_loop(0, tm // rows, stats, 0, unroll=4)

    m = mod_ref[0]
    for t in range(d // cw):
        cs = slice(t * cw, (t + 1) * cw)
        gain = g_ref[:, cs] * (1.0 + m[mi + 1:mi + 2, cs])
        shift = m[mi:mi + 1, cs]

        def apply(c, carry):
            r = pl.multiple_of(c * rows, rows)
            rs = rs_ref[pl.ds(r, rows), :]
            rs = jnp.concatenate([rs] * (cw // LANE), axis=1) if cw > LANE else rs[:, :cw]
            h_ref[pl.ds(r, rows), cs] = (x_ref[pl.ds(r, rows), cs] * rs * gain + shift).astype(h_ref.dtype)
            return carry

        lax.fori_loop(0, tm // rows, apply, 0)


def _dot(a, b):
    return jnp.dot(a, b, preferred_element_type=F32)


def _dot_hp(a, b):
    return jnp.dot(a, b, preferred_element_type=F32, precision=HIGHEST)


def _dot_nt(a, b):
    return lax.dot_general(a, b, (((1,), (1,)), ((), ())), preferred_element_type=F32)


def _adaln_kernel(c_ref, w_ref, b_ref, o_ref):
    c = c_ref[...]
    s = c * (1.0 / (1.0 + jnp.exp(-c)))
    o_ref[...] = _dot_hp(s, w_ref[...]) + b_ref[...]


def _adaln(conds, w_mod, b_mod, l):
    d = conds.shape[1]
    n = w_mod.shape[-1]
    tn = _tile(n, 1024)
    out = pl.pallas_call(
        _adaln_kernel,
        grid=(n // tn,),
        in_specs=[
            pl.BlockSpec((COND_ROWS, d), lambda j: (0, 0)),
            pl.BlockSpec((None, d, tn), lambda j: (l, 0, j)),
            pl.BlockSpec((None, 1, tn), lambda j: (l, 0, j)),
        ],
        out_specs=pl.BlockSpec((COND_ROWS, tn), lambda j: (0, j)),
        out_shape=jax.ShapeDtypeStruct((COND_ROWS, n), F32),
        compiler_params=_params("arbitrary"),
        name="adaln",
    )(conds, w_mod, b_mod.reshape(b_mod.shape[0], 1, n))
    return out.reshape(COND_ROWS, N_MOD, d)


def _ffn_kernel(x_ref, mod_ref, g_ref, w1_ref, w3_ref, w2_ref, o_ref, h_ref, rs_ref, *, mi):
    j = pl.program_id(1)

    @pl.when(j == 0)
    def _():
        _fill_mod_norm(x_ref, g_ref, mod_ref, h_ref, rs_ref, mi)
        o_ref[...] = jnp.zeros_like(o_ref)

    h = h_ref[...]
    a = _dot(h, w1_ref[...])
    b = _dot(h, w3_ref[...])
    act = (a * (1.0 / (1.0 + jnp.exp(-a))) * b).astype(BF16)
    o_ref[...] += _dot(act, w2_ref[...])

    @pl.when(j == pl.num_programs(1) - 1)
    def _():
        rows = BF16_SUBLANES
        half_gate = 0.5 * mod_ref[0][mi + 2:mi + 3]

        def body(c, carry):
            r = pl.multiple_of(c * rows, rows)
            o_ref[pl.ds(r, rows), :] = x_ref[pl.ds(r, rows), :] + half_gate * o_ref[pl.ds(r, rows), :]
            return carry

        lax.fori_loop(0, o_ref.shape[0] // rows, body, 0)


def _ffn(st, x, mods, g, w1, w3, w2, l, s, mi):
    m, d = x.shape
    f = w1.shape[-1]
    tm = st.row_tile(512)
    tf = _tile(f, 256)
    return pl.pallas_call(
        functools.partial(_ffn_kernel, mi=mi),
        grid=(m // tm, f // tf),
        in_specs=[
            pl.BlockSpec((tm, d), lambda i, j: (i, 0)),
            pl.BlockSpec((1, N_MOD, d), lambda i, j: (st.cond(i, tm), 0, 0)),
            pl.BlockSpec((1, d), lambda i, j: (0, 0)),
            pl.BlockSpec((None, None, d, tf), lambda i, j: (l, s, 0, j)),
            pl.BlockSpec((None, None, d, tf), lambda i, j: (l, s, 0, j)),
            pl.BlockSpec((None, None, tf, d), lambda i, j: (l, s, j, 0)),
        ],
        out_specs=pl.BlockSpec((tm, d), lambda i, j: (i, 0)),
        out_shape=jax.ShapeDtypeStruct((m, d), F32),
        scratch_shapes=[pltpu.VMEM((tm, d), BF16), pltpu.VMEM((tm, LANE), F32)],
        compiler_params=_params("parallel", "arbitrary"),
        name="ffn",
    )(x, mods, g.reshape(1, d), w1, w3, w2)


def _norm_mm_kernel(x_ref, mod_ref, g_ref, w_ref, o_ref, h_ref, rs_ref, *, mi):
    @pl.when(pl.program_id(1) == 0)
    def _():
        _fill_mod_norm(x_ref, g_ref, mod_ref, h_ref, rs_ref, mi)

    o_ref[...] = _dot(h_ref[...], w_ref[...])


def _norm_mm(st, x, mods, g, w, e, mi):
    m, d = x.shape
    n = w.shape[-1]
    tm = st.row_tile(512)
    tn = _tile(n, 1024)
    return pl.pallas_call(
        functools.partial(_norm_mm_kernel, mi=mi),
        grid=(m // tm, n // tn),
        in_specs=[
            pl.BlockSpec((tm, d), lambda i, j: (i, 0)),
            pl.BlockSpec((1, N_MOD, d), lambda i, j: (st.cond(i, tm), 0, 0)),
            pl.BlockSpec((1, d), lambda i, j: (0, 0)),
            pl.BlockSpec((None, d, tn), lambda i, j: (e, 0, j)),
        ],
        out_specs=pl.BlockSpec((tm, tn), lambda i, j: (i, j)),
        out_shape=jax.ShapeDtypeStruct((m, n), F32),
        scratch_shapes=[pltpu.VMEM((tm, d), BF16), pltpu.VMEM((tm, LANE), F32)],
        compiler_params=_params("parallel", "arbitrary"),
        name="in_proj",
    )(x, mods, g.reshape(1, d), w)


def _mod_norm_kernel(x_ref, mod_ref, g_ref, h_ref, rs_ref, *, mi):
    _fill_mod_norm(x_ref, g_ref, mod_ref, h_ref, rs_ref, mi)


def _mod_norm(st, x, mods, g, mi):
    m, d = x.shape
    tm = st.row_tile(256)
    return pl.pallas_call(
        functools.partial(_mod_norm_kernel, mi=mi),
        grid=(m // tm,),
        in_specs=[
            pl.BlockSpec((tm, d), lambda i: (i, 0)),
            pl.BlockSpec((1, N_MOD, d), lambda i: (st.cond(i, tm), 0, 0)),
            pl.BlockSpec((1, d), lambda i: (0, 0)),
        ],
        out_specs=pl.BlockSpec((tm, d), lambda i: (i, 0)),
        out_shape=jax.ShapeDtypeStruct((m, d), BF16),
        scratch_shapes=[pltpu.VMEM((tm, LANE), F32)],
        compiler_params=_params("parallel"),
        name="mod_norm",
    )(x, mods, g.reshape(1, d))


def _mm_res_kernel(a0_ref, a1_ref, w0_ref, w1_ref, res_ref, mod_ref, o_ref, *, gi):
    mix = _dot(a0_ref[...], w0_ref[...]) + _dot(a1_ref[...], w1_ref[...])
    o_ref[...] = res_ref[...] + mod_ref[0][gi:gi + 1] * mix


def _mm_res(st, a0, a1, w, e, res, mods, gi):
    m, k = a0.shape
    assert a1.shape == (m, k) and w.shape[1] == 2 * k
    n = w.shape[-1]
    tm = st.row_tile(512)
    tn = _tile(n, 1024)
    return pl.pallas_call(
        functools.partial(_mm_res_kernel, gi=gi),
        grid=(m // tm, n // tn),
        in_specs=[
            pl.BlockSpec((tm, k), lambda i, j: (i, 0)),
            pl.BlockSpec((tm, k), lambda i, j: (i, 0)),
            pl.BlockSpec((None, k, tn), lambda i, j: (e, 0, j)),
            pl.BlockSpec((None, k, tn), lambda i, j: (e, 1, j)),
            pl.BlockSpec((tm, tn), lambda i, j: (i, j)),
            pl.BlockSpec((1, N_MOD, tn), lambda i, j: (st.cond(i, tm), 0, j)),
        ],
        out_specs=pl.BlockSpec((tm, tn), lambda i, j: (i, j)),
        out_shape=jax.ShapeDtypeStruct((m, n), F32),
        compiler_params=_params("parallel", "arbitrary"),
        name="out_proj",
    )(a0, a1, w, w, res, mods)


def _ctx_attn_kernel(q_ref, k_ref, v_ref, o_ref, *, n_heads, scale):
    for h in range(n_heads):
        sl = slice(h * HEAD_DIM, (h + 1) * HEAD_DIM)
        q = q_ref[:, sl].astype(BF16)
        k = k_ref[:, sl].astype(BF16)
        v = v_ref[:, sl].astype(BF16)
        s = _dot_nt(q, k) * scale
        e = jnp.exp(s - jnp.max(s, axis=-1, keepdims=True))
        p = e * (1.0 / jnp.sum(e, axis=-1, keepdims=True))
        o_ref[:, sl] = _dot(p.astype(BF16), v).astype(o_ref.dtype)


def _ctx_attn(st, proj, width):
    l = st.l
    return pl.pallas_call(
        functools.partial(_ctx_attn_kernel, n_heads=width // HEAD_DIM, scale=HEAD_DIM ** -0.5),
        grid=(st.n_seq,),
        in_specs=[pl.BlockSpec((l, width), lambda b, part=part: (b, part)) for part in range(3)],
        out_specs=pl.BlockSpec((l, width), lambda b: (b, 0)),
        out_shape=jax.ShapeDtypeStruct((st.n, width), BF16),
        compiler_params=_params("parallel"),
        name="ctx_attn",
    )(proj, proj, proj)


def _na_window(r, rows, kh):
    start = min(max(r - kh // 2, 0), rows - kh)
    lo = start - start % 2
    hi = start + kh + (start + kh) % 2
    return start, lo, hi


def _na_attn_kernel(q_ref, k_ref, v_ref, ck_ref, cv_ref, tab_ref, o_ref, pl_ref, pc_ref, *, rows, kh, scale):
    q = q_ref[...].astype(BF16)
    s_lat_all = _dot_nt(q, k_ref[...].astype(BF16))
    s_ctx_all = _dot_nt(q, ck_ref[...].astype(BF16))
    for r in range(rows):
        start, lo, hi = _na_window(r, rows, kh)
        qs = slice(r * GRID_W, (r + 1) * GRID_W)
        pieces = []
        for p in range(lo // 2, hi // 2):
            ok0 = start <= 2 * p < start + kh
            ok1 = start <= 2 * p + 1 < start + kh
            dr = 2 * p - r + WIN_H - 1
            pieces.append(tab_ref[0, 0, dr] if ok0 and ok1 else tab_ref[0, 1, dr] if ok0 else tab_ref[0, 2, dr + 1])
        s_lat = s_lat_all[qs, lo * GRID_W:hi * GRID_W] * scale + jnp.concatenate(pieces, axis=1)
        s_ctx = s_ctx_all[qs, :] * scale
        mx = jnp.maximum(jnp.max(s_lat, axis=-1, keepdims=True), jnp.max(s_ctx, axis=-1, keepdims=True))
        e_lat = jnp.exp(s_lat - mx)
        e_ctx = jnp.exp(s_ctx - mx)
        inv = 1.0 / (jnp.sum(e_lat, axis=-1, keepdims=True) + jnp.sum(e_ctx, axis=-1, keepdims=True))
        p_lat = (e_lat * inv).astype(BF16)
        parts = []
        if lo > 0:
            parts.append(jnp.zeros((GRID_W, lo * GRID_W), BF16))
        parts.append(p_lat)
        if hi < rows:
            parts.append(jnp.zeros((GRID_W, (rows - hi) * GRID_W), BF16))
        pl_ref[qs, :] = jnp.concatenate(parts, axis=1) if len(parts) > 1 else p_lat
        pc_ref[qs, :] = (e_ctx * inv).astype(BF16)
    o = _dot(pl_ref[...], v_ref[...].astype(BF16)) + _dot(pc_ref[...], cv_ref[...].astype(BF16))
    o_ref[...] = o.astype(o_ref.dtype)


def _na_bias_table(rpb):
    col = jnp.arange(GRID_W)
    col_start = jnp.clip(col - WIN_W // 2, 0, GRID_W - WIN_W)
    col_ok = (col[None, :] >= col_start[:, None]) & (col[None, :] < col_start[:, None] + WIN_W)
    dc_idx = jnp.clip(col[None, :] - col[:, None] + WIN_W - 1, 0, 2 * WIN_W - 2)
    t = jnp.where(col_ok, rpb.astype(F32)[:, :, dc_idx], NEG_INF)
    neg = jnp.full_like(t, NEG_INF)
    nxt = jnp.concatenate([t[:, 1:], neg[:, :1]], axis=1)
    return jnp.stack([jnp.concatenate([t, nxt], axis=-1), jnp.concatenate([t, neg], axis=-1),
                      jnp.concatenate([neg, t], axis=-1)], axis=1)


def _na_attn(st, proj, width, cache_k, cache_v, rpb):
    l = st.l
    n_heads = width // HEAD_DIM
    rows = l // GRID_W
    kh = min(WIN_H, rows)
    assert rows % 2 == 0 and kh % 2 == 0
    past = cache_k.shape[0] // st.n_seq
    tab = _na_bias_table(rpb)
    qkv_specs = [pl.BlockSpec((l, HEAD_DIM), lambda b, h, part=part: (b, part * n_heads + h)) for part in range(3)]
    return pl.pallas_call(
        functools.partial(_na_attn_kernel, rows=rows, kh=kh, scale=HEAD_DIM ** -0.5),
        grid=(st.n_seq, n_heads),
        in_specs=qkv_specs + [
            pl.BlockSpec((past, HEAD_DIM), lambda b, h: (b, h)),
            pl.BlockSpec((past, HEAD_DIM), lambda b, h: (b, h)),
            pl.BlockSpec((1,) + tab.shape[1:], lambda b, h: (h, 0, 0, 0, 0)),
        ],
        out_specs=pl.BlockSpec((l, HEAD_DIM), lambda b, h: (b, h)),
        out_shape=jax.ShapeDtypeStruct((st.n, width), BF16),
        scratch_shapes=[pltpu.VMEM((l, l), BF16), pltpu.VMEM((l, past), BF16)],
        compiler_params=_params("parallel", "arbitrary"),
        name="na_attn",
    )(proj, proj, proj, cache_k, cache_v, tab)


def _cos_sin(n_rows, n_cols, period):
    r = lax.broadcasted_iota(jnp.int32, (n_rows, n_cols), 0)
    c = lax.broadcasted_iota(jnp.int32, (n_rows, n_cols), 1)
    ang = ((r * c) % period).astype(F32) * (2.0 * math.pi / period)
    return jnp.cos(ang), jnp.sin(ang)


def _rfft_matrices(l):
    n = 2 * l
    cos, sin = _cos_sin(l, l, n)
    alt = jnp.where(lax.broadcasted_iota(jnp.int32, (l, l), 1) % 2 == 0, 1.0, -1.0).astype(F32)
    k_is0 = lax.broadcasted_iota(jnp.int32, (l, l), 0) == 0
    fwd_re = cos
    fwd_im = jnp.where(k_is0, alt, -sin)
    bin_is0 = lax.broadcasted_iota(jnp.int32, (l, l), 1) == 0
    inv_re = jnp.where(bin_is0, 1.0 / n, 2.0 / n) * cos
    inv_im = jnp.where(bin_is0, alt.T / n, (-2.0 / n) * sin)
    return fwd_re, fwd_im, inv_re, inv_im


def _filter_kernel(z_ref, t_ref, w1_ref, b1_ref, w2_ref, b2_ref, fq_ref, w3f_ref, w3b_ref, dl_ref,
                   fre_ref, fim_ref, kr_ref, ki_ref, kn_ref):
    fq = fq_ref[...]
    hid = jnp.sin(fq[0:1] * (_dot_hp(z_ref[...], w1_ref[...]) + b1_ref[...]))
    hid = jnp.sin(fq[1:2] * (_dot_hp(hid, w2_ref[...]) + b2_ref[...]))
    decay = jnp.exp(-t_ref[...] * dl_ref[...])
    row = lax.broadcasted_iota(jnp.int32, kr_ref.shape, 0)
    h_fwd = _dot_hp(hid, w3f_ref[...]) * decay
    h_bwd = jnp.where(row == 0, 0.0, _dot_hp(hid, w3b_ref[...]) * decay)
    nrm = lax.rsqrt(jnp.sum(h_fwd * h_fwd + h_bwd * h_bwd, axis=0, keepdims=True) + EPS)
    even = (h_fwd + h_bwd) * nrm
    odd = (h_fwd - h_bwd) * nrm
    k_re = _dot_hp(fre_ref[...], even)
    k_im = _dot_hp(fim_ref[...], odd)
    nyq = jnp.sum(jnp.where(row % 2 == 0, even, -even), axis=0, keepdims=True)
    kr_ref[...] = k_re
    ki_ref[...] = jnp.where(row == 0, 0.0, k_im)
    kn_ref[...] = jnp.where(row == 0, nyq, k_re)


def _pad_to(a, shape):
    return jnp.pad(a.astype(F32), [(0, s - n) for n, s in zip(a.shape, shape)])


def _hyena_filters(l, width, w1, b1, w2, b2, w3, freq, fwd_re, fwd_im):
    hid = w2.shape[0]
    hp = LANE * pl.cdiv(hid, LANE)
    n_bands = (FILTER_EMB - 1) // 2
    t = jnp.linspace(0.0, 1.0, l, dtype=F32)[:, None]
    w = (2.0 * math.pi / l) * jnp.arange(l, dtype=F32)[:, None]
    bands = jnp.linspace(1e-4, n_bands - 1, n_bands, dtype=F32)[None, :]
    z = _pad_to(jnp.concatenate([t, jnp.cos(bands * w), -jnp.sin(bands * w)], axis=-1), (l, LANE))
    deltas = jnp.abs(jnp.linspace(math.log(DECAY_TARGET) / FAST_DECAY_PCT,
                                  math.log(DECAY_TARGET) / SLOW_DECAY_PCT, width, dtype=F32))[None, :]
    tc = _tile(width, 512)
    nc = width // tc
    small = lambda shape: pl.BlockSpec(shape, lambda o, c: (0,) * len(shape))
    out = jax.ShapeDtypeStruct((HYENA_ORDER, l, width), F32)
    return pl.pallas_call(
        _filter_kernel,
        grid=(HYENA_ORDER, nc),
        in_specs=[small((l, LANE)), small((l, 1)), small((LANE, hp)), small((1, hp)), small((hp, hp)),
                  small((1, hp)), small((2, hp)),
                  pl.BlockSpec((hp, tc), lambda o, c: (0, 2 * o * nc + c)),
                  pl.BlockSpec((hp, tc), lambda o, c: (0, (2 * o + 1) * nc + c)),
                  pl.BlockSpec((1, tc), lambda o, c: (0, c)),
                  small((l, l)), small((l, l))],
        out_specs=[pl.BlockSpec((None, l, tc), lambda o, c: (o, 0, c))] * 3,
        out_shape=[out, out, out],
        compiler_params=_params("arbitrary", "arbitrary"),
        name="hyena_filters",
    )(z, t, _pad_to(w1, (LANE, hp)), _pad_to(b1[None, :], (1, hp)), _pad_to(w2, (hp, hp)),
      _pad_to(b2[None, :], (1, hp)), _pad_to(freq, (2, hp)), _pad_to(w3, (hp, w3.shape[1])),
      _pad_to(w3, (hp, w3.shape[1])), deltas, fwd_re, fwd_im)


def _hyena_kernel(pv_ref, p1_ref, p2_ref, wv_ref, w1_ref, w2_ref, bv_ref, b1_ref, b2_ref,
                  fre_ref, fim_ref, gre_ref, gim_ref, kr_ref, ki_ref, kn_ref, fbias_ref, o_ref):
    l, tc = o_ref.shape
    row = lax.broadcasted_iota(jnp.int32, (l, tc), 0)

    def short_conv(p_ref, w_ref, b_ref):
        p = p_ref[...]
        prev = jnp.where(row == 0, 0.0, pltpu.roll(p, 1, 0))
        nxt = jnp.where(row == l - 1, 0.0, pltpu.roll(p, l - 1, 0))
        w = w_ref[...]
        out = b_ref[...] + prev * w[0:1]
        out = out + p * w[1:2]
        return out + nxt * w[2:3]

    def long_conv(u, o, bias):
        ub = u.astype(BF16)
        u_re = _dot(fre_ref[...], ub)
        u_im = _dot(fim_ref[...], ub)
        kr, ki, kn = kr_ref[o], ki_ref[o], kn_ref[o]
        w_re = (u_re * kr - u_im * ki).astype(BF16)
        w_im = (u_re * ki + u_im * kn).astype(BF16)
        y = _dot(gre_ref[...], w_re) + _dot(gim_ref[...], w_im)
        return y + u * bias

    fbias = fbias_ref[...]
    v = short_conv(pv_ref, wv_ref, bv_ref)
    z = short_conv(p1_ref, w1_ref, b1_ref) * long_conv(v, 0, fbias[0:1])
    z = short_conv(p2_ref, w2_ref, b2_ref) * long_conv(z, 1, fbias[1:2])
    o_ref[...] = z.astype(o_ref.dtype)


def _hyena(st, proj, col0, width, e, w_short, b_short, mats, filters, fbias):
    l = st.l
    tc = _tile(width, 1024 if l <= 256 else 256)
    nc = width // tc
    c0 = col0 // tc
    p_specs = [pl.BlockSpec((l, tc), lambda c, b, part=part: (b, c0 + part * nc + c)) for part in range(3)]
    w_specs = [pl.BlockSpec((None, SHORT_CONV, tc), lambda c, b, part=part: (e, 0, part * nc + c))
               for part in range(3)]
    b_specs = [pl.BlockSpec((None, 1, tc), lambda c, b, part=part: (e, 0, part * nc + c)) for part in range(3)]
    mat_specs = [pl.BlockSpec((l, l), lambda c, b: (0, 0))] * 4
    k_specs = [pl.BlockSpec((HYENA_ORDER, l, tc), lambda c, b: (0, 0, c))] * 3
    return pl.pallas_call(
        _hyena_kernel,
        grid=(nc, st.n_seq),
        in_specs=p_specs + w_specs + b_specs + mat_specs + k_specs
        + [pl.BlockSpec((None, HYENA_ORDER, tc), lambda c, b: (e, 0, c))],
        out_specs=pl.BlockSpec((l, tc), lambda c, b: (b, c)),
        out_shape=jax.ShapeDtypeStruct((st.n, width), BF16),
        compiler_params=_params("parallel", "arbitrary"),
        name="hyena",
    )(proj, proj, proj, w_short, w_short, w_short, *([b_short.reshape(b_short.shape[0], 1, -1)] * 3),
      *[a.astype(BF16) for a in mats], *filters, fbias)


def _seq_dft_kernel(cos_ref, sin_ref, h_ref, a_ref, b_ref):
    h = h_ref[...]
    a_ref[...] = _dot(cos_ref[...], h).astype(a_ref.dtype)
    b_ref[...] = _dot(sin_ref[...], h).astype(b_ref.dtype)


def _seq_dft(st, h):
    l = st.l
    d = h.shape[1]
    tn = _tile(d, 2048 if l <= 256 else 512)
    cos, sin = _cos_sin(l, l, l)
    cos = (cos * l ** -0.5).astype(BF16)
    sin = (sin * l ** -0.5).astype(BF16)
    out = jax.ShapeDtypeStruct((st.n, d), BF16)
    return pl.pallas_call(
        _seq_dft_kernel,
        grid=(st.n_seq, d // tn),
        in_specs=[pl.BlockSpec((l, l), lambda b, j: (0, 0))] * 2 + [pl.BlockSpec((l, tn), lambda b, j: (b, j))],
        out_specs=[pl.BlockSpec((l, tn), lambda b, j: (b, j))] * 2,
        out_shape=[out, out],
        compiler_params=_params("parallel", "arbitrary"),
        name="seq_dft",
    )(cos, sin, h)


def _fnet_out_kernel(a_ref, b_ref, cos_ref, sin_ref, w_ref, res_ref, mod_ref, o_ref, mix_ref, *, gi, gw):
    @pl.when(pl.program_id(1) == 0)
    def _():
        cos = cos_ref[...]
        sin = sin_ref[...]
        for g in range(a_ref.shape[1] // gw):
            sl = slice(g * gw, (g + 1) * gw)
            mix_ref[:, sl] = (_dot(a_ref[:, sl], cos) - _dot(b_ref[:, sl], sin)).astype(mix_ref.dtype)

    o_ref[...] = res_ref[...] + mod_ref[0][gi:gi + 1] * _dot(mix_ref[...], w_ref[...])


def _fnet_out(st, a, b, w, o, res, mods, gi):
    m, d = a.shape
    gw = d // FNET_GROUPS
    tm = st.row_tile(512)
    tn = _tile(d, 512)
    cos, sin = _cos_sin(gw, gw, gw)
    cos = (cos * gw ** -0.5).astype(BF16)
    sin = (sin * gw ** -0.5).astype(BF16)
    return pl.pallas_call(
        functools.partial(_fnet_out_kernel, gi=gi, gw=gw),
        grid=(m // tm, d // tn),
        in_specs=[
            pl.BlockSpec((tm, d), lambda i, j: (i, 0)),
            pl.BlockSpec((tm, d), lambda i, j: (i, 0)),
            pl.BlockSpec((gw, gw), lambda i, j: (0, 0)),
            pl.BlockSpec((gw, gw), lambda i, j: (0, 0)),
            pl.BlockSpec((None, d, tn), lambda i, j: (o, 0, j)),
            pl.BlockSpec((tm, tn), lambda i, j: (i, j)),
            pl.BlockSpec((1, N_MOD, tn), lambda i, j: (st.cond(i, tm), 0, j)),
        ],
        out_specs=pl.BlockSpec((tm, tn), lambda i, j: (i, j)),
        out_shape=jax.ShapeDtypeStruct((m, d), F32),
        scratch_shapes=[pltpu.VMEM((tm, d), BF16)],
        compiler_params=_params("parallel", "arbitrary"),
        name="fnet_out",
    )(a, b, cos, sin, w, res, mods)


def _rms_kernel(x_ref, g_ref, o_ref):
    rows = BF16_SUBLANES
    g = g_ref[...]

    def body(c, carry):
        r = pl.multiple_of(c * rows, rows)
        x = x_ref[pl.ds(r, rows), :]
        o_ref[pl.ds(r, rows), :] = x * lax.rsqrt(jnp.mean(x * x, axis=-1, keepdims=True) + EPS) * g
        return carry

    lax.fori_loop(0, x_ref.shape[0] // rows, body, 0)


def _rms_norm(x, g):
    m, d = x.shape
    tm = _tile(m, 256, BF16_SUBLANES)
    return pl.pallas_call(
        _rms_kernel,
        grid=(m // tm,),
        in_specs=[pl.BlockSpec((tm, d), lambda i: (i, 0)), pl.BlockSpec((1, d), lambda i: (0, 0))],
        out_specs=pl.BlockSpec((tm, d), lambda i: (i, 0)),
        out_shape=jax.ShapeDtypeStruct((m, d), F32),
        compiler_params=_params("parallel"),
        name="final_norm",
    )(x, g.reshape(1, d))


def kernel(x_prompt, x_sample, cache_k, cache_v, c, c_ctx, w_mod, b_mod, norm_g, ffn_w1, ffn_w3, ffn_w2,
           w_in, w_out, rpb, w_short, b_short, filt_w1, filt_b1, filt_w2, filt_b2, filt_w3, filt_freq,
           filt_bias, w_fnet, final_g):
    b_p, l_p, d = x_prompt.shape
    b_s, l_s, _ = x_sample.shape
    depth = w_mod.shape[0]
    attn_w = cache_k.shape[-2] * cache_k.shape[-1]
    hyena_w = filt_bias.shape[-1]
    n_heads = attn_w // HEAD_DIM
    past = cache_k.shape[2]
    assert 1 + b_s <= COND_ROWS

    streams = (_Stream(b_p, l_p, 0, True), _Stream(b_s, l_s, 1, False))
    xs = [x_prompt.reshape(b_p * l_p, d), x_sample.reshape(b_s * l_s, d)]
    conds = jnp.concatenate([c_ctx[None, :], c, jnp.zeros((COND_ROWS - 1 - b_s, d), F32)], axis=0)
    w1b, w3b, w2b = ffn_w1.astype(BF16), ffn_w3.astype(BF16), ffn_w2.astype(BF16)
    w_in_b, w_out_b, w_fnet_b = w_in.astype(BF16), w_out.astype(BF16), w_fnet.astype(BF16)
    rfft_mats = {st.l: _rfft_matrices(st.l) for st in streams}

    new_k, new_v = [], []
    for l in range(depth):
        mods = _adaln(conds, w_mod, b_mod, l)
        xs = [_ffn(st, x, mods, norm_g[l, 0], w1b, w3b, w2b, l, 0, 0) for st, x in zip(streams, xs)]
        if l % 2 == 0:
            e = l // 2
            projs = [_norm_mm(st, x, mods, norm_g[l, 1], w_in_b, e, 3) for st, x in zip(streams, xs)]
            attn = [_ctx_attn(streams[0], projs[0], attn_w),
                    _na_attn(streams[1], projs[1], attn_w, cache_k[:, e].reshape(b_s * past, attn_w),
                             cache_v[:, e].reshape(b_s * past, attn_w), rpb[e])]
            for i, st in enumerate(streams):
                mats = rfft_mats[st.l]
                filters = _hyena_filters(st.l, hyena_w, filt_w1[e], filt_b1[e], filt_w2[e], filt_b2[e],
                                         filt_w3[e], filt_freq[e], mats[0], mats[1])
                hy = _hyena(st, projs[i], 3 * attn_w, hyena_w, e, w_short, b_short, mats, filters, filt_bias)
                xs[i] = _mm_res(st, attn[i], hy, w_out_b, e, xs[i], mods, 5)
            new_k.append(projs[0][:, attn_w:2 * attn_w].reshape(b_p, l_p, n_heads, HEAD_DIM))
            new_v.append(projs[0][:, 2 * attn_w:3 * attn_w].reshape(b_p, l_p, n_heads, HEAD_DIM))
        else:
            o = l // 2
            for i, st in enumerate(streams):
                h = _mod_norm(st, xs[i], mods, norm_g[l, 1], 3)
                a, b = _seq_dft(st, h)
                xs[i] = _fnet_out(st, a, b, w_fnet_b, o, xs[i], mods, 5)
        xs = [_ffn(st, x, mods, norm_g[l, 2], w1b, w3b, w2b, l, 1, 6) for st, x in zip(streams, xs)]

    y_prompt = _rms_norm(xs[0], final_g).reshape(b_p, l_p, d)
    y_sample = _rms_norm(xs[1], final_g).reshape(b_s, l_s, d)
    return y_prompt, y_sample, jnp.stack(new_k, axis=1), jnp.stack(new_v, axis=1)
```

```python
import dataclasses
import functools
import math

import jax
import jax.numpy as jnp
from jax import lax
from jax.experimental import pallas as pl
from jax.experimental.pallas import tpu as pltpu

F32 = jnp.float32
BF16 = jnp.bfloat16
HIGHEST = lax.Precision.HIGHEST

N_MOD = 9
GRID_W = 64
WIN_H = 8
WIN_W = 16
HEAD_DIM = 128
SHORT_CONV = 3
FILTER_EMB = 33
HYENA_ORDER = 2
DECAY_TARGET = 1e-2
FAST_DECAY_PCT = 0.3
SLOW_DECAY_PCT = 1.5
FNET_GROUPS = 8
EPS = 1e-6
NEG_INF = -1e30

V7X_VMEM_BYTES = 64 * 2**20
VMEM_LIMIT = V7X_VMEM_BYTES - 8 * 2**20
LANE = 128
BF16_SUBLANES = 16
COND_ROWS = 16


def _tile(n, pref, unit=LANE):
    if n <= pref:
        return n
    t = (pref // unit) * unit
    while t > unit and n % t:
        t -= unit
    assert n % t == 0, (n, pref)
    return t


def _column_tiles(w, tn):
    *lead, k, n = w.shape
    return jnp.moveaxis(w.reshape(*lead, k, n // tn, tn), -2, -3)


def _params(*sem):
    return pltpu.CompilerParams(dimension_semantics=sem, vmem_limit_bytes=VMEM_LIMIT)


@dataclasses.dataclass(frozen=True)
class _Stream:
    n_seq: int
    l: int
    cond0: int
    shared: bool

    @property
    def n(self):
        return self.n_seq * self.l

    def row_tile(self, pref):
        return _tile(self.n if self.shared else self.l, pref, BF16_SUBLANES)

    def cond(self, i, tm):
        return self.cond0 if self.shared else self.cond0 + i // (self.l // tm)


def _fill_mod_norm(x_ref, g_ref, mod_ref, h_ref, rs_ref, mi):
    rows = BF16_SUBLANES
    tm, d = x_ref.shape
    cw = _tile(d, 512)

    def stats(c, carry):
        r = pl.multiple_of(c * rows, rows)
        x = x_ref[pl.ds(r, rows), :]
        rs = lax.rsqrt(jnp.mean(x * x, axis=-1, keepdims=True) + EPS)
        rs_ref[pl.ds(r, rows), :] = jnp.broadcast_to(rs, (rows, LANE))
        return carry

    lax.fori_loop(0, tm // rows, stats, 0, unroll=4)

    m = mod_ref[0]
    for t in range(d // cw):
        cs = slice(t * cw, (t + 1) * cw)
        gain = g_ref[:, cs] * (1.0 + m[mi + 1:mi + 2, cs])
        shift = m[mi:mi + 1, cs]

        def apply(c, carry):
            r = pl.multiple_of(c * rows, rows)
            rs = rs_ref[pl.ds(r, rows), :]
            rs = jnp.concatenate([rs] * (cw // LANE), axis=1) if cw > LANE else rs[:, :cw]
            h_ref[pl.ds(r, rows), cs] = (x_ref[pl.ds(r, rows), cs] * rs * gain + shift).astype(h_ref.dtype)
            return carry

        lax.fori_loop(0, tm // rows, apply, 0)


def _dot(a, b):
    return jnp.dot(a, b, preferred_element_type=F32)


def _dot_hp(a, b):
    return jnp.dot(a, b, preferred_element_type=F32, precision=HIGHEST)


def _dot_nt(a, b):
    return lax.dot_general(a, b, (((1,), (1,)), ((), ())), preferred_element_type=F32)


def _adaln_kernel(c_ref, w_ref, b_ref, o_ref):
    c = c_ref[...]
    s = c * (1.0 / (1.0 + jnp.exp(-c)))
    s_hi = s.astype(BF16)
    s_lo = (s - s_hi.astype(F32)).astype(BF16)
    w = w_ref[...].astype(BF16)
    o_ref[...] = _dot(s_hi, w) + _dot(s_lo, w) + b_ref[...]


def _adaln(conds, w_mod, b_mod, l):
    d = conds.shape[1]
    n = w_mod.shape[-1]
    tn = _tile(n, 1024)
    out = pl.pallas_call(
        _adaln_kernel,
        grid=(n // tn,),
        in_specs=[
            pl.BlockSpec((COND_ROWS, d), lambda j: (0, 0)),
            pl.BlockSpec((None, d, tn), lambda j: (l, 0, j)),
            pl.BlockSpec((None, 1, tn), lambda j: (l, 0, j)),
        ],
        out_specs=pl.BlockSpec((COND_ROWS, tn), lambda j: (0, j)),
        out_shape=jax.ShapeDtypeStruct((COND_ROWS, n), F32),
        compiler_params=_params("arbitrary"),
        name="adaln",
    )(conds, w_mod, b_mod.reshape(b_mod.shape[0], 1, n))
    return out.reshape(COND_ROWS, N_MOD, d)


def _ffn_kernel(x_ref, mod_ref, g_ref, w13_ref, w2_ref, fin_ref, o_ref, h_ref, rs_ref, *, mi, final_norm):
    j = pl.program_id(1)
    tf = w2_ref.shape[0]

    @pl.when(j == 0)
    def _():
        _fill_mod_norm(x_ref, g_ref, mod_ref, h_ref, rs_ref, mi)
        o_ref[...] = jnp.zeros_like(o_ref)

    ab = _dot(h_ref[...], w13_ref[...])
    a, b = ab[:, :tf], ab[:, tf:]
    act = (a * (1.0 / (1.0 + jnp.exp(-a))) * b).astype(BF16)
    o_ref[...] += _dot(act, w2_ref[...])

    @pl.when(j == pl.num_programs(1) - 1)
    def _():
        rows = 8
        half_gate = 0.5 * mod_ref[0][mi + 2:mi + 3]

        def body(c, carry):
            r = pl.multiple_of(c * rows, rows)
            y = x_ref[pl.ds(r, rows), :] + half_gate * o_ref[pl.ds(r, rows), :]
            if final_norm:
                y = y * lax.rsqrt(jnp.mean(y * y, axis=-1, keepdims=True) + EPS) * fin_ref[...]
            o_ref[pl.ds(r, rows), :] = y
            return carry

        lax.fori_loop(0, o_ref.shape[0] // rows, body, 0)


def _ffn(st, x, mods, g, w13, w2, l, s, mi, final_g, final_norm):
    m, d = x.shape
    f = w2.shape[-2]
    tm = st.row_tile(512)
    tf = w13.shape[-1] // 2
    return pl.pallas_call(
        functools.partial(_ffn_kernel, mi=mi, final_norm=final_norm),
        grid=(m // tm, f // tf),
        in_specs=[
            pl.BlockSpec((tm, d), lambda i, j: (i, 0)),
            pl.BlockSpec((1, N_MOD, d), lambda i, j: (st.cond(i, tm), 0, 0)),
            pl.BlockSpec((1, d), lambda i, j: (0, 0)),
            pl.BlockSpec((None, None, None, d, 2 * tf), lambda i, j: (l, s, j, 0, 0)),
            pl.BlockSpec((None, None, tf, d), lambda i, j: (l, s, j, 0)),
            pl.BlockSpec((1, d), lambda i, j: (0, 0)),
        ],
        out_specs=pl.BlockSpec((tm, d), lambda i, j: (i, 0)),
        out_shape=jax.ShapeDtypeStruct((m, d), F32),
        scratch_shapes=[pltpu.VMEM((tm, d), BF16), pltpu.VMEM((tm, LANE), F32)],
        compiler_params=_params("parallel", "arbitrary"),
        name="ffn",
    )(x, mods, g.reshape(1, d), w13, w2, final_g.reshape(1, d))


def _norm_mm_kernel(x_ref, mod_ref, g_ref, w_ref, o_ref, h_ref, rs_ref, *, mi):
    @pl.when(pl.program_id(1) == 0)
    def _():
        _fill_mod_norm(x_ref, g_ref, mod_ref, h_ref, rs_ref, mi)

    o_ref[...] = _dot(h_ref[...], w_ref[...])


def _norm_mm(st, x, mods, g, w, e, mi):
    m, d = x.shape
    tn = w.shape[-1]
    n = w.shape[1] * tn
    tm = st.row_tile(512)
    return pl.pallas_call(
        functools.partial(_norm_mm_kernel, mi=mi),
        grid=(m // tm, n // tn),
        in_specs=[
            pl.BlockSpec((tm, d), lambda i, j: (i, 0)),
            pl.BlockSpec((1, N_MOD, d), lambda i, j: (st.cond(i, tm), 0, 0)),
            pl.BlockSpec((1, d), lambda i, j: (0, 0)),
            pl.BlockSpec((None, None, d, tn), lambda i, j: (e, j, 0, 0)),
        ],
        out_specs=pl.BlockSpec((tm, tn), lambda i, j: (i, j)),
        out_shape=jax.ShapeDtypeStruct((m, n), F32),
        scratch_shapes=[pltpu.VMEM((tm, d), BF16), pltpu.VMEM((tm, LANE), F32)],
        compiler_params=_params("parallel", "arbitrary"),
        name="in_proj",
    )(x, mods, g.reshape(1, d), w)


def _mod_norm_kernel(x_ref, mod_ref, g_ref, h_ref, rs_ref, *, mi):
    _fill_mod_norm(x_ref, g_ref, mod_ref, h_ref, rs_ref, mi)


def _mod_norm(st, x, mods, g, mi):
    m, d = x.shape
    tm = st.row_tile(256)
    return pl.pallas_call(
        functools.partial(_mod_norm_kernel, mi=mi),
        grid=(m // tm,),
        in_specs=[
            pl.BlockSpec((tm, d), lambda i: (i, 0)),
            pl.BlockSpec((1, N_MOD, d), lambda i: (st.cond(i, tm), 0, 0)),
            pl.BlockSpec((1, d), lambda i: (0, 0)),
        ],
        out_specs=pl.BlockSpec((tm, d), lambda i: (i, 0)),
        out_shape=jax.ShapeDtypeStruct((m, d), BF16),
        scratch_shapes=[pltpu.VMEM((tm, LANE), F32)],
        compiler_params=_params("parallel"),
        name="mod_norm",
    )(x, mods, g.reshape(1, d))


def _mm_res_kernel(a0_ref, a1_ref, w0_ref, w1_ref, res_ref, mod_ref, o_ref, *, gi):
    mix = _dot(a0_ref[...], w0_ref[...]) + _dot(a1_ref[...], w1_ref[...])
    o_ref[...] = res_ref[...] + mod_ref[0][gi:gi + 1] * mix


def _mm_res(st, a0, a1, w, e, res, mods, gi):
    m, k = a0.shape
    assert a1.shape == (m, k) and w.shape[1] == 2 and w.shape[3] == k
    tn = w.shape[-1]
    n = w.shape[2] * tn
    tm = st.row_tile(512)
    return pl.pallas_call(
        functools.partial(_mm_res_kernel, gi=gi),
        grid=(m // tm, n // tn),
        in_specs=[
            pl.BlockSpec((tm, k), lambda i, j: (i, 0)),
            pl.BlockSpec((tm, k), lambda i, j: (i, 0)),
            pl.BlockSpec((None, None, None, k, tn), lambda i, j: (e, 0, j, 0, 0)),
            pl.BlockSpec((None, None, None, k, tn), lambda i, j: (e, 1, j, 0, 0)),
            pl.BlockSpec((tm, tn), lambda i, j: (i, j)),
            pl.BlockSpec((1, N_MOD, tn), lambda i, j: (st.cond(i, tm), 0, j)),
        ],
        out_specs=pl.BlockSpec((tm, tn), lambda i, j: (i, j)),
        out_shape=jax.ShapeDtypeStruct((m, n), F32),
        compiler_params=_params("parallel", "arbitrary"),
        name="out_proj",
    )(a0, a1, w, w, res, mods)


def _ctx_attn_kernel(q_ref, k_ref, v_ref, o_ref, *, n_heads, scale):
    for h in range(n_heads):
        sl = slice(h * HEAD_DIM, (h + 1) * HEAD_DIM)
        q = q_ref[:, sl].astype(BF16)
        k = k_ref[:, sl].astype(BF16)
        v = v_ref[:, sl].astype(BF16)
        s = _dot_nt(q, k) * scale
        e = jnp.exp(s - jnp.max(s, axis=-1, keepdims=True))
        p = e * (1.0 / jnp.sum(e, axis=-1, keepdims=True))
        o_ref[:, sl] = _dot(p.astype(BF16), v).astype(o_ref.dtype)


def _ctx_attn(st, proj, width):
    l = st.l
    return pl.pallas_call(
        functools.partial(_ctx_attn_kernel, n_heads=width // HEAD_DIM, scale=HEAD_DIM ** -0.5),
        grid=(st.n_seq,),
        in_specs=[pl.BlockSpec((l, width), lambda b, part=part: (b, part)) for part in range(3)],
        out_specs=pl.BlockSpec((l, width), lambda b: (b, 0)),
        out_shape=jax.ShapeDtypeStruct((st.n, width), BF16),
        compiler_params=_params("parallel"),
        name="ctx_attn",
    )(proj, proj, proj)


def _na_window(r, rows, kh):
    start = min(max(r - kh // 2, 0), rows - kh)
    lo = start - start % 2
    hi = start + kh + (start + kh) % 2
    return start, lo, hi


def _na_attn_kernel(q_ref, k_ref, v_ref, ck_ref, cv_ref, tab_ref, o_ref, pl_ref, pc_ref, *, rows, kh, scale):
    q = q_ref[...].astype(BF16)
    s_lat_all = _dot_nt(q, k_ref[...].astype(BF16))
    s_ctx_all = _dot_nt(q, ck_ref[...].astype(BF16))
    for r in range(rows):
        start, lo, hi = _na_window(r, rows, kh)
        qs = slice(r * GRID_W, (r + 1) * GRID_W)
        pieces = []
        for p in range(lo // 2, hi // 2):
            ok0 = start <= 2 * p < start + kh
            ok1 = start <= 2 * p + 1 < start + kh
            dr = 2 * p - r + WIN_H - 1
            pieces.append(tab_ref[0, 0, dr] if ok0 and ok1 else tab_ref[0, 1, dr] if ok0 else tab_ref[0, 2, dr + 1])
        s_lat = s_lat_all[qs, lo * GRID_W:hi * GRID_W] * scale + jnp.concatenate(pieces, axis=1)
        s_ctx = s_ctx_all[qs, :] * scale
        mx = jnp.maximum(jnp.max(s_lat, axis=-1, keepdims=True), jnp.max(s_ctx, axis=-1, keepdims=True))
        e_lat = jnp.exp(s_lat - mx)
        e_ctx = jnp.exp(s_ctx - mx)
        inv = 1.0 / (jnp.sum(e_lat, axis=-1, keepdims=True) + jnp.sum(e_ctx, axis=-1, keepdims=True))
        p_lat = (e_lat * inv).astype(BF16)
        parts = []
        if lo > 0:
            parts.append(jnp.zeros((GRID_W, lo * GRID_W), BF16))
        parts.append(p_lat)
        if hi < rows:
            parts.append(jnp.zeros((GRID_W, (rows - hi) * GRID_W), BF16))
        pl_ref[qs, :] = jnp.concatenate(parts, axis=1) if len(parts) > 1 else p_lat
        pc_ref[qs, :] = (e_ctx * inv).astype(BF16)
    o = _dot(pl_ref[...], v_ref[...].astype(BF16)) + _dot(pc_ref[...], cv_ref[...].astype(BF16))
    o_ref[...] = o.astype(o_ref.dtype)


def _na_bias_table(rpb):
    col = jnp.arange(GRID_W)
    col_start = jnp.clip(col - WIN_W // 2, 0, GRID_W - WIN_W)
    col_ok = (col[None, :] >= col_start[:, None]) & (col[None, :] < col_start[:, None] + WIN_W)
    dc_idx = jnp.clip(col[None, :] - col[:, None] + WIN_W - 1, 0, 2 * WIN_W - 2)
    t = jnp.where(col_ok, rpb.astype(F32)[:, :, dc_idx], NEG_INF)
    neg = jnp.full_like(t, NEG_INF)
    nxt = jnp.concatenate([t[:, 1:], neg[:, :1]], axis=1)
    return jnp.stack([jnp.concatenate([t, nxt], axis=-1), jnp.concatenate([t, neg], axis=-1),
                      jnp.concatenate([neg, t], axis=-1)], axis=1)


def _na_attn(st, proj, width, cache_k, cache_v, rpb):
    l = st.l
    n_heads = width // HEAD_DIM
    rows = l // GRID_W
    kh = min(WIN_H, rows)
    assert rows % 2 == 0 and kh % 2 == 0
    past = cache_k.shape[0] // st.n_seq
    tab = _na_bias_table(rpb)
    qkv_specs = [pl.BlockSpec((l, HEAD_DIM), lambda b, h, part=part: (b, part * n_heads + h)) for part in range(3)]
    return pl.pallas_call(
        functools.partial(_na_attn_kernel, rows=rows, kh=kh, scale=HEAD_DIM ** -0.5),
        grid=(st.n_seq, n_heads),
        in_specs=qkv_specs + [
            pl.BlockSpec((past, HEAD_DIM), lambda b, h: (b, h)),
            pl.BlockSpec((past, HEAD_DIM), lambda b, h: (b, h)),
            pl.BlockSpec((1,) + tab.shape[1:], lambda b, h: (h, 0, 0, 0, 0)),
        ],
        out_specs=pl.BlockSpec((l, HEAD_DIM), lambda b, h: (b, h)),
        out_shape=jax.ShapeDtypeStruct((st.n, width), BF16),
        scratch_shapes=[pltpu.VMEM((l, l), BF16), pltpu.VMEM((l, past), BF16)],
        compiler_params=_params("parallel", "arbitrary"),
        name="na_attn",
    )(proj, proj, proj, cache_k, cache_v, tab)


def _cos_sin(n_rows, n_cols, period):
    r = lax.broadcasted_iota(jnp.int32, (n_rows, n_cols), 0)
    c = lax.broadcasted_iota(jnp.int32, (n_rows, n_cols), 1)
    ang = ((r * c) % period).astype(F32) * (2.0 * math.pi / period)
    return jnp.cos(ang), jnp.sin(ang)


def _rfft_matrices(l):
    n = 2 * l
    cos, sin = _cos_sin(l, l, n)
    alt = jnp.where(lax.broadcasted_iota(jnp.int32, (l, l), 1) % 2 == 0, 1.0, -1.0).astype(F32)
    k_is0 = lax.broadcasted_iota(jnp.int32, (l, l), 0) == 0
    fwd_re = cos
    fwd_im = jnp.where(k_is0, alt, -sin)
    bin_is0 = lax.broadcasted_iota(jnp.int32, (l, l), 1) == 0
    inv_re = jnp.where(bin_is0, 1.0 / n, 2.0 / n) * cos
    inv_im = jnp.where(bin_is0, alt.T / n, (-2.0 / n) * sin)
    return fwd_re, fwd_im, inv_re, inv_im


def _filter_kernel(z_ref, t_ref, w1_ref, b1_ref, w2_ref, b2_ref, fq_ref, w3f_ref, w3b_ref, dl_ref,
                   fre_ref, fim_ref, kr_ref, ki_ref, kn_ref):
    fq = fq_ref[...]
    hid = jnp.sin(fq[0:1] * (_dot_hp(z_ref[...], w1_ref[...]) + b1_ref[...]))
    hid = jnp.sin(fq[1:2] * (_dot_hp(hid, w2_ref[...]) + b2_ref[...]))
    decay = jnp.exp(-t_ref[...] * dl_ref[...])
    row = lax.broadcasted_iota(jnp.int32, kr_ref.shape, 0)
    h_fwd = _dot_hp(hid, w3f_ref[...]) * decay
    h_bwd = jnp.where(row == 0, 0.0, _dot_hp(hid, w3b_ref[...]) * decay)
    nrm = lax.rsqrt(jnp.sum(h_fwd * h_fwd + h_bwd * h_bwd, axis=0, keepdims=True) + EPS)
    even = (h_fwd + h_bwd) * nrm
    odd = (h_fwd - h_bwd) * nrm
    k_re = _dot_hp(fre_ref[...], even)
    k_im = _dot_hp(fim_ref[...], odd)
    nyq = jnp.sum(jnp.where(row % 2 == 0, even, -even), axis=0, keepdims=True)
    kr_ref[...] = k_re
    ki_ref[...] = jnp.where(row == 0, 0.0, k_im)
    kn_ref[...] = jnp.where(row == 0, nyq, k_re)


def _pad_to(a, shape):
    return jnp.pad(a.astype(F32), [(0, s - n) for n, s in zip(a.shape, shape)])


def _hyena_filters(l, width, w1, b1, w2, b2, w3, freq, fwd_re, fwd_im):
    hid = w2.shape[0]
    hp = LANE * pl.cdiv(hid, LANE)
    n_bands = (FILTER_EMB - 1) // 2
    t = jnp.linspace(0.0, 1.0, l, dtype=F32)[:, None]
    w = (2.0 * math.pi / l) * jnp.arange(l, dtype=F32)[:, None]
    bands = jnp.linspace(1e-4, n_bands - 1, n_bands, dtype=F32)[None, :]
    z = _pad_to(jnp.concatenate([t, jnp.cos(bands * w), -jnp.sin(bands * w)], axis=-1), (l, LANE))
    deltas = jnp.abs(jnp.linspace(math.log(DECAY_TARGET) / FAST_DECAY_PCT,
                                  math.log(DECAY_TARGET) / SLOW_DECAY_PCT, width, dtype=F32))[None, :]
    tc = _tile(width, 512)
    nc = width // tc
    small = lambda shape: pl.BlockSpec(shape, lambda o, c: (0,) * len(shape))
    out = jax.ShapeDtypeStruct((HYENA_ORDER, l, width), F32)
    return pl.pallas_call(
        _filter_kernel,
        grid=(HYENA_ORDER, nc),
        in_specs=[small((l, LANE)), small((l, 1)), small((LANE, hp)), small((1, hp)), small((hp, hp)),
                  small((1, hp)), small((2, hp)),
                  pl.BlockSpec((hp, tc), lambda o, c: (0, 2 * o * nc + c)),
                  pl.BlockSpec((hp, tc), lambda o, c: (0, (2 * o + 1) * nc + c)),
                  pl.BlockSpec((1, tc), lambda o, c: (0, c)),
                  small((l, l)), small((l, l))],
        out_specs=[pl.BlockSpec((None, l, tc), lambda o, c: (o, 0, c))] * 3,
        out_shape=[out, out, out],
        compiler_params=_params("arbitrary", "arbitrary"),
        name="hyena_filters",
    )(z, t, _pad_to(w1, (LANE, hp)), _pad_to(b1[None, :], (1, hp)), _pad_to(w2, (hp, hp)),
      _pad_to(b2[None, :], (1, hp)), _pad_to(freq, (2, hp)), _pad_to(w3, (hp, w3.shape[1])),
      _pad_to(w3, (hp, w3.shape[1])), deltas, fwd_re, fwd_im)


def _hyena_kernel(pv_ref, p1_ref, p2_ref, wv_ref, w1_ref, w2_ref, bv_ref, b1_ref, b2_ref,
                  fre_ref, fim_ref, gre_ref, gim_ref, kr_ref, ki_ref, kn_ref, fbias_ref, o_ref):
    l, tc = o_ref.shape
    row = lax.broadcasted_iota(jnp.int32, (l, tc), 0)

    def short_conv(p_ref, w_ref, b_ref):
        p = p_ref[...]
        prev = jnp.where(row == 0, 0.0, pltpu.roll(p, 1, 0))
        nxt = jnp.where(row == l - 1, 0.0, pltpu.roll(p, l - 1, 0))
        w = w_ref[...]
        out = b_ref[...] + prev * w[0:1]
        out = out + p * w[1:2]
        return out + nxt * w[2:3]

    def long_conv(u, o, bias):
        ub = u.astype(BF16)
        u_re = _dot(fre_ref[...], ub)
        u_im = _dot(fim_ref[...], ub)
        kr, ki, kn = kr_ref[o], ki_ref[o], kn_ref[o]
        w_re = (u_re * kr - u_im * ki).astype(BF16)
        w_im = (u_re * ki + u_im * kn).astype(BF16)
        y = _dot(gre_ref[...], w_re) + _dot(gim_ref[...], w_im)
        return y + u * bias

    fbias = fbias_ref[...]
    v = short_conv(pv_ref, wv_ref, bv_ref)
    z = short_conv(p1_ref, w1_ref, b1_ref) * long_conv(v, 0, fbias[0:1])
    z = short_conv(p2_ref, w2_ref, b2_ref) * long_conv(z, 1, fbias[1:2])
    o_ref[...] = z.astype(o_ref.dtype)


def _hyena(st, proj, col0, width, e, w_short, b_short, mats, filters, fbias):
    l = st.l
    tc = _tile(width, 1024 if l <= 256 else 256)
    nc = width // tc
    c0 = col0 // tc
    p_specs = [pl.BlockSpec((l, tc), lambda c, b, part=part: (b, c0 + part * nc + c)) for part in range(3)]
    w_specs = [pl.BlockSpec((None, SHORT_CONV, tc), lambda c, b, part=part: (e, 0, part * nc + c))
               for part in range(3)]
    b_specs = [pl.BlockSpec((None, 1, tc), lambda c, b, part=part: (e, 0, part * nc + c)) for part in range(3)]
    mat_specs = [pl.BlockSpec((l, l), lambda c, b: (0, 0))] * 4
    k_specs = [pl.BlockSpec((HYENA_ORDER, l, tc), lambda c, b: (0, 0, c))] * 3
    return pl.pallas_call(
        _hyena_kernel,
        grid=(nc, st.n_seq),
        in_specs=p_specs + w_specs + b_specs + mat_specs + k_specs
        + [pl.BlockSpec((None, HYENA_ORDER, tc), lambda c, b: (e, 0, c))],
        out_specs=pl.BlockSpec((l, tc), lambda c, b: (b, c)),
        out_shape=jax.ShapeDtypeStruct((st.n, width), BF16),
        compiler_params=_params("parallel", "arbitrary"),
        name="hyena",
    )(proj, proj, proj, w_short, w_short, w_short, *([b_short.reshape(b_short.shape[0], 1, -1)] * 3),
      *[a.astype(BF16) for a in mats], *filters, fbias)


def _seq_dft_kernel(cos_ref, sin_ref, h_ref, a_ref, b_ref):
    h = h_ref[...]
    a_ref[...] = _dot(cos_ref[...], h).astype(a_ref.dtype)
    b_ref[...] = _dot(sin_ref[...], h).astype(b_ref.dtype)


def _seq_dft(st, h):
    l = st.l
    d = h.shape[1]
    tn = _tile(d, 2048 if l <= 256 else 512)
    cos, sin = _cos_sin(l, l, l)
    cos = (cos * l ** -0.5).astype(BF16)
    sin = (sin * l ** -0.5).astype(BF16)
    out = jax.ShapeDtypeStruct((st.n, d), BF16)
    return pl.pallas_call(
        _seq_dft_kernel,
        grid=(st.n_seq, d // tn),
        in_specs=[pl.BlockSpec((l, l), lambda b, j: (0, 0))] * 2 + [pl.BlockSpec((l, tn), lambda b, j: (b, j))],
        out_specs=[pl.BlockSpec((l, tn), lambda b, j: (b, j))] * 2,
        out_shape=[out, out],
        compiler_params=_params("parallel", "arbitrary"),
        name="seq_dft",
    )(cos, sin, h)


def _fnet_out_kernel(a_ref, b_ref, cos_ref, sin_ref, w_ref, res_ref, mod_ref, o_ref, mix_ref, *, gi, gw):
    @pl.when(pl.program_id(1) == 0)
    def _():
        cos = cos_ref[...]
        sin = sin_ref[...]
        for g in range(a_ref.shape[1] // gw):
            sl = slice(g * gw, (g + 1) * gw)
            mix_ref[:, sl] = (_dot(a_ref[:, sl], cos) - _dot(b_ref[:, sl], sin)).astype(mix_ref.dtype)

    o_ref[...] = res_ref[...] + mod_ref[0][gi:gi + 1] * _dot(mix_ref[...], w_ref[...])


def _fnet_out(st, a, b, w, o, res, mods, gi):
    m, d = a.shape
    gw = d // FNET_GROUPS
    tm = st.row_tile(512)
    tn = w.shape[-1]
    cos, sin = _cos_sin(gw, gw, gw)
    cos = (cos * gw ** -0.5).astype(BF16)
    sin = (sin * gw ** -0.5).astype(BF16)
    return pl.pallas_call(
        functools.partial(_fnet_out_kernel, gi=gi, gw=gw),
        grid=(m // tm, d // tn),
        in_specs=[
            pl.BlockSpec((tm, d), lambda i, j: (i, 0)),
            pl.BlockSpec((tm, d), lambda i, j: (i, 0)),
            pl.BlockSpec((gw, gw), lambda i, j: (0, 0)),
            pl.BlockSpec((gw, gw), lambda i, j: (0, 0)),
            pl.BlockSpec((None, None, d, tn), lambda i, j: (o, j, 0, 0)),
            pl.BlockSpec((tm, tn), lambda i, j: (i, j)),
            pl.BlockSpec((1, N_MOD, tn), lambda i, j: (st.cond(i, tm), 0, j)),
        ],
        out_specs=pl.BlockSpec((tm, tn), lambda i, j: (i, j)),
        out_shape=jax.ShapeDtypeStruct((m, d), F32),
        scratch_shapes=[pltpu.VMEM((tm, d), BF16)],
        compiler_params=_params("parallel", "arbitrary"),
        name="fnet_out",
    )(a, b, cos, sin, w, res, mods)


def kernel(x_prompt, x_sample, cache_k, cache_v, c, c_ctx, w_mod, b_mod, norm_g, ffn_w1, ffn_w3, ffn_w2,
           w_in, w_out, rpb, w_short, b_short, filt_w1, filt_b1, filt_w2, filt_b2, filt_w3, filt_freq,
           filt_bias, w_fnet, final_g):
    b_p, l_p, d = x_prompt.shape
    b_s, l_s, _ = x_sample.shape
    depth = w_mod.shape[0]
    attn_w = cache_k.shape[-2] * cache_k.shape[-1]
    hyena_w = filt_bias.shape[-1]
    n_heads = attn_w // HEAD_DIM
    past = cache_k.shape[2]
    assert 1 + b_s <= COND_ROWS

    streams = (_Stream(b_p, l_p, 0, True), _Stream(b_s, l_s, 1, False))
    xs = [x_prompt.reshape(b_p * l_p, d), x_sample.reshape(b_s * l_s, d)]
    conds = jnp.concatenate([c_ctx[None, :], c, jnp.zeros((COND_ROWS - 1 - b_s, d), F32)], axis=0)
    tf = _tile(ffn_w1.shape[-1], 256)
    w13b = jnp.concatenate([_column_tiles(ffn_w1.astype(BF16), tf), _column_tiles(ffn_w3.astype(BF16), tf)], axis=-1)
    w2b = ffn_w2.astype(BF16)
    w_in_b = _column_tiles(w_in.astype(BF16), _tile(w_in.shape[-1], 1024))
    w_out_b = _column_tiles(w_out.astype(BF16).reshape(w_out.shape[0], 2, w_out.shape[1] // 2, d), _tile(d, 1024))
    w_fnet_b = _column_tiles(w_fnet.astype(BF16), _tile(d, 512))
    rfft_mats = {st.l: _rfft_matrices(st.l) for st in streams}

    new_k, new_v = [], []
    for l in range(depth):
        mods = _adaln(conds, w_mod, b_mod, l)
        xs = [_ffn(st, x, mods, norm_g[l, 0], w13b, w2b, l, 0, 0, final_g, False) for st, x in zip(streams, xs)]
        if l % 2 == 0:
            e = l // 2
            projs = [_norm_mm(st, x, mods, norm_g[l, 1], w_in_b, e, 3) for st, x in zip(streams, xs)]
            attn = [_ctx_attn(streams[0], projs[0], attn_w),
                    _na_attn(streams[1], projs[1], attn_w, cache_k[:, e].reshape(b_s * past, attn_w),
                             cache_v[:, e].reshape(b_s * past, attn_w), rpb[e])]
            for i, st in enumerate(streams):
                mats = rfft_mats[st.l]
                filters = _hyena_filters(st.l, hyena_w, filt_w1[e], filt_b1[e], filt_w2[e], filt_b2[e],
                                         filt_w3[e], filt_freq[e], mats[0], mats[1])
                hy = _hyena(st, projs[i], 3 * attn_w, hyena_w, e, w_short, b_short, mats, filters, filt_bias)
                xs[i] = _mm_res(st, attn[i], hy, w_out_b, e, xs[i], mods, 5)
            new_k.append(projs[0][:, attn_w:2 * attn_w].reshape(b_p, l_p, n_heads, HEAD_DIM))
            new_v.append(projs[0][:, 2 * attn_w:3 * attn_w].reshape(b_p, l_p, n_heads, HEAD_DIM))
        else:
            o = l // 2
            for i, st in enumerate(streams):
                h = _mod_norm(st, xs[i], mods, norm_g[l, 1], 3)
                a, b = _seq_dft(st, h)
                xs[i] = _fnet_out(st, a, b, w_fnet_b, o, xs[i], mods, 5)
        xs = [_ffn(st, x, mods, norm_g[l, 2], w13b, w2b, l, 1, 6, final_g, l == depth - 1)
              for st, x in zip(streams, xs)]

    y_prompt = xs[0].reshape(b_p, l_p, d)
    y_sample = xs[1].reshape(b_s, l_s, d)
    return y_prompt, y_sample, jnp.stack(new_k, axis=1), jnp.stack(new_v, axis=1)
```

```python
import dataclasses
import functools
import math

import jax
import jax.numpy as jnp
from jax import lax
from jax.experimental import pallas as pl
from jax.experimental.pallas import tpu as pltpu

F32 = jnp.float32
BF16 = jnp.bfloat16
HIGHEST = lax.Precision.HIGHEST

N_MOD = 9
GRID_W = 64
WIN_H = 8
WIN_W = 16
HEAD_DIM = 128
SHORT_CONV = 3
FILTER_EMB = 33
HYENA_ORDER = 2
DECAY_TARGET = 1e-2
FAST_DECAY_PCT = 0.3
SLOW_DECAY_PCT = 1.5
FNET_GROUPS = 8
EPS = 1e-6
NEG_INF = -1e30

V7X_VMEM_BYTES = 64 * 2**20
VMEM_LIMIT = V7X_VMEM_BYTES - 8 * 2**20
LANE = 128
BF16_SUBLANES = 16
COND_ROWS = 16


def _tile(n, pref, unit=LANE):
    if n <= pref:
        return n
    t = (pref // unit) * unit
    while t > unit and n % t:
        t -= unit
    assert n % t == 0, (n, pref)
    return t


def _params(*sem):
    return pltpu.CompilerParams(dimension_semantics=sem, vmem_limit_bytes=VMEM_LIMIT)


@dataclasses.dataclass(frozen=True)
class _Stream:
    n_seq: int
    l: int
    cond0: int
    shared: bool

    @property
    def n(self):
        return self.n_seq * self.l

    def row_tile(self, pref):
        return _tile(self.n if self.shared else self.l, pref, BF16_SUBLANES)

    def cond(self, i, tm):
        return self.cond0 if self.shared else self.cond0 + i // (self.l // tm)


def _fill_mod_norm(x_ref, g_ref, mod_ref, h_ref, rs_ref, mi):
    rows = BF16_SUBLANES
    tm, d = x_ref.shape
    cw = _tile(d, 512)

    def stats(c, carry):
        r = pl.multiple_of(c * rows, rows)
        x = x_ref[pl.ds(r, rows), :]
        rs = lax.rsqrt(jnp.mean(x * x, axis=-1, keepdims=True) + EPS)
        rs_ref[pl.ds(r, rows), :] = jnp.broadcast_to(rs, (rows, LANE))
        return carry

    lax.fori_loop(0, tm // rows, stats, 0, unroll=4)

    m = mod_ref[0]
    for t in range(d // cw):
        cs = slice(t * cw, (t + 1) * cw)
        gain = g_ref[:, cs] * (1.0 + m[mi + 1:mi + 2, cs])
        shift = m[mi:mi + 1, cs]

        def apply(c, carry):
            r = pl.multiple_of(c * rows, rows)
            rs = rs_ref[pl.ds(r, rows), :]
            rs = jnp.concatenate([rs] * (cw // LANE), axis=1) if cw > LANE else rs[:, :cw]
            h_ref[pl.ds(r, rows), cs] = (x_ref[pl.ds(r, rows), cs] * rs * gain + shift).astype(h_ref.dtype)
            return carry

        lax.fori_loop(0, tm // rows, apply, 0)


def _dot(a, b):
    return jnp.dot(a, b, preferred_element_type=F32)


def _dot_hp(a, b):
    return jnp.dot(a, b, preferred_element_type=F32, precision=HIGHEST)


def _dot_nt(a, b):
    return lax.dot_general(a, b, (((1,), (1,)), ((), ())), preferred_element_type=F32)


def _adaln_kernel(c_ref, w_ref, b_ref, o_ref):
    c = c_ref[...]
    s = c * (1.0 / (1.0 + jnp.exp(-c)))
    s_hi = s.astype(BF16)
    s_lo = (s - s_hi.astype(F32)).astype(BF16)
    w = w_ref[...].astype(BF16)
    o_ref[...] = _dot(s_hi, w) + _dot(s_lo, w) + b_ref[...]


def _adaln(conds, w_mod, b_mod, l):
    d = conds.shape[1]
    n = w_mod.shape[-1]
    tn = _tile(n, 1024)
    out = pl.pallas_call(
        _adaln_kernel,
        grid=(n // tn,),
        in_specs=[
            pl.BlockSpec((COND_ROWS, d), lambda j: (0, 0)),
            pl.BlockSpec((None, d, tn), lambda j: (l, 0, j)),
            pl.BlockSpec((None, 1, tn), lambda j: (l, 0, j)),
        ],
        out_specs=pl.BlockSpec((COND_ROWS, tn), lambda j: (0, j)),
        out_shape=jax.ShapeDtypeStruct((COND_ROWS, n), F32),
        compiler_params=_params("arbitrary"),
        name="adaln",
    )(conds, w_mod, b_mod.reshape(b_mod.shape[0], 1, n))
    return out.reshape(COND_ROWS, N_MOD, d)


def _ffn_kernel(x_ref, mod_ref, g_ref, w1_ref, w3_ref, w2_ref, o_ref, h_ref, rs_ref, *, mi):
    j = pl.program_id(1)

    @pl.when(j == 0)
    def _():
        _fill_mod_norm(x_ref, g_ref, mod_ref, h_ref, rs_ref, mi)
        o_ref[...] = jnp.zeros_like(o_ref)

    h = h_ref[...]
    a = _dot(h, w1_ref[...])
    b = _dot(h, w3_ref[...])
    act = (a * (1.0 / (1.0 + jnp.exp(-a))) * b).astype(BF16)
    o_ref[...] += _dot(act, w2_ref[...])

    @pl.when(j == pl.num_programs(1) - 1)
    def _():
        rows = BF16_SUBLANES
        half_gate = 0.5 * mod_ref[0][mi + 2:mi + 3]

        def body(c, carry):
            r = pl.multiple_of(c * rows, rows)
            o_ref[pl.ds(r, rows), :] = x_ref[pl.ds(r, rows), :] + half_gate * o_ref[pl.ds(r, rows), :]
            return carry

        lax.fori_loop(0, o_ref.shape[0] // rows, body, 0)


def _ffn(st, x, mods, g, w1, w3, w2, l, s, mi):
    m, d = x.shape
    f = w1.shape[-1]
    tm = st.row_tile(512)
    tf = _tile(f, 256)
    return pl.pallas_call(
        functools.partial(_ffn_kernel, mi=mi),
        grid=(m // tm, f // tf),
        in_specs=[
            pl.BlockSpec((tm, d), lambda i, j: (i, 0)),
            pl.BlockSpec((1, N_MOD, d), lambda i, j: (st.cond(i, tm), 0, 0)),
            pl.BlockSpec((1, d), lambda i, j: (0, 0)),
            pl.BlockSpec((None, None, d, tf), lambda i, j: (l, s, 0, j)),
            pl.BlockSpec((None, None, d, tf), lambda i, j: (l, s, 0, j)),
            pl.BlockSpec((None, None, tf, d), lambda i, j: (l, s, j, 0)),
        ],
        out_specs=pl.BlockSpec((tm, d), lambda i, j: (i, 0)),
        out_shape=jax.ShapeDtypeStruct((m, d), F32),
        scratch_shapes=[pltpu.VMEM((tm, d), BF16), pltpu.VMEM((tm, LANE), F32)],
        compiler_params=_params("parallel", "arbitrary"),
        name="ffn",
    )(x, mods, g.reshape(1, d), w1, w3, w2)


def _norm_mm_kernel(x_ref, mod_ref, g_ref, w_ref, o_ref, h_ref, rs_ref, *, mi):
    @pl.when(pl.program_id(1) == 0)
    def _():
        _fill_mod_norm(x_ref, g_ref, mod_ref, h_ref, rs_ref, mi)

    o_ref[...] = _dot(h_ref[...], w_ref[...])


def _norm_mm(st, x, mods, g, w, e, mi):
    m, d = x.shape
    n = w.shape[-1]
    tm = st.row_tile(1024)
    tn = _tile(n, 512)
    return pl.pallas_call(
        functools.partial(_norm_mm_kernel, mi=mi),
        grid=(m // tm, n // tn),
        in_specs=[
            pl.BlockSpec((tm, d), lambda i, j: (i, 0)),
            pl.BlockSpec((1, N_MOD, d), lambda i, j: (st.cond(i, tm), 0, 0)),
            pl.BlockSpec((1, d), lambda i, j: (0, 0)),
            pl.BlockSpec((None, d, tn), lambda i, j: (e, 0, j)),
        ],
        out_specs=pl.BlockSpec((tm, tn), lambda i, j: (i, j)),
        out_shape=jax.ShapeDtypeStruct((m, n), F32),
        scratch_shapes=[pltpu.VMEM((tm, d), BF16), pltpu.VMEM((tm, LANE), F32)],
        compiler_params=_params("parallel", "arbitrary"),
        name="in_proj",
    )(x, mods, g.reshape(1, d), w)


def _mod_norm_kernel(x_ref, mod_ref, g_ref, h_ref, rs_ref, *, mi):
    _fill_mod_norm(x_ref, g_ref, mod_ref, h_ref, rs_ref, mi)


def _mod_norm(st, x, mods, g, mi):
    m, d = x.shape
    tm = st.row_tile(256)
    return pl.pallas_call(
        functools.partial(_mod_norm_kernel, mi=mi),
        grid=(m // tm,),
        in_specs=[
            pl.BlockSpec((tm, d), lambda i: (i, 0)),
            pl.BlockSpec((1, N_MOD, d), lambda i: (st.cond(i, tm), 0, 0)),
            pl.BlockSpec((1, d), lambda i: (0, 0)),
        ],
        out_specs=pl.BlockSpec((tm, d), lambda i: (i, 0)),
        out_shape=jax.ShapeDtypeStruct((m, d), BF16),
        scratch_shapes=[pltpu.VMEM((tm, LANE), F32)],
        compiler_params=_params("parallel"),
        name="mod_norm",
    )(x, mods, g.reshape(1, d))


def _mm_res_kernel(a0_ref, a1_ref, w0_ref, w1_ref, res_ref, mod_ref, o_ref, *, gi):
    mix = _dot(a0_ref[...], w0_ref[...]) + _dot(a1_ref[...], w1_ref[...])
    o_ref[...] = res_ref[...] + mod_ref[0][gi:gi + 1] * mix


def _mm_res(st, a0, a1, w, e, res, mods, gi):
    m, k = a0.shape
    assert a1.shape == (m, k) and w.shape[1] == 2 * k
    n = w.shape[-1]
    tm = st.row_tile(1024)
    tn = _tile(n, 1024)
    return pl.pallas_call(
        functools.partial(_mm_res_kernel, gi=gi),
        grid=(m // tm, n // tn),
        in_specs=[
            pl.BlockSpec((tm, k), lambda i, j: (i, 0)),
            pl.BlockSpec((tm, k), lambda i, j: (i, 0)),
            pl.BlockSpec((None, k, tn), lambda i, j: (e, 0, j)),
            pl.BlockSpec((None, k, tn), lambda i, j: (e, 1, j)),
            pl.BlockSpec((tm, tn), lambda i, j: (i, j)),
            pl.BlockSpec((1, N_MOD, tn), lambda i, j: (st.cond(i, tm), 0, j)),
        ],
        out_specs=pl.BlockSpec((tm, tn), lambda i, j: (i, j)),
        out_shape=jax.ShapeDtypeStruct((m, n), F32),
        compiler_params=_params("parallel", "arbitrary"),
        name="out_proj",
    )(a0, a1, w, w, res, mods)


def _ctx_attn_kernel(q_ref, k_ref, v_ref, o_ref, *, n_heads, scale):
    for h in range(n_heads):
        sl = slice(h * HEAD_DIM, (h + 1) * HEAD_DIM)
        q = q_ref[:, sl].astype(BF16)
        k = k_ref[:, sl].astype(BF16)
        v = v_ref[:, sl].astype(BF16)
        s = _dot_nt(q, k) * scale
        e = jnp.exp(s - jnp.max(s, axis=-1, keepdims=True))
        p = e * (1.0 / jnp.sum(e, axis=-1, keepdims=True))
        o_ref[:, sl] = _dot(p.astype(BF16), v).astype(o_ref.dtype)


def _ctx_attn(st, proj, width):
    l = st.l
    return pl.pallas_call(
        functools.partial(_ctx_attn_kernel, n_heads=width // HEAD_DIM, scale=HEAD_DIM ** -0.5),
        grid=(st.n_seq,),
        in_specs=[pl.BlockSpec((l, width), lambda b, part=part: (b, part)) for part in range(3)],
        out_specs=pl.BlockSpec((l, width), lambda b: (b, 0)),
        out_shape=jax.ShapeDtypeStruct((st.n, width), BF16),
        compiler_params=_params("parallel"),
        name="ctx_attn",
    )(proj, proj, proj)


def _na_window(r, rows, kh):
    start = min(max(r - kh // 2, 0), rows - kh)
    lo = start - start % 2
    hi = start + kh + (start + kh) % 2
    return start, lo, hi


def _na_attn_kernel(q_ref, k_ref, v_ref, ck_ref, cv_ref, tab_ref, o_ref, pl_ref, pc_ref, *, rows, kh, scale):
    q = q_ref[...].astype(BF16)
    s_lat_all = _dot_nt(q, k_ref[...].astype(BF16))
    s_ctx_all = _dot_nt(q, ck_ref[...].astype(BF16))
    for r in range(rows):
        start, lo, hi = _na_window(r, rows, kh)
        qs = slice(r * GRID_W, (r + 1) * GRID_W)
        pieces = []
        for p in range(lo // 2, hi // 2):
            ok0 = start <= 2 * p < start + kh
            ok1 = start <= 2 * p + 1 < start + kh
            dr = 2 * p - r + WIN_H - 1
            pieces.append(tab_ref[0, 0, dr] if ok0 and ok1 else tab_ref[0, 1, dr] if ok0 else tab_ref[0, 2, dr + 1])
        s_lat = s_lat_all[qs, lo * GRID_W:hi * GRID_W] * scale + jnp.concatenate(pieces, axis=1)
        s_ctx = s_ctx_all[qs, :] * scale
        mx = jnp.maximum(jnp.max(s_lat, axis=-1, keepdims=True), jnp.max(s_ctx, axis=-1, keepdims=True))
        e_lat = jnp.exp(s_lat - mx)
        e_ctx = jnp.exp(s_ctx - mx)
        inv = 1.0 / (jnp.sum(e_lat, axis=-1, keepdims=True) + jnp.sum(e_ctx, axis=-1, keepdims=True))
        p_lat = (e_lat * inv).astype(BF16)
        parts = []
        if lo > 0:
            parts.append(jnp.zeros((GRID_W, lo * GRID_W), BF16))
        parts.append(p_lat)
        if hi < rows:
            parts.append(jnp.zeros((GRID_W, (rows - hi) * GRID_W), BF16))
        pl_ref[qs, :] = jnp.concatenate(parts, axis=1) if len(parts) > 1 else p_lat
        pc_ref[qs, :] = (e_ctx * inv).astype(BF16)
    o = _dot(pl_ref[...], v_ref[...].astype(BF16)) + _dot(pc_ref[...], cv_ref[...].astype(BF16))
    o_ref[...] = o.astype(o_ref.dtype)


def _na_bias_table(rpb):
    col = jnp.arange(GRID_W)
    col_start = jnp.clip(col - WIN_W // 2, 0, GRID_W - WIN_W)
    col_ok = (col[None, :] >= col_start[:, None]) & (col[None, :] < col_start[:, None] + WIN_W)
    dc_idx = jnp.clip(col[None, :] - col[:, None] + WIN_W - 1, 0, 2 * WIN_W - 2)
    t = jnp.where(col_ok, rpb.astype(F32)[:, :, dc_idx], NEG_INF)
    neg = jnp.full_like(t, NEG_INF)
    nxt = jnp.concatenate([t[:, 1:], neg[:, :1]], axis=1)
    return jnp.stack([jnp.concatenate([t, nxt], axis=-1), jnp.concatenate([t, neg], axis=-1),
                      jnp.concatenate([neg, t], axis=-1)], axis=1)


def _na_attn(st, proj, width, cache_k, cache_v, rpb):
    l = st.l
    n_heads = width // HEAD_DIM
    rows = l // GRID_W
    kh = min(WIN_H, rows)
    assert rows % 2 == 0 and kh % 2 == 0
    past = cache_k.shape[0] // st.n_seq
    tab = _na_bias_table(rpb)
    qkv_specs = [pl.BlockSpec((l, HEAD_DIM), lambda b, h, part=part: (b, part * n_heads + h)) for part in range(3)]
    return pl.pallas_call(
        functools.partial(_na_attn_kernel, rows=rows, kh=kh, scale=HEAD_DIM ** -0.5),
        grid=(st.n_seq, n_heads),
        in_specs=qkv_specs + [
            pl.BlockSpec((past, HEAD_DIM), lambda b, h: (b, h)),
            pl.BlockSpec((past, HEAD_DIM), lambda b, h: (b, h)),
            pl.BlockSpec((1,) + tab.shape[1:], lambda b, h: (h, 0, 0, 0, 0)),
        ],
        out_specs=pl.BlockSpec((l, HEAD_DIM), lambda b, h: (b, h)),
        out_shape=jax.ShapeDtypeStruct((st.n, width), BF16),
        scratch_shapes=[pltpu.VMEM((l, l), BF16), pltpu.VMEM((l, past), BF16)],
        compiler_params=_params("parallel", "arbitrary"),
        name="na_attn",
    )(proj, proj, proj, cache_k, cache_v, tab)


def _cos_sin(n_rows, n_cols, period):
    r = lax.broadcasted_iota(jnp.int32, (n_rows, n_cols), 0)
    c = lax.broadcasted_iota(jnp.int32, (n_rows, n_cols), 1)
    ang = ((r * c) % period).astype(F32) * (2.0 * math.pi / period)
    return jnp.cos(ang), jnp.sin(ang)


def _rfft_matrices(l):
    n = 2 * l
    cos, sin = _cos_sin(l, l, n)
    alt = jnp.where(lax.broadcasted_iota(jnp.int32, (l, l), 1) % 2 == 0, 1.0, -1.0).astype(F32)
    k_is0 = lax.broadcasted_iota(jnp.int32, (l, l), 0) == 0
    fwd_re = cos
    fwd_im = jnp.where(k_is0, alt, -sin)
    bin_is0 = lax.broadcasted_iota(jnp.int32, (l, l), 1) == 0
    inv_re = jnp.where(bin_is0, 1.0 / n, 2.0 / n) * cos
    inv_im = jnp.where(bin_is0, alt.T / n, (-2.0 / n) * sin)
    return fwd_re, fwd_im, inv_re, inv_im


def _filter_kernel(z_ref, t_ref, w1_ref, b1_ref, w2_ref, b2_ref, fq_ref, w3f_ref, w3b_ref, dl_ref,
                   fre_ref, fim_ref, kr_ref, ki_ref, kn_ref):
    fq = fq_ref[...]
    hid = jnp.sin(fq[0:1] * (_dot_hp(z_ref[...], w1_ref[...]) + b1_ref[...]))
    hid = jnp.sin(fq[1:2] * (_dot_hp(hid, w2_ref[...]) + b2_ref[...]))
    decay = jnp.exp(-t_ref[...] * dl_ref[...])
    row = lax.broadcasted_iota(jnp.int32, kr_ref.shape, 0)
    h_fwd = _dot_hp(hid, w3f_ref[...]) * decay
    h_bwd = jnp.where(row == 0, 0.0, _dot_hp(hid, w3b_ref[...]) * decay)
    nrm = lax.rsqrt(jnp.sum(h_fwd * h_fwd + h_bwd * h_bwd, axis=0, keepdims=True) + EPS)
    even = (h_fwd + h_bwd) * nrm
    odd = (h_fwd - h_bwd) * nrm
    k_re = _dot_hp(fre_ref[...], even)
    k_im = _dot_hp(fim_ref[...], odd)
    nyq = jnp.sum(jnp.where(row % 2 == 0, even, -even), axis=0, keepdims=True)
    kr_ref[...] = k_re
    ki_ref[...] = jnp.where(row == 0, 0.0, k_im)
    kn_ref[...] = jnp.where(row == 0, nyq, k_re)


def _pad_to(a, shape):
    return jnp.pad(a.astype(F32), [(0, s - n) for n, s in zip(a.shape, shape)])


def _hyena_filters(l, width, w1, b1, w2, b2, w3, freq, fwd_re, fwd_im):
    hid = w2.shape[0]
    hp = LANE * pl.cdiv(hid, LANE)
    n_bands = (FILTER_EMB - 1) // 2
    t = jnp.linspace(0.0, 1.0, l, dtype=F32)[:, None]
    w = (2.0 * math.pi / l) * jnp.arange(l, dtype=F32)[:, None]
    bands = jnp.linspace(1e-4, n_bands - 1, n_bands, dtype=F32)[None, :]
    z = _pad_to(jnp.concatenate([t, jnp.cos(bands * w), -jnp.sin(bands * w)], axis=-1), (l, LANE))
    deltas = jnp.abs(jnp.linspace(math.log(DECAY_TARGET) / FAST_DECAY_PCT,
                                  math.log(DECAY_TARGET) / SLOW_DECAY_PCT, width, dtype=F32))[None, :]
    tc = _tile(width, 512)
    nc = width // tc
    small = lambda shape: pl.BlockSpec(shape, lambda o, c: (0,) * len(shape))
    out = jax.ShapeDtypeStruct((HYENA_ORDER, l, width), F32)
    return pl.pallas_call(
        _filter_kernel,
        grid=(HYENA_ORDER, nc),
        in_specs=[small((l, LANE)), small((l, 1)), small((LANE, hp)), small((1, hp)), small((hp, hp)),
                  small((1, hp)), small((2, hp)),
                  pl.BlockSpec((hp, tc), lambda o, c: (0, 2 * o * nc + c)),
                  pl.BlockSpec((hp, tc), lambda o, c: (0, (2 * o + 1) * nc + c)),
                  pl.BlockSpec((1, tc), lambda o, c: (0, c)),
                  small((l, l)), small((l, l))],
        out_specs=[pl.BlockSpec((None, l, tc), lambda o, c: (o, 0, c))] * 3,
        out_shape=[out, out, out],
        compiler_params=_params("arbitrary", "arbitrary"),
        name="hyena_filters",
    )(z, t, _pad_to(w1, (LANE, hp)), _pad_to(b1[None, :], (1, hp)), _pad_to(w2, (hp, hp)),
      _pad_to(b2[None, :], (1, hp)), _pad_to(freq, (2, hp)), _pad_to(w3, (hp, w3.shape[1])),
      _pad_to(w3, (hp, w3.shape[1])), deltas, fwd_re, fwd_im)


def _hyena_kernel(pv_ref, p1_ref, p2_ref, wv_ref, w1_ref, w2_ref, bv_ref, b1_ref, b2_ref,
                  fre_ref, fim_ref, gre_ref, gim_ref, kr_ref, ki_ref, kn_ref, fbias_ref, o_ref, *, l):
    tc = o_ref.shape[1]
    row = lax.broadcasted_iota(jnp.int32, (l, tc), 0)

    def short_conv(p, w_ref, b_ref):
        prev = jnp.where(row == 0, 0.0, pltpu.roll(p, 1, 0))
        nxt = jnp.where(row == l - 1, 0.0, pltpu.roll(p, l - 1, 0))
        w = w_ref[...]
        out = b_ref[...] + prev * w[0:1]
        out = out + p * w[1:2]
        return out + nxt * w[2:3]

    def long_conv(u, o, bias):
        ub = u.astype(BF16)
        u_re = _dot(fre_ref[...], ub)
        u_im = _dot(fim_ref[...], ub)
        kr, ki, kn = kr_ref[o], ki_ref[o], kn_ref[o]
        w_re = (u_re * kr - u_im * ki).astype(BF16)
        w_im = (u_re * ki + u_im * kn).astype(BF16)
        y = _dot(gre_ref[...], w_re) + _dot(gim_ref[...], w_im)
        return y + u * bias

    fbias = fbias_ref[...]
    for s in range(o_ref.shape[0] // l):
        rs = slice(s * l, (s + 1) * l)
        v = short_conv(pv_ref[rs, :], wv_ref, bv_ref)
        z = short_conv(p1_ref[rs, :], w1_ref, b1_ref) * long_conv(v, 0, fbias[0:1])
        z = short_conv(p2_ref[rs, :], w2_ref, b2_ref) * long_conv(z, 1, fbias[1:2])
        o_ref[rs, :] = z.astype(o_ref.dtype)


def _hyena(st, proj, col0, width, e, w_short, b_short, mats, filters, fbias):
    l = st.l
    tc = _tile(width, 1024 if l <= 256 else 256)
    per_step = 2 if l > 256 and st.n_seq % 2 == 0 else 1
    nc = width // tc
    c0 = col0 // tc
    rows = per_step * l
    once = pl.Buffered(1)
    p_specs = [pl.BlockSpec((rows, tc), lambda c, b, part=part: (b, c0 + part * nc + c)) for part in range(3)]
    w_specs = [pl.BlockSpec((None, SHORT_CONV, tc), lambda c, b, part=part: (e, 0, part * nc + c))
               for part in range(3)]
    b_specs = [pl.BlockSpec((None, 1, tc), lambda c, b, part=part: (e, 0, part * nc + c)) for part in range(3)]
    mat_specs = [pl.BlockSpec((l, l), lambda c, b: (0, 0), pipeline_mode=once)] * 4
    k_specs = [pl.BlockSpec((HYENA_ORDER, l, tc), lambda c, b: (0, 0, c), pipeline_mode=once)] * 3
    return pl.pallas_call(
        functools.partial(_hyena_kernel, l=l),
        grid=(nc, st.n_seq // per_step),
        in_specs=p_specs + w_specs + b_specs + mat_specs + k_specs
        + [pl.BlockSpec((None, HYENA_ORDER, tc), lambda c, b: (e, 0, c))],
        out_specs=pl.BlockSpec((rows, tc), lambda c, b: (b, c)),
        out_shape=jax.ShapeDtypeStruct((st.n, width), BF16),
        compiler_params=_params("parallel", "arbitrary"),
        name="hyena",
    )(proj, proj, proj, w_short, w_short, w_short, *([b_short.reshape(b_short.shape[0], 1, -1)] * 3),
      *[a.astype(BF16) for a in mats], *filters, fbias)


def _seq_dft_kernel(cos_ref, sin_ref, h_ref, a_ref, b_ref):
    h = h_ref[...]
    a_ref[...] = _dot(cos_ref[...], h).astype(a_ref.dtype)
    b_ref[...] = _dot(sin_ref[...], h).astype(b_ref.dtype)


def _seq_dft(st, h):
    l = st.l
    d = h.shape[1]
    tn = _tile(d, 2048 if l <= 256 else 512)
    cos, sin = _cos_sin(l, l, l)
    cos = (cos * l ** -0.5).astype(BF16)
    sin = (sin * l ** -0.5).astype(BF16)
    out = jax.ShapeDtypeStruct((st.n, d), BF16)
    return pl.pallas_call(
        _seq_dft_kernel,
        grid=(st.n_seq, d // tn),
        in_specs=[pl.BlockSpec((l, l), lambda b, j: (0, 0))] * 2 + [pl.BlockSpec((l, tn), lambda b, j: (b, j))],
        out_specs=[pl.BlockSpec((l, tn), lambda b, j: (b, j))] * 2,
        out_shape=[out, out],
        compiler_params=_params("parallel", "arbitrary"),
        name="seq_dft",
    )(cos, sin, h)


def _fnet_out_kernel(a_ref, b_ref, cos_ref, sin_ref, w_ref, res_ref, mod_ref, o_ref, mix_ref, *, gi, gw):
    @pl.when(pl.program_id(1) == 0)
    def _():
        cos = cos_ref[...]
        sin = sin_ref[...]
        for g in range(a_ref.shape[1] // gw):
            sl = slice(g * gw, (g + 1) * gw)
            mix_ref[:, sl] = (_dot(a_ref[:, sl], cos) - _dot(b_ref[:, sl], sin)).astype(mix_ref.dtype)

    o_ref[...] = res_ref[...] + mod_ref[0][gi:gi + 1] * _dot(mix_ref[...], w_ref[...])


def _fnet_out(st, a, b, w, o, res, mods, gi):
    m, d = a.shape
    gw = d // FNET_GROUPS
    tm = st.row_tile(512)
    tn = _tile(d, 1024)
    cos, sin = _cos_sin(gw, gw, gw)
    cos = (cos * gw ** -0.5).astype(BF16)
    sin = (sin * gw ** -0.5).astype(BF16)
    return pl.pallas_call(
        functools.partial(_fnet_out_kernel, gi=gi, gw=gw),
        grid=(m // tm, d // tn),
        in_specs=[
            pl.BlockSpec((tm, d), lambda i, j: (i, 0)),
            pl.BlockSpec((tm, d), lambda i, j: (i, 0)),
            pl.BlockSpec((gw, gw), lambda i, j: (0, 0)),
            pl.BlockSpec((gw, gw), lambda i, j: (0, 0)),
            pl.BlockSpec((None, d, tn), lambda i, j: (o, 0, j)),
            pl.BlockSpec((tm, tn), lambda i, j: (i, j)),
            pl.BlockSpec((1, N_MOD, tn), lambda i, j: (st.cond(i, tm), 0, j)),
        ],
        out_specs=pl.BlockSpec((tm, tn), lambda i, j: (i, j)),
        out_shape=jax.ShapeDtypeStruct((m, d), F32),
        scratch_shapes=[pltpu.VMEM((tm, d), BF16)],
        compiler_params=_params("parallel", "arbitrary"),
        name="fnet_out",
    )(a, b, cos, sin, w, res, mods)


def _rms_kernel(x_ref, g_ref, o_ref):
    rows = BF16_SUBLANES
    g = g_ref[...]

    def body(c, carry):
        r = pl.multiple_of(c * rows, rows)
        x = x_ref[pl.ds(r, rows), :]
        o_ref[pl.ds(r, rows), :] = x * lax.rsqrt(jnp.mean(x * x, axis=-1, keepdims=True) + EPS) * g
        return carry

    lax.fori_loop(0, x_ref.shape[0] // rows, body, 0)


def _rms_norm(x, g):
    m, d = x.shape
    tm = _tile(m, 256, BF16_SUBLANES)
    return pl.pallas_call(
        _rms_kernel,
        grid=(m // tm,),
        in_specs=[pl.BlockSpec((tm, d), lambda i: (i, 0)), pl.BlockSpec((1, d), lambda i: (0, 0))],
        out_specs=pl.BlockSpec((tm, d), lambda i: (i, 0)),
        out_shape=jax.ShapeDtypeStruct((m, d), F32),
        compiler_params=_params("parallel"),
        name="final_norm",
    )(x, g.reshape(1, d))


def kernel(x_prompt, x_sample, cache_k, cache_v, c, c_ctx, w_mod, b_mod, norm_g, ffn_w1, ffn_w3, ffn_w2,
           w_in, w_out, rpb, w_short, b_short, filt_w1, filt_b1, filt_w2, filt_b2, filt_w3, filt_freq,
           filt_bias, w_fnet, final_g):
    b_p, l_p, d = x_prompt.shape
    b_s, l_s, _ = x_sample.shape
    depth = w_mod.shape[0]
    attn_w = cache_k.shape[-2] * cache_k.shape[-1]
    hyena_w = filt_bias.shape[-1]
    n_heads = attn_w // HEAD_DIM
    past = cache_k.shape[2]
    assert 1 + b_s <= COND_ROWS

    streams = (_Stream(b_p, l_p, 0, True), _Stream(b_s, l_s, 1, False))
    xs = [x_prompt.reshape(b_p * l_p, d), x_sample.reshape(b_s * l_s, d)]
    conds = jnp.concatenate([c_ctx[None, :], c, jnp.zeros((COND_ROWS - 1 - b_s, d), F32)], axis=0)
    w1b, w3b, w2b = ffn_w1.astype(BF16), ffn_w3.astype(BF16), ffn_w2.astype(BF16)
    w_in_b, w_out_b, w_fnet_b = w_in.astype(BF16), w_out.astype(BF16), w_fnet.astype(BF16)
    rfft_mats = {st.l: _rfft_matrices(st.l) for st in streams}

    new_k, new_v = [], []
    for l in range(depth):
        mods = _adaln(conds, w_mod, b_mod, l)
        xs = [_ffn(st, x, mods, norm_g[l, 0], w1b, w3b, w2b, l, 0, 0) for st, x in zip(streams, xs)]
        if l % 2 == 0:
            e = l // 2
            projs = [_norm_mm(st, x, mods, norm_g[l, 1], w_in_b, e, 3) for st, x in zip(streams, xs)]
            attn = [_ctx_attn(streams[0], projs[0], attn_w),
                    _na_attn(streams[1], projs[1], attn_w, cache_k[:, e].reshape(b_s * past, attn_w),
                             cache_v[:, e].reshape(b_s * past, attn_w), rpb[e])]
            for i, st in enumerate(streams):
                mats = rfft_mats[st.l]
                filters = _hyena_filters(st.l, hyena_w, filt_w1[e], filt_b1[e], filt_w2[e], filt_b2[e],
                                         filt_w3[e], filt_freq[e], mats[0], mats[1])
                hy = _hyena(st, projs[i], 3 * attn_w, hyena_w, e, w_short, b_short, mats, filters, filt_bias)
                xs[i] = _mm_res(st, attn[i], hy, w_out_b, e, xs[i], mods, 5)
            new_k.append(projs[0][:, attn_w:2 * attn_w].reshape(b_p, l_p, n_heads, HEAD_DIM))
            new_v.append(projs[0][:, 2 * attn_w:3 * attn_w].reshape(b_p, l_p, n_heads, HEAD_DIM))
        else:
            o = l // 2
            for i, st in enumerate(streams):
                h = _mod_norm(st, xs[i], mods, norm_g[l, 1], 3)
                a, b = _seq_dft(st, h)
                xs[i] = _fnet_out(st, a, b, w_fnet_b, o, xs[i], mods, 5)
        xs = [_ffn(st, x, mods, norm_g[l, 2], w1b, w3b, w2b, l, 1, 6) for st, x in zip(streams, xs)]

    y_prompt = _rms_norm(xs[0], final_g).reshape(b_p, l_p, d)
    y_sample = _rms_norm(xs[1], final_g).reshape(b_s, l_s, d)
    return y_prompt, y_sample, jnp.stack(new_k, axis=1), jnp.stack(new_v, axis=1)
```

```python
import dataclasses
import functools
import math

import jax
import jax.numpy as jnp
from jax import lax
from jax.experimental import pallas as pl
from jax.experimental.pallas import tpu as pltpu

F32 = jnp.float32
BF16 = jnp.bfloat16
HIGHEST = lax.Precision.HIGHEST

N_MOD = 9
GRID_W = 64
WIN_H = 8
WIN_W = 16
HEAD_DIM = 128
SHORT_CONV = 3
FILTER_EMB = 33
HYENA_ORDER = 2
DECAY_TARGET = 1e-2
FAST_DECAY_PCT = 0.3
SLOW_DECAY_PCT = 1.5
FNET_GROUPS = 8
EPS = 1e-6
NEG_INF = -1e30

V7X_VMEM_BYTES = 64 * 2**20
VMEM_LIMIT = V7X_VMEM_BYTES - 8 * 2**20
LANE = 128
BF16_SUBLANES = 16
COND_ROWS = 16


def _tile(n, pref, unit=LANE):
    if n <= pref:
        return n
    t = (pref // unit) * unit
    while t > unit and n % t:
        t -= unit
    assert n % t == 0, (n, pref)
    return t


def _params(*sem):
    return pltpu.CompilerParams(dimension_semantics=sem, vmem_limit_bytes=VMEM_LIMIT)


@dataclasses.dataclass(frozen=True)
class _Stream:
    n_seq: int
    l: int
    cond0: int
    shared: bool

    @property
    def n(self):
        return self.n_seq * self.l

    def row_tile(self, pref):
        return _tile(self.n if self.shared else self.l, pref, BF16_SUBLANES)

    def cond(self, i, tm):
        return self.cond0 if self.shared else self.cond0 + i // (self.l // tm)


def _fill_mod_norm(x_ref, g_ref, mod_ref, h_ref, rs_ref, mi):
    rows = BF16_SUBLANES
    tm, d = x_ref.shape
    cw = _tile(d, 512)

    def stats(c, carry):
        r = pl.multiple_of(c * rows, rows)
        x = x_ref[pl.ds(r, rows), :]
        rs = lax.rsqrt(jnp.mean(x * x, axis=-1, keepdims=True) + EPS)
        rs_ref[pl.ds(r, rows), :] = jnp.broadcast_to(rs, (rows, LANE))
        return carry

    lax.fori_loop(0, tm // rows, stats, 0, unroll=4)

    m = mod_ref[0]
    for t in range(d // cw):
        cs = slice(t * cw, (t + 1) * cw)
        gain = g_ref[:, cs] * (1.0 + m[mi + 1:mi + 2, cs])
        shift = m[mi:mi + 1, cs]

        def apply(c, carry):
            r = pl.multiple_of(c * rows, rows)
            rs = rs_ref[pl.ds(r, rows), :]
            rs = jnp.concatenate([rs] * (cw // LANE), axis=1) if cw > LANE else rs[:, :cw]
            h_ref[pl.ds(r, rows), cs] = (x_ref[pl.ds(r, rows), cs] * rs * gain + shift).astype(h_ref.dtype)
            return carry

        lax.fori_loop(0, tm // rows, apply, 0)


def _dot(a, b):
    return jnp.dot(a, b, preferred_element_type=F32)


def _dot_hp(a, b):
    return jnp.dot(a, b, preferred_element_type=F32, precision=HIGHEST)


def _dot_nt(a, b):
    return lax.dot_general(a, b, (((1,), (1,)), ((), ())), preferred_element_type=F32)


def _adaln_kernel(c_ref, w_ref, b_ref, o_ref):
    c = c_ref[...]
    s = c * (1.0 / (1.0 + jnp.exp(-c)))
    s_hi = s.astype(BF16)
    s_lo = (s - s_hi.astype(F32)).astype(BF16)
    w = w_ref[...].astype(BF16)
    o_ref[...] = _dot(s_hi, w) + _dot(s_lo, w) + b_ref[...]


def _adaln(conds, w_mod, b_mod, l):
    d = conds.shape[1]
    n = w_mod.shape[-1]
    tn = _tile(n, 1024)
    out = pl.pallas_call(
        _adaln_kernel,
        grid=(n // tn,),
        in_specs=[
            pl.BlockSpec((COND_ROWS, d), lambda j: (0, 0)),
            pl.BlockSpec((None, d, tn), lambda j: (l, 0, j)),
            pl.BlockSpec((None, 1, tn), lambda j: (l, 0, j)),
        ],
        out_specs=pl.BlockSpec((COND_ROWS, tn), lambda j: (0, j)),
        out_shape=jax.ShapeDtypeStruct((COND_ROWS, n), F32),
        compiler_params=_params("arbitrary"),
        name="adaln",
    )(conds, w_mod, b_mod.reshape(b_mod.shape[0], 1, n))
    return out.reshape(COND_ROWS, N_MOD, d)


def _ffn_kernel(x_ref, mod_ref, g_ref, w1_ref, w3_ref, w2_ref, *rest, mi):
    j = pl.program_id(1)
    f32_tiles, bf16_tiles = (), ()
    if len(rest) > 3:
        f32_tiles, (o_ref, *bf16_tiles), (h_ref, rs_ref) = rest[:3], rest[3:7], rest[7:]
    else:
        o_ref, h_ref, rs_ref = rest

    @pl.when(j == 0)
    def _():
        _fill_mod_norm(x_ref, g_ref, mod_ref, h_ref, rs_ref, mi)
        o_ref[...] = jnp.zeros_like(o_ref)

    for src, dst in zip(f32_tiles, bf16_tiles):
        dst[...] = src[...].astype(dst.dtype)

    h = h_ref[...]
    a = _dot(h, w1_ref[...])
    b = _dot(h, w3_ref[...])
    act = (a * (1.0 / (1.0 + jnp.exp(-a))) * b).astype(BF16)
    o_ref[...] += _dot(act, w2_ref[...])

    @pl.when(j == pl.num_programs(1) - 1)
    def _():
        rows = BF16_SUBLANES
        half_gate = 0.5 * mod_ref[0][mi + 2:mi + 3]

        def body(c, carry):
            r = pl.multiple_of(c * rows, rows)
            o_ref[pl.ds(r, rows), :] = x_ref[pl.ds(r, rows), :] + half_gate * o_ref[pl.ds(r, rows), :]
            return carry

        lax.fori_loop(0, o_ref.shape[0] // rows, body, 0)


FFN_TF = 256


def _ffn_cast_ok(st, d, f):
    tm = st.row_tile(512)
    nm = st.n // tm
    return f % FFN_TF == 0 and d % nm == 0 and (d // nm) % LANE == 0


def _ffn(st, x, mods, g, w1, w3, w2, mi, cast_next=None):
    m, d = x.shape
    f = w1.shape[-1]
    tm = st.row_tile(512)
    tf = _tile(f, FFN_TF)
    nm, nf = m // tm, f // tf
    in_specs = [
        pl.BlockSpec((tm, d), lambda i, j: (i, 0)),
        pl.BlockSpec((1, N_MOD, d), lambda i, j: (st.cond(i, tm), 0, 0)),
        pl.BlockSpec((1, d), lambda i, j: (0, 0)),
        pl.BlockSpec((d, tf), lambda i, j: (0, j)),
        pl.BlockSpec((d, tf), lambda i, j: (0, j)),
        pl.BlockSpec((tf, d), lambda i, j: (j, 0)),
    ]
    out_specs = [pl.BlockSpec((tm, d), lambda i, j: (i, 0))]
    out_shape = [jax.ShapeDtypeStruct((m, d), F32)]
    args = [x, mods, g.reshape(1, d), w1, w3, w2]
    if cast_next is not None:
        n1, n3, n2, ln, sn = cast_next
        dr = d // nm
        in_specs += [pl.BlockSpec((None, None, dr, tf), lambda i, j: (ln, sn, i, j)),
                     pl.BlockSpec((None, None, dr, tf), lambda i, j: (ln, sn, i, j)),
                     pl.BlockSpec((None, None, tf, dr), lambda i, j: (ln, sn, j, i))]
        out_specs += [pl.BlockSpec((dr, tf), lambda i, j: (i, j)), pl.BlockSpec((dr, tf), lambda i, j: (i, j)),
                      pl.BlockSpec((tf, dr), lambda i, j: (j, i))]
        out_shape += [jax.ShapeDtypeStruct((d, f), BF16), jax.ShapeDtypeStruct((d, f), BF16),
                      jax.ShapeDtypeStruct((f, d), BF16)]
        args += [n1, n3, n2]
    out = pl.pallas_call(
        functools.partial(_ffn_kernel, mi=mi),
        grid=(nm, nf),
        in_specs=in_specs,
        out_specs=out_specs,
        out_shape=out_shape,
        scratch_shapes=[pltpu.VMEM((tm, d), BF16), pltpu.VMEM((tm, LANE), F32)],
        compiler_params=_params("parallel", "arbitrary"),
        name="ffn",
    )(*args)
    return out[0] if cast_next is None else (out[0], tuple(out[1:]))


def _norm_mm_kernel(x_ref, mod_ref, g_ref, w_ref, o_ref, h_ref, rs_ref, *, mi):
    @pl.when(pl.program_id(1) == 0)
    def _():
        _fill_mod_norm(x_ref, g_ref, mod_ref, h_ref, rs_ref, mi)

    o_ref[...] = _dot(h_ref[...], w_ref[...])


def _norm_mm(st, x, mods, g, w, e, mi):
    m, d = x.shape
    n = w.shape[-1]
    tm = st.row_tile(1024)
    tn = _tile(n, 512)
    return pl.pallas_call(
        functools.partial(_norm_mm_kernel, mi=mi),
        grid=(m // tm, n // tn),
        in_specs=[
            pl.BlockSpec((tm, d), lambda i, j: (i, 0)),
            pl.BlockSpec((1, N_MOD, d), lambda i, j: (st.cond(i, tm), 0, 0)),
            pl.BlockSpec((1, d), lambda i, j: (0, 0)),
            pl.BlockSpec((None, d, tn), lambda i, j: (e, 0, j)),
        ],
        out_specs=pl.BlockSpec((tm, tn), lambda i, j: (i, j)),
        out_shape=jax.ShapeDtypeStruct((m, n), F32),
        scratch_shapes=[pltpu.VMEM((tm, d), BF16), pltpu.VMEM((tm, LANE), F32)],
        compiler_params=_params("parallel", "arbitrary"),
        name="in_proj",
    )(x, mods, g.reshape(1, d), w)


def _mod_norm_kernel(x_ref, mod_ref, g_ref, h_ref, rs_ref, *, mi):
    _fill_mod_norm(x_ref, g_ref, mod_ref, h_ref, rs_ref, mi)


def _mod_norm(st, x, mods, g, mi):
    m, d = x.shape
    tm = st.row_tile(256)
    return pl.pallas_call(
        functools.partial(_mod_norm_kernel, mi=mi),
        grid=(m // tm,),
        in_specs=[
            pl.BlockSpec((tm, d), lambda i: (i, 0)),
            pl.BlockSpec((1, N_MOD, d), lambda i: (st.cond(i, tm), 0, 0)),
            pl.BlockSpec((1, d), lambda i: (0, 0)),
        ],
        out_specs=pl.BlockSpec((tm, d), lambda i: (i, 0)),
        out_shape=jax.ShapeDtypeStruct((m, d), BF16),
        scratch_shapes=[pltpu.VMEM((tm, LANE), F32)],
        compiler_params=_params("parallel"),
        name="mod_norm",
    )(x, mods, g.reshape(1, d))


def _mm_res_kernel(a0_ref, a1_ref, w0_ref, w1_ref, res_ref, mod_ref, o_ref, *, gi):
    mix = _dot(a0_ref[...], w0_ref[...]) + _dot(a1_ref[...], w1_ref[...])
    o_ref[...] = res_ref[...] + mod_ref[0][gi:gi + 1] * mix


def _mm_res(st, a0, a1, w, e, res, mods, gi):
    m, k = a0.shape
    assert a1.shape == (m, k) and w.shape[1] == 2 * k
    n = w.shape[-1]
    tm = st.row_tile(1024)
    tn = _tile(n, 1024)
    return pl.pallas_call(
        functools.partial(_mm_res_kernel, gi=gi),
        grid=(m // tm, n // tn),
        in_specs=[
            pl.BlockSpec((tm, k), lambda i, j: (i, 0)),
            pl.BlockSpec((tm, k), lambda i, j: (i, 0)),
            pl.BlockSpec((None, k, tn), lambda i, j: (e, 0, j)),
            pl.BlockSpec((None, k, tn), lambda i, j: (e, 1, j)),
            pl.BlockSpec((tm, tn), lambda i, j: (i, j)),
            pl.BlockSpec((1, N_MOD, tn), lambda i, j: (st.cond(i, tm), 0, j)),
        ],
        out_specs=pl.BlockSpec((tm, tn), lambda i, j: (i, j)),
        out_shape=jax.ShapeDtypeStruct((m, n), F32),
        compiler_params=_params("parallel", "arbitrary"),
        name="out_proj",
    )(a0, a1, w, w, res, mods)


def _ctx_attn_kernel(q_ref, k_ref, v_ref, o_ref, *, n_heads, scale):
    for h in range(n_heads):
        sl = slice(h * HEAD_DIM, (h + 1) * HEAD_DIM)
        q = q_ref[:, sl].astype(BF16)
        k = k_ref[:, sl].astype(BF16)
        v = v_ref[:, sl].astype(BF16)
        s = _dot_nt(q, k) * scale
        e = jnp.exp(s - jnp.max(s, axis=-1, keepdims=True))
        p = e * (1.0 / jnp.sum(e, axis=-1, keepdims=True))
        o_ref[:, sl] = _dot(p.astype(BF16), v).astype(o_ref.dtype)


def _ctx_attn(st, proj, width):
    l = st.l
    return pl.pallas_call(
        functools.partial(_ctx_attn_kernel, n_heads=width // HEAD_DIM, scale=HEAD_DIM ** -0.5),
        grid=(st.n_seq,),
        in_specs=[pl.BlockSpec((l, width), lambda b, part=part: (b, part)) for part in range(3)],
        out_specs=pl.BlockSpec((l, width), lambda b: (b, 0)),
        out_shape=jax.ShapeDtypeStruct((st.n, width), BF16),
        compiler_params=_params("parallel"),
        name="ctx_attn",
    )(proj, proj, proj)


def _na_window(r, rows, kh):
    start = min(max(r - kh // 2, 0), rows - kh)
    lo = start - start % 2
    hi = start + kh + (start + kh) % 2
    return start, lo, hi


def _na_attn_kernel(q_ref, k_ref, v_ref, ck_ref, cv_ref, tab_ref, o_ref, pl_ref, pc_ref, *, rows, kh, scale):
    q = q_ref[...].astype(BF16)
    s_lat_all = _dot_nt(q, k_ref[...].astype(BF16))
    s_ctx_all = _dot_nt(q, ck_ref[...].astype(BF16))
    for r in range(rows):
        start, lo, hi = _na_window(r, rows, kh)
        qs = slice(r * GRID_W, (r + 1) * GRID_W)
        pieces = []
        for p in range(lo // 2, hi // 2):
            ok0 = start <= 2 * p < start + kh
            ok1 = start <= 2 * p + 1 < start + kh
            dr = 2 * p - r + WIN_H - 1
            pieces.append(tab_ref[0, 0, dr] if ok0 and ok1 else tab_ref[0, 1, dr] if ok0 else tab_ref[0, 2, dr + 1])
        s_lat = s_lat_all[qs, lo * GRID_W:hi * GRID_W] * scale + jnp.concatenate(pieces, axis=1)
        s_ctx = s_ctx_all[qs, :] * scale
        mx = jnp.maximum(jnp.max(s_lat, axis=-1, keepdims=True), jnp.max(s_ctx, axis=-1, keepdims=True))
        e_lat = jnp.exp(s_lat - mx)
        e_ctx = jnp.exp(s_ctx - mx)
        inv = 1.0 / (jnp.sum(e_lat, axis=-1, keepdims=True) + jnp.sum(e_ctx, axis=-1, keepdims=True))
        p_lat = (e_lat * inv).astype(BF16)
        parts = []
        if lo > 0:
            parts.append(jnp.zeros((GRID_W, lo * GRID_W), BF16))
        parts.append(p_lat)
        if hi < rows:
            parts.append(jnp.zeros((GRID_W, (rows - hi) * GRID_W), BF16))
        pl_ref[qs, :] = jnp.concatenate(parts, axis=1) if len(parts) > 1 else p_lat
        pc_ref[qs, :] = (e_ctx * inv).astype(BF16)
    o = _dot(pl_ref[...], v_ref[...].astype(BF16)) + _dot(pc_ref[...], cv_ref[...].astype(BF16))
    o_ref[...] = o.astype(o_ref.dtype)


def _na_bias_table(rpb):
    col = jnp.arange(GRID_W)
    col_start = jnp.clip(col - WIN_W // 2, 0, GRID_W - WIN_W)
    col_ok = (col[None, :] >= col_start[:, None]) & (col[None, :] < col_start[:, None] + WIN_W)
    dc_idx = jnp.clip(col[None, :] - col[:, None] + WIN_W - 1, 0, 2 * WIN_W - 2)
    t = jnp.where(col_ok, rpb.astype(F32)[:, :, dc_idx], NEG_INF)
    neg = jnp.full_like(t, NEG_INF)
    nxt = jnp.concatenate([t[:, 1:], neg[:, :1]], axis=1)
    return jnp.stack([jnp.concatenate([t, nxt], axis=-1), jnp.concatenate([t, neg], axis=-1),
                      jnp.concatenate([neg, t], axis=-1)], axis=1)


def _na_attn(st, proj, width, cache_k, cache_v, rpb):
    l = st.l
    n_heads = width // HEAD_DIM
    rows = l // GRID_W
    kh = min(WIN_H, rows)
    assert rows % 2 == 0 and kh % 2 == 0
    past = cache_k.shape[0] // st.n_seq
    tab = _na_bias_table(rpb)
    qkv_specs = [pl.BlockSpec((l, HEAD_DIM), lambda b, h, part=part: (b, part * n_heads + h)) for part in range(3)]
    return pl.pallas_call(
        functools.partial(_na_attn_kernel, rows=rows, kh=kh, scale=HEAD_DIM ** -0.5),
        grid=(st.n_seq, n_heads),
        in_specs=qkv_specs + [
            pl.BlockSpec((past, HEAD_DIM), lambda b, h: (b, h)),
            pl.BlockSpec((past, HEAD_DIM), lambda b, h: (b, h)),
            pl.BlockSpec((1,) + tab.shape[1:], lambda b, h: (h, 0, 0, 0, 0)),
        ],
        out_specs=pl.BlockSpec((l, HEAD_DIM), lambda b, h: (b, h)),
        out_shape=jax.ShapeDtypeStruct((st.n, width), BF16),
        scratch_shapes=[pltpu.VMEM((l, l), BF16), pltpu.VMEM((l, past), BF16)],
        compiler_params=_params("parallel", "arbitrary"),
        name="na_attn",
    )(proj, proj, proj, cache_k, cache_v, tab)


def _cos_sin(n_rows, n_cols, period):
    r = lax.broadcasted_iota(jnp.int32, (n_rows, n_cols), 0)
    c = lax.broadcasted_iota(jnp.int32, (n_rows, n_cols), 1)
    ang = ((r * c) % period).astype(F32) * (2.0 * math.pi / period)
    return jnp.cos(ang), jnp.sin(ang)


def _rfft_matrices(l):
    n = 2 * l
    cos, sin = _cos_sin(l, l, n)
    alt = jnp.where(lax.broadcasted_iota(jnp.int32, (l, l), 1) % 2 == 0, 1.0, -1.0).astype(F32)
    k_is0 = lax.broadcasted_iota(jnp.int32, (l, l), 0) == 0
    fwd_re = cos
    fwd_im = jnp.where(k_is0, alt, -sin)
    bin_is0 = lax.broadcasted_iota(jnp.int32, (l, l), 1) == 0
    inv_re = jnp.where(bin_is0, 1.0 / n, 2.0 / n) * cos
    inv_im = jnp.where(bin_is0, alt.T / n, (-2.0 / n) * sin)
    return fwd_re, fwd_im, inv_re, inv_im


def _filter_kernel(z_ref, t_ref, w1_ref, b1_ref, w2_ref, b2_ref, fq_ref, w3f_ref, w3b_ref, dl_ref,
                   fre_ref, fim_ref, kr_ref, ki_ref, kn_ref):
    fq = fq_ref[...]
    hid = jnp.sin(fq[0:1] * (_dot_hp(z_ref[...], w1_ref[...]) + b1_ref[...]))
    hid = jnp.sin(fq[1:2] * (_dot_hp(hid, w2_ref[...]) + b2_ref[...]))
    decay = jnp.exp(-t_ref[...] * dl_ref[...])
    row = lax.broadcasted_iota(jnp.int32, kr_ref.shape, 0)
    h_fwd = _dot_hp(hid, w3f_ref[...]) * decay
    h_bwd = jnp.where(row == 0, 0.0, _dot_hp(hid, w3b_ref[...]) * decay)
    nrm = lax.rsqrt(jnp.sum(h_fwd * h_fwd + h_bwd * h_bwd, axis=0, keepdims=True) + EPS)
    even = (h_fwd + h_bwd) * nrm
    odd = (h_fwd - h_bwd) * nrm
    k_re = _dot_hp(fre_ref[...], even)
    k_im = _dot_hp(fim_ref[...], odd)
    nyq = jnp.sum(jnp.where(row % 2 == 0, even, -even), axis=0, keepdims=True)
    kr_ref[...] = k_re
    ki_ref[...] = jnp.where(row == 0, 0.0, k_im)
    kn_ref[...] = jnp.where(row == 0, nyq, k_re)


def _pad_to(a, shape):
    return jnp.pad(a.astype(F32), [(0, s - n) for n, s in zip(a.shape, shape)])


def _hyena_filters(l, width, w1, b1, w2, b2, w3, freq, fwd_re, fwd_im):
    hid = w2.shape[0]
    hp = LANE * pl.cdiv(hid, LANE)
    n_bands = (FILTER_EMB - 1) // 2
    t = jnp.linspace(0.0, 1.0, l, dtype=F32)[:, None]
    w = (2.0 * math.pi / l) * jnp.arange(l, dtype=F32)[:, None]
    bands = jnp.linspace(1e-4, n_bands - 1, n_bands, dtype=F32)[None, :]
    z = _pad_to(jnp.concatenate([t, jnp.cos(bands * w), -jnp.sin(bands * w)], axis=-1), (l, LANE))
    deltas = jnp.abs(jnp.linspace(math.log(DECAY_TARGET) / FAST_DECAY_PCT,
                                  math.log(DECAY_TARGET) / SLOW_DECAY_PCT, width, dtype=F32))[None, :]
    tc = _tile(width, 512)
    nc = width // tc
    small = lambda shape: pl.BlockSpec(shape, lambda o, c: (0,) * len(shape))
    out = jax.ShapeDtypeStruct((HYENA_ORDER, l, width), F32)
    return pl.pallas_call(
        _filter_kernel,
        grid=(HYENA_ORDER, nc),
        in_specs=[small((l, LANE)), small((l, 1)), small((LANE, hp)), small((1, hp)), small((hp, hp)),
                  small((1, hp)), small((2, hp)),
                  pl.BlockSpec((hp, tc), lambda o, c: (0, 2 * o * nc + c)),
                  pl.BlockSpec((hp, tc), lambda o, c: (0, (2 * o + 1) * nc + c)),
                  pl.BlockSpec((1, tc), lambda o, c: (0, c)),
                  small((l, l)), small((l, l))],
        out_specs=[pl.BlockSpec((None, l, tc), lambda o, c: (o, 0, c))] * 3,
        out_shape=[out, out, out],
        compiler_params=_params("arbitrary", "arbitrary"),
        name="hyena_filters",
    )(z, t, _pad_to(w1, (LANE, hp)), _pad_to(b1[None, :], (1, hp)), _pad_to(w2, (hp, hp)),
      _pad_to(b2[None, :], (1, hp)), _pad_to(freq, (2, hp)), _pad_to(w3, (hp, w3.shape[1])),
      _pad_to(w3, (hp, w3.shape[1])), deltas, fwd_re, fwd_im)


def _hyena_kernel(pv_ref, p1_ref, p2_ref, wv_ref, w1_ref, w2_ref, bv_ref, b1_ref, b2_ref,
                  fre_ref, fim_ref, gre_ref, gim_ref, kr_ref, ki_ref, kn_ref, fbias_ref, o_ref, *, l):
    tc = o_ref.shape[1]
    row = lax.broadcasted_iota(jnp.int32, (l, tc), 0)

    def short_conv(p, w_ref, b_ref):
        prev = jnp.where(row == 0, 0.0, pltpu.roll(p, 1, 0))
        nxt = jnp.where(row == l - 1, 0.0, pltpu.roll(p, l - 1, 0))
        w = w_ref[...]
        out = b_ref[...] + prev * w[0:1]
        out = out + p * w[1:2]
        return out + nxt * w[2:3]

    def long_conv(u, o, bias):
        ub = u.astype(BF16)
        u_re = _dot(fre_ref[...], ub)
        u_im = _dot(fim_ref[...], ub)
        kr, ki, kn = kr_ref[o], ki_ref[o], kn_ref[o]
        w_re = (u_re * kr - u_im * ki).astype(BF16)
        w_im = (u_re * ki + u_im * kn).astype(BF16)
        y = _dot(gre_ref[...], w_re) + _dot(gim_ref[...], w_im)
        return y + u * bias

    fbias = fbias_ref[...]
    for s in range(o_ref.shape[0] // l):
        rs = slice(s * l, (s + 1) * l)
        v = short_conv(pv_ref[rs, :], wv_ref, bv_ref)
        z = short_conv(p1_ref[rs, :], w1_ref, b1_ref) * long_conv(v, 0, fbias[0:1])
        z = short_conv(p2_ref[rs, :], w2_ref, b2_ref) * long_conv(z, 1, fbias[1:2])
        o_ref[rs, :] = z.astype(o_ref.dtype)


def _hyena(st, proj, col0, width, e, w_short, b_short, mats, filters, fbias):
    l = st.l
    tc = _tile(width, 1024 if l <= 256 else 256)
    per_step = 2 if l > 256 and st.n_seq % 2 == 0 else 1
    nc = width // tc
    c0 = col0 // tc
    rows = per_step * l
    once = pl.Buffered(1)
    p_specs = [pl.BlockSpec((rows, tc), lambda c, b, part=part: (b, c0 + part * nc + c)) for part in range(3)]
    w_specs = [pl.BlockSpec((None, SHORT_CONV, tc), lambda c, b, part=part: (e, 0, part * nc + c))
               for part in range(3)]
    b_specs = [pl.BlockSpec((None, 1, tc), lambda c, b, part=part: (e, 0, part * nc + c)) for part in range(3)]
    mat_specs = [pl.BlockSpec((l, l), lambda c, b: (0, 0), pipeline_mode=once)] * 4
    k_specs = [pl.BlockSpec((HYENA_ORDER, l, tc), lambda c, b: (0, 0, c), pipeline_mode=once)] * 3
    return pl.pallas_call(
        functools.partial(_hyena_kernel, l=l),
        grid=(nc, st.n_seq // per_step),
        in_specs=p_specs + w_specs + b_specs + mat_specs + k_specs
        + [pl.BlockSpec((None, HYENA_ORDER, tc), lambda c, b: (e, 0, c))],
        out_specs=pl.BlockSpec((rows, tc), lambda c, b: (b, c)),
        out_shape=jax.ShapeDtypeStruct((st.n, width), BF16),
        compiler_params=_params("parallel", "arbitrary"),
        name="hyena",
    )(proj, proj, proj, w_short, w_short, w_short, *([b_short.reshape(b_short.shape[0], 1, -1)] * 3),
      *[a.astype(BF16) for a in mats], *filters, fbias)


def _seq_dft_kernel(cos_ref, sin_ref, h_ref, a_ref, b_ref):
    h = h_ref[...]
    a_ref[...] = _dot(cos_ref[...], h).astype(a_ref.dtype)
    b_ref[...] = _dot(sin_ref[...], h).astype(b_ref.dtype)


def _seq_dft(st, h):
    l = st.l
    d = h.shape[1]
    tn = _tile(d, 2048 if l <= 256 else 512)
    cos, sin = _cos_sin(l, l, l)
    cos = (cos * l ** -0.5).astype(BF16)
    sin = (sin * l ** -0.5).astype(BF16)
    out = jax.ShapeDtypeStruct((st.n, d), BF16)
    return pl.pallas_call(
        _seq_dft_kernel,
        grid=(st.n_seq, d // tn),
        in_specs=[pl.BlockSpec((l, l), lambda b, j: (0, 0))] * 2 + [pl.BlockSpec((l, tn), lambda b, j: (b, j))],
        out_specs=[pl.BlockSpec((l, tn), lambda b, j: (b, j))] * 2,
        out_shape=[out, out],
        compiler_params=_params("parallel", "arbitrary"),
        name="seq_dft",
    )(cos, sin, h)


def _fnet_out_kernel(a_ref, b_ref, cos_ref, sin_ref, w_ref, res_ref, mod_ref, o_ref, mix_ref, *, gi, gw):
    @pl.when(pl.program_id(1) == 0)
    def _():
        cos = cos_ref[...]
        sin = sin_ref[...]
        for g in range(a_ref.shape[1] // gw):
            sl = slice(g * gw, (g + 1) * gw)
            mix_ref[:, sl] = (_dot(a_ref[:, sl], cos) - _dot(b_ref[:, sl], sin)).astype(mix_ref.dtype)

    o_ref[...] = res_ref[...] + mod_ref[0][gi:gi + 1] * _dot(mix_ref[...], w_ref[...])


def _fnet_out(st, a, b, w, o, res, mods, gi):
    m, d = a.shape
    gw = d // FNET_GROUPS
    tm = st.row_tile(512)
    tn = _tile(d, 1024)
    cos, sin = _cos_sin(gw, gw, gw)
    cos = (cos * gw ** -0.5).astype(BF16)
    sin = (sin * gw ** -0.5).astype(BF16)
    return pl.pallas_call(
        functools.partial(_fnet_out_kernel, gi=gi, gw=gw),
        grid=(m // tm, d // tn),
        in_specs=[
            pl.BlockSpec((tm, d), lambda i, j: (i, 0)),
            pl.BlockSpec((tm, d), lambda i, j: (i, 0)),
            pl.BlockSpec((gw, gw), lambda i, j: (0, 0)),
            pl.BlockSpec((gw, gw), lambda i, j: (0, 0)),
            pl.BlockSpec((None, d, tn), lambda i, j: (o, 0, j)),
            pl.BlockSpec((tm, tn), lambda i, j: (i, j)),
            pl.BlockSpec((1, N_MOD, tn), lambda i, j: (st.cond(i, tm), 0, j)),
        ],
        out_specs=pl.BlockSpec((tm, tn), lambda i, j: (i, j)),
        out_shape=jax.ShapeDtypeStruct((m, d), F32),
        scratch_shapes=[pltpu.VMEM((tm, d), BF16)],
        compiler_params=_params("parallel", "arbitrary"),
        name="fnet_out",
    )(a, b, cos, sin, w, res, mods)


def _rms_kernel(x_ref, g_ref, o_ref):
    rows = BF16_SUBLANES
    g = g_ref[...]

    def body(c, carry):
        r = pl.multiple_of(c * rows, rows)
        x = x_ref[pl.ds(r, rows), :]
        o_ref[pl.ds(r, rows), :] = x * lax.rsqrt(jnp.mean(x * x, axis=-1, keepdims=True) + EPS) * g
        return carry

    lax.fori_loop(0, x_ref.shape[0] // rows, body, 0)


def _rms_norm(x, g):
    m, d = x.shape
    tm = _tile(m, 256, BF16_SUBLANES)
    return pl.pallas_call(
        _rms_kernel,
        grid=(m // tm,),
        in_specs=[pl.BlockSpec((tm, d), lambda i: (i, 0)), pl.BlockSpec((1, d), lambda i: (0, 0))],
        out_specs=pl.BlockSpec((tm, d), lambda i: (i, 0)),
        out_shape=jax.ShapeDtypeStruct((m, d), F32),
        compiler_params=_params("parallel"),
        name="final_norm",
    )(x, g.reshape(1, d))


def kernel(x_prompt, x_sample, cache_k, cache_v, c, c_ctx, w_mod, b_mod, norm_g, ffn_w1, ffn_w3, ffn_w2,
           w_in, w_out, rpb, w_short, b_short, filt_w1, filt_b1, filt_w2, filt_b2, filt_w3, filt_freq,
           filt_bias, w_fnet, final_g):
    b_p, l_p, d = x_prompt.shape
    b_s, l_s, _ = x_sample.shape
    depth = w_mod.shape[0]
    attn_w = cache_k.shape[-2] * cache_k.shape[-1]
    hyena_w = filt_bias.shape[-1]
    n_heads = attn_w // HEAD_DIM
    past = cache_k.shape[2]
    assert 1 + b_s <= COND_ROWS

    streams = (_Stream(b_p, l_p, 0, True), _Stream(b_s, l_s, 1, False))
    xs = [x_prompt.reshape(b_p * l_p, d), x_sample.reshape(b_s * l_s, d)]
    conds = jnp.concatenate([c_ctx[None, :], c, jnp.zeros((COND_ROWS - 1 - b_s, d), F32)], axis=0)
    w_in_b, w_out_b, w_fnet_b = w_in.astype(BF16), w_out.astype(BF16), w_fnet.astype(BF16)
    rfft_mats = {st.l: _rfft_matrices(st.l) for st in streams}

    ride_along = _ffn_cast_ok(streams[0], d, ffn_w1.shape[-1])
    ffn_w = {(0, 0): (ffn_w1[0, 0].astype(BF16), ffn_w3[0, 0].astype(BF16), ffn_w2[0, 0].astype(BF16))}

    def ffn_half_step(xs, mods, l, s):
        nxt = (l, 1) if s == 0 else (l + 1, 0)
        w = ffn_w[(l, s)]
        g, mi = norm_g[l, 2 * s], 6 * s
        if nxt[0] == depth:
            return [_ffn(st, x, mods, g, *w, mi) for st, x in zip(streams, xs)]
        if ride_along:
            x0, ffn_w[nxt] = _ffn(streams[0], xs[0], mods, g, *w, mi, cast_next=(ffn_w1, ffn_w3, ffn_w2, *nxt))
        else:
            x0 = _ffn(streams[0], xs[0], mods, g, *w, mi)
            ffn_w[nxt] = tuple(a[nxt].astype(BF16) for a in (ffn_w1, ffn_w3, ffn_w2))
        return [x0, _ffn(streams[1], xs[1], mods, g, *w, mi)]

    new_k, new_v = [], []
    for l in range(depth):
        mods = _adaln(conds, w_mod, b_mod, l)
        xs = ffn_half_step(xs, mods, l, 0)
        if l % 2 == 0:
            e = l // 2
            projs = [_norm_mm(st, x, mods, norm_g[l, 1], w_in_b, e, 3) for st, x in zip(streams, xs)]
            attn = [_ctx_attn(streams[0], projs[0], attn_w),
                    _na_attn(streams[1], projs[1], attn_w, cache_k[:, e].reshape(b_s * past, attn_w),
                             cache_v[:, e].reshape(b_s * past, attn_w), rpb[e])]
            for i, st in enumerate(streams):
                mats = rfft_mats[st.l]
                filters = _hyena_filters(st.l, hyena_w, filt_w1[e], filt_b1[e], filt_w2[e], filt_b2[e],
                                         filt_w3[e], filt_freq[e], mats[0], mats[1])
                hy = _hyena(st, projs[i], 3 * attn_w, hyena_w, e, w_short, b_short, mats, filters, filt_bias)
                xs[i] = _mm_res(st, attn[i], hy, w_out_b, e, xs[i], mods, 5)
            new_k.append(projs[0][:, attn_w:2 * attn_w].reshape(b_p, l_p, n_heads, HEAD_DIM))
            new_v.append(projs[0][:, 2 * attn_w:3 * attn_w].reshape(b_p, l_p, n_heads, HEAD_DIM))
        else:
            o = l // 2
            for i, st in enumerate(streams):
                h = _mod_norm(st, xs[i], mods, norm_g[l, 1], 3)
                a, b = _seq_dft(st, h)
                xs[i] = _fnet_out(st, a, b, w_fnet_b, o, xs[i], mods, 5)
        xs = ffn_half_step(xs, mods, l, 1)

    y_prompt = _rms_norm(xs[0], final_g).reshape(b_p, l_p, d)
    y_sample = _rms_norm(xs[1], final_g).reshape(b_s, l_s, d)
    return y_prompt, y_sample, jnp.stack(new_k, axis=1), jnp.stack(new_v, axis=1)
```

```python
import dataclasses
import functools
import math

import jax
import jax.numpy as jnp
from jax import lax
from jax.experimental import pallas as pl
from jax.experimental.pallas import tpu as pltpu

F32 = jnp.float32
BF16 = jnp.bfloat16
HIGHEST = lax.Precision.HIGHEST

N_MOD = 9
GRID_W = 64
WIN_H = 8
WIN_W = 16
HEAD_DIM = 128
SHORT_CONV = 3
FILTER_EMB = 33
HYENA_ORDER = 2
DECAY_TARGET = 1e-2
FAST_DECAY_PCT = 0.3
SLOW_DECAY_PCT = 1.5
FNET_GROUPS = 8
EPS = 1e-6
NEG_INF = -1e30

V7X_VMEM_BYTES = 64 * 2**20
VMEM_LIMIT = V7X_VMEM_BYTES - 8 * 2**20
LANE = 128
BF16_SUBLANES = 16
COND_ROWS = 16


def _tile(n, pref, unit=LANE):
    if n <= pref:
        return n
    t = (pref // unit) * unit
    while t > unit and n % t:
        t -= unit
    assert n % t == 0, (n, pref)
    return t


def _params(*sem):
    return pltpu.CompilerParams(dimension_semantics=sem, vmem_limit_bytes=VMEM_LIMIT)


@dataclasses.dataclass(frozen=True)
class _Stream:
    n_seq: int
    l: int
    cond0: int
    shared: bool

    @property
    def n(self):
        return self.n_seq * self.l

    def row_tile(self, pref):
        return _tile(self.n if self.shared else self.l, pref, BF16_SUBLANES)

    def cond(self, i, tm):
        return self.cond0 if self.shared else self.cond0 + i // (self.l // tm)


def _fill_mod_norm(x_ref, g_ref, mod_ref, h_ref, rs_ref, mi):
    rows = BF16_SUBLANES
    tm, d = x_ref.shape
    cw = _tile(d, 512)

    def stats(c, carry):
        r = pl.multiple_of(c * rows, rows)
        x = x_ref[pl.ds(r, rows), :]
        rs = lax.rsqrt(jnp.mean(x * x, axis=-1, keepdims=True) + EPS)
        rs_ref[pl.ds(r, rows), :] = jnp.broadcast_to(rs, (rows, LANE))
        return carry

    lax.fori_loop(0, tm // rows, stats, 0, unroll=4)

    m = mod_ref[0]
    for t in range(d // cw):
        cs = slice(t * cw, (t + 1) * cw)
        gain = g_ref[:, cs] * (1.0 + m[mi + 1:mi + 2, cs])
        shift = m[mi:mi + 1, cs]

        def apply(c, carry):
            r = pl.multiple_of(c * rows, rows)
            rs = rs_ref[pl.ds(r, rows), :]
            rs = jnp.concatenate([rs] * (cw // LANE), axis=1) if cw > LANE else rs[:, :cw]
            h_ref[pl.ds(r, rows), cs] = (x_ref[pl.ds(r, rows), cs] * rs * gain + shift).astype(h_ref.dtype)
            return carry

        lax.fori_loop(0, tm // rows, apply, 0)


def _dot(a, b):
    return jnp.dot(a, b, preferred_element_type=F32)


def _dot_hp(a, b):
    return jnp.dot(a, b, preferred_element_type=F32, precision=HIGHEST)


def _dot_nt(a, b):
    return lax.dot_general(a, b, (((1,), (1,)), ((), ())), preferred_element_type=F32)


def _adaln_kernel(c_ref, w_ref, b_ref, o_ref):
    c = c_ref[...]
    s = c * (1.0 / (1.0 + jnp.exp(-c)))
    s_hi = s.astype(BF16)
    s_lo = (s - s_hi.astype(F32)).astype(BF16)
    w = w_ref[...].astype(BF16)
    o_ref[...] = _dot(s_hi, w) + _dot(s_lo, w) + b_ref[...]


def _adaln(conds, w_mod, b_mod, l):
    d = conds.shape[1]
    n = w_mod.shape[-1]
    tn = _tile(n, 1024)
    out = pl.pallas_call(
        _adaln_kernel,
        grid=(n // tn,),
        in_specs=[
            pl.BlockSpec((COND_ROWS, d), lambda j: (0, 0)),
            pl.BlockSpec((None, d, tn), lambda j: (l, 0, j)),
            pl.BlockSpec((None, 1, tn), lambda j: (l, 0, j)),
        ],
        out_specs=pl.BlockSpec((COND_ROWS, tn), lambda j: (0, j)),
        out_shape=jax.ShapeDtypeStruct((COND_ROWS, n), F32),
        compiler_params=_params("arbitrary"),
        name="adaln",
    )(conds, w_mod, b_mod.reshape(b_mod.shape[0], 1, n))
    return out.reshape(COND_ROWS, N_MOD, d)


def _ffn_kernel(x_ref, mod_ref, g_ref, w1_ref, w3_ref, w2_ref, *rest, mi):
    j = pl.program_id(1)
    f32_tiles, bf16_tiles = (), ()
    if len(rest) > 3:
        f32_tiles, (o_ref, *bf16_tiles), (h_ref, rs_ref) = rest[:3], rest[3:7], rest[7:]
    else:
        o_ref, h_ref, rs_ref = rest

    @pl.when(j == 0)
    def _():
        _fill_mod_norm(x_ref, g_ref, mod_ref, h_ref, rs_ref, mi)
        o_ref[...] = jnp.zeros_like(o_ref)

    for src, dst in zip(f32_tiles, bf16_tiles):
        dst[...] = src[...].astype(dst.dtype)

    h = h_ref[...]
    a = _dot(h, w1_ref[...])
    b = _dot(h, w3_ref[...])
    act = (a * (1.0 / (1.0 + jnp.exp(-a))) * b).astype(BF16)
    o_ref[...] += _dot(act, w2_ref[...])

    @pl.when(j == pl.num_programs(1) - 1)
    def _():
        rows = BF16_SUBLANES
        half_gate = 0.5 * mod_ref[0][mi + 2:mi + 3]

        def body(c, carry):
            r = pl.multiple_of(c * rows, rows)
            o_ref[pl.ds(r, rows), :] = x_ref[pl.ds(r, rows), :] + half_gate * o_ref[pl.ds(r, rows), :]
            return carry

        lax.fori_loop(0, o_ref.shape[0] // rows, body, 0)


FFN_TF = 256


def _ffn_cast_ok(st, d, f):
    tm = st.row_tile(512)
    nm = st.n // tm
    return f % FFN_TF == 0 and d % nm == 0 and (d // nm) % LANE == 0


def _ffn(st, x, mods, g, w1, w3, w2, mi, cast_next=None):
    m, d = x.shape
    f = w1.shape[-1]
    tm = st.row_tile(512)
    tf = _tile(f, FFN_TF)
    nm, nf = m // tm, f // tf
    in_specs = [
        pl.BlockSpec((tm, d), lambda i, j: (i, 0)),
        pl.BlockSpec((1, N_MOD, d), lambda i, j: (st.cond(i, tm), 0, 0)),
        pl.BlockSpec((1, d), lambda i, j: (0, 0)),
        pl.BlockSpec((d, tf), lambda i, j: (0, j)),
        pl.BlockSpec((d, tf), lambda i, j: (0, j)),
        pl.BlockSpec((tf, d), lambda i, j: (j, 0)),
    ]
    out_specs = [pl.BlockSpec((tm, d), lambda i, j: (i, 0))]
    out_shape = [jax.ShapeDtypeStruct((m, d), F32)]
    args = [x, mods, g.reshape(1, d), w1, w3, w2]
    if cast_next is not None:
        n1, n3, n2, ln, sn = cast_next
        dr = d // nm
        in_specs += [pl.BlockSpec((None, None, dr, tf), lambda i, j: (ln, sn, i, j)),
                     pl.BlockSpec((None, None, dr, tf), lambda i, j: (ln, sn, i, j)),
                     pl.BlockSpec((None, None, tf, dr), lambda i, j: (ln, sn, j, i))]
        out_specs += [pl.BlockSpec((dr, tf), lambda i, j: (i, j)), pl.BlockSpec((dr, tf), lambda i, j: (i, j)),
                      pl.BlockSpec((tf, dr), lambda i, j: (j, i))]
        out_shape += [jax.ShapeDtypeStruct((d, f), BF16), jax.ShapeDtypeStruct((d, f), BF16),
                      jax.ShapeDtypeStruct((f, d), BF16)]
        args += [n1, n3, n2]
    out = pl.pallas_call(
        functools.partial(_ffn_kernel, mi=mi),
        grid=(nm, nf),
        in_specs=in_specs,
        out_specs=out_specs,
        out_shape=out_shape,
        scratch_shapes=[pltpu.VMEM((tm, d), BF16), pltpu.VMEM((tm, LANE), F32)],
        compiler_params=_params("parallel", "arbitrary"),
        name="ffn",
    )(*args)
    return out[0] if cast_next is None else (out[0], tuple(out[1:]))


def _norm_mm_kernel(x_ref, mod_ref, g_ref, w_ref, o_ref, h_ref, rs_ref, *, mi):
    @pl.when(pl.program_id(1) == 0)
    def _():
        _fill_mod_norm(x_ref, g_ref, mod_ref, h_ref, rs_ref, mi)

    o_ref[...] = _dot(h_ref[...], w_ref[...])


def _norm_mm(st, x, mods, g, w, e, mi):
    m, d = x.shape
    n = w.shape[-1]
    tm = st.row_tile(1024)
    tn = _tile(n, 512)
    return pl.pallas_call(
        functools.partial(_norm_mm_kernel, mi=mi),
        grid=(m // tm, n // tn),
        in_specs=[
            pl.BlockSpec((tm, d), lambda i, j: (i, 0)),
            pl.BlockSpec((1, N_MOD, d), lambda i, j: (st.cond(i, tm), 0, 0)),
            pl.BlockSpec((1, d), lambda i, j: (0, 0)),
            pl.BlockSpec((None, d, tn), lambda i, j: (e, 0, j)),
        ],
        out_specs=pl.BlockSpec((tm, tn), lambda i, j: (i, j)),
        out_shape=jax.ShapeDtypeStruct((m, n), F32),
        scratch_shapes=[pltpu.VMEM((tm, d), BF16), pltpu.VMEM((tm, LANE), F32)],
        compiler_params=_params("parallel", "arbitrary"),
        name="in_proj",
    )(x, mods, g.reshape(1, d), w)


def _mod_norm_kernel(x_ref, mod_ref, g_ref, h_ref, rs_ref, *, mi):
    _fill_mod_norm(x_ref, g_ref, mod_ref, h_ref, rs_ref, mi)


def _mod_norm(st, x, mods, g, mi):
    m, d = x.shape
    tm = st.row_tile(256)
    return pl.pallas_call(
        functools.partial(_mod_norm_kernel, mi=mi),
        grid=(m // tm,),
        in_specs=[
            pl.BlockSpec((tm, d), lambda i: (i, 0)),
            pl.BlockSpec((1, N_MOD, d), lambda i: (st.cond(i, tm), 0, 0)),
            pl.BlockSpec((1, d), lambda i: (0, 0)),
        ],
        out_specs=pl.BlockSpec((tm, d), lambda i: (i, 0)),
        out_shape=jax.ShapeDtypeStruct((m, d), BF16),
        scratch_shapes=[pltpu.VMEM((tm, LANE), F32)],
        compiler_params=_params("parallel"),
        name="mod_norm",
    )(x, mods, g.reshape(1, d))


def _mm_res_kernel(a0_ref, a1_ref, w0_ref, w1_ref, res_ref, mod_ref, o_ref, *, gi):
    mix = _dot(a0_ref[...], w0_ref[...]) + _dot(a1_ref[...], w1_ref[...])
    o_ref[...] = res_ref[...] + mod_ref[0][gi:gi + 1] * mix


def _mm_res(st, a0, a1, w, e, res, mods, gi):
    m, k = a0.shape
    assert a1.shape == (m, k) and w.shape[1] == 2 * k
    n = w.shape[-1]
    tm = st.row_tile(1024)
    tn = _tile(n, 1024)
    return pl.pallas_call(
        functools.partial(_mm_res_kernel, gi=gi),
        grid=(m // tm, n // tn),
        in_specs=[
            pl.BlockSpec((tm, k), lambda i, j: (i, 0)),
            pl.BlockSpec((tm, k), lambda i, j: (i, 0)),
            pl.BlockSpec((None, k, tn), lambda i, j: (e, 0, j)),
            pl.BlockSpec((None, k, tn), lambda i, j: (e, 1, j)),
            pl.BlockSpec((tm, tn), lambda i, j: (i, j)),
            pl.BlockSpec((1, N_MOD, tn), lambda i, j: (st.cond(i, tm), 0, j)),
        ],
        out_specs=pl.BlockSpec((tm, tn), lambda i, j: (i, j)),
        out_shape=jax.ShapeDtypeStruct((m, n), F32),
        compiler_params=_params("parallel", "arbitrary"),
        name="out_proj",
    )(a0, a1, w, w, res, mods)


def _ctx_attn_kernel(q_ref, k_ref, v_ref, o_ref, *, n_heads, scale):
    for h in range(n_heads):
        sl = slice(h * HEAD_DIM, (h + 1) * HEAD_DIM)
        q = q_ref[:, sl].astype(BF16)
        k = k_ref[:, sl].astype(BF16)
        v = v_ref[:, sl].astype(BF16)
        s = _dot_nt(q, k) * scale
        e = jnp.exp(s - jnp.max(s, axis=-1, keepdims=True))
        p = e * (1.0 / jnp.sum(e, axis=-1, keepdims=True))
        o_ref[:, sl] = _dot(p.astype(BF16), v).astype(o_ref.dtype)


def _ctx_attn(st, proj, width):
    l = st.l
    return pl.pallas_call(
        functools.partial(_ctx_attn_kernel, n_heads=width // HEAD_DIM, scale=HEAD_DIM ** -0.5),
        grid=(st.n_seq,),
        in_specs=[pl.BlockSpec((l, width), lambda b, part=part: (b, part)) for part in range(3)],
        out_specs=pl.BlockSpec((l, width), lambda b: (b, 0)),
        out_shape=jax.ShapeDtypeStruct((st.n, width), BF16),
        compiler_params=_params("parallel"),
        name="ctx_attn",
    )(proj, proj, proj)


def _na_window(r, rows, kh):
    start = min(max(r - kh // 2, 0), rows - kh)
    lo = start - start % 2
    hi = start + kh + (start + kh) % 2
    return start, lo, hi


NA_ROW_GROUP = 4


def _na_attn_kernel(q_ref, k_ref, v_ref, ck_ref, cv_ref, tab_ref, o_ref, *, rows, kh, scale):
    ck = ck_ref[...].astype(BF16)
    cv = cv_ref[...].astype(BF16)
    for g0 in range(0, rows, NA_ROW_GROUP):
        group = range(g0, min(g0 + NA_ROW_GROUP, rows))
        windows = [_na_window(r, rows, kh) for r in group]
        glo = min(w[1] for w in windows) * GRID_W
        ghi = max(w[2] for w in windows) * GRID_W
        q = q_ref[g0 * GRID_W:(g0 + len(group)) * GRID_W, :].astype(BF16)
        s_lat_g = _dot_nt(q, k_ref[glo:ghi, :].astype(BF16))
        s_ctx_g = _dot_nt(q, ck)
        p_lat_g, p_ctx_g = [], []
        for i, (r, (start, lo, hi)) in enumerate(zip(group, windows)):
            qs = slice(i * GRID_W, (i + 1) * GRID_W)
            pieces = []
            for p in range(lo // 2, hi // 2):
                ok0 = start <= 2 * p < start + kh
                ok1 = start <= 2 * p + 1 < start + kh
                dr = 2 * p - r + WIN_H - 1
                pieces.append(tab_ref[0, 0, dr] if ok0 and ok1 else tab_ref[0, 1, dr] if ok0
                              else tab_ref[0, 2, dr + 1])
            lo, hi = lo * GRID_W, hi * GRID_W
            s_lat = s_lat_g[qs, lo - glo:hi - glo] * scale + jnp.concatenate(pieces, axis=1)
            s_ctx = s_ctx_g[qs, :] * scale
            mx = jnp.maximum(jnp.max(s_lat, axis=-1, keepdims=True), jnp.max(s_ctx, axis=-1, keepdims=True))
            e_lat = jnp.exp(s_lat - mx)
            e_ctx = jnp.exp(s_ctx - mx)
            inv = 1.0 / (jnp.sum(e_lat, axis=-1, keepdims=True) + jnp.sum(e_ctx, axis=-1, keepdims=True))
            parts = [(e_lat * inv).astype(BF16)]
            if lo > glo:
                parts.insert(0, jnp.zeros((GRID_W, lo - glo), BF16))
            if hi < ghi:
                parts.append(jnp.zeros((GRID_W, ghi - hi), BF16))
            p_lat_g.append(jnp.concatenate(parts, axis=1) if len(parts) > 1 else parts[0])
            p_ctx_g.append((e_ctx * inv).astype(BF16))
        o = (_dot(jnp.concatenate(p_lat_g, axis=0), v_ref[glo:ghi, :].astype(BF16))
             + _dot(jnp.concatenate(p_ctx_g, axis=0), cv))
        o_ref[g0 * GRID_W:(g0 + len(group)) * GRID_W, :] = o.astype(o_ref.dtype)


def _na_bias_table(rpb):
    col = jnp.arange(GRID_W)
    col_start = jnp.clip(col - WIN_W // 2, 0, GRID_W - WIN_W)
    col_ok = (col[None, :] >= col_start[:, None]) & (col[None, :] < col_start[:, None] + WIN_W)
    dc_idx = jnp.clip(col[None, :] - col[:, None] + WIN_W - 1, 0, 2 * WIN_W - 2)
    t = jnp.where(col_ok, rpb.astype(F32)[:, :, dc_idx], NEG_INF)
    neg = jnp.full_like(t, NEG_INF)
    nxt = jnp.concatenate([t[:, 1:], neg[:, :1]], axis=1)
    return jnp.stack([jnp.concatenate([t, nxt], axis=-1), jnp.concatenate([t, neg], axis=-1),
                      jnp.concatenate([neg, t], axis=-1)], axis=1)


def _na_attn(st, proj, width, cache_k, cache_v, rpb):
    l = st.l
    n_heads = width // HEAD_DIM
    rows = l // GRID_W
    kh = min(WIN_H, rows)
    assert rows % 2 == 0 and kh % 2 == 0
    past = cache_k.shape[0] // st.n_seq
    tab = _na_bias_table(rpb)
    qkv_specs = [pl.BlockSpec((l, HEAD_DIM), lambda b, h, part=part: (b, part * n_heads + h)) for part in range(3)]
    return pl.pallas_call(
        functools.partial(_na_attn_kernel, rows=rows, kh=kh, scale=HEAD_DIM ** -0.5),
        grid=(st.n_seq, n_heads),
        in_specs=qkv_specs + [
            pl.BlockSpec((past, HEAD_DIM), lambda b, h: (b, h)),
            pl.BlockSpec((past, HEAD_DIM), lambda b, h: (b, h)),
            pl.BlockSpec((1,) + tab.shape[1:], lambda b, h: (h, 0, 0, 0, 0)),
        ],
        out_specs=pl.BlockSpec((l, HEAD_DIM), lambda b, h: (b, h)),
        out_shape=jax.ShapeDtypeStruct((st.n, width), BF16),
        compiler_params=_params("parallel", "arbitrary"),
        name="na_attn",
    )(proj, proj, proj, cache_k, cache_v, tab)


def _cos_sin(n_rows, n_cols, period):
    r = lax.broadcasted_iota(jnp.int32, (n_rows, n_cols), 0)
    c = lax.broadcasted_iota(jnp.int32, (n_rows, n_cols), 1)
    ang = ((r * c) % period).astype(F32) * (2.0 * math.pi / period)
    return jnp.cos(ang), jnp.sin(ang)


def _rfft_matrices(l):
    n = 2 * l
    cos, sin = _cos_sin(l, l, n)
    alt = jnp.where(lax.broadcasted_iota(jnp.int32, (l, l), 1) % 2 == 0, 1.0, -1.0).astype(F32)
    k_is0 = lax.broadcasted_iota(jnp.int32, (l, l), 0) == 0
    fwd_re = cos
    fwd_im = jnp.where(k_is0, alt, -sin)
    bin_is0 = lax.broadcasted_iota(jnp.int32, (l, l), 1) == 0
    inv_re = jnp.where(bin_is0, 1.0 / n, 2.0 / n) * cos
    inv_im = jnp.where(bin_is0, alt.T / n, (-2.0 / n) * sin)
    return fwd_re, fwd_im, inv_re, inv_im


def _filter_kernel(z_ref, t_ref, w1_ref, b1_ref, w2_ref, b2_ref, fq_ref, w3f_ref, w3b_ref, dl_ref,
                   fre_ref, fim_ref, kr_ref, ki_ref, kn_ref):
    fq = fq_ref[...]
    hid = jnp.sin(fq[0:1] * (_dot_hp(z_ref[...], w1_ref[...]) + b1_ref[...]))
    hid = jnp.sin(fq[1:2] * (_dot_hp(hid, w2_ref[...]) + b2_ref[...]))
    decay = jnp.exp(-t_ref[...] * dl_ref[...])
    row = lax.broadcasted_iota(jnp.int32, kr_ref.shape, 0)
    h_fwd = _dot_hp(hid, w3f_ref[...]) * decay
    h_bwd = jnp.where(row == 0, 0.0, _dot_hp(hid, w3b_ref[...]) * decay)
    nrm = lax.rsqrt(jnp.sum(h_fwd * h_fwd + h_bwd * h_bwd, axis=0, keepdims=True) + EPS)
    even = (h_fwd + h_bwd) * nrm
    odd = (h_fwd - h_bwd) * nrm
    k_re = _dot_hp(fre_ref[...], even)
    k_im = _dot_hp(fim_ref[...], odd)
    nyq = jnp.sum(jnp.where(row % 2 == 0, even, -even), axis=0, keepdims=True)
    kr_ref[...] = k_re
    ki_ref[...] = jnp.where(row == 0, 0.0, k_im)
    kn_ref[...] = jnp.where(row == 0, nyq, k_re)


def _pad_to(a, shape):
    return jnp.pad(a.astype(F32), [(0, s - n) for n, s in zip(a.shape, shape)])


def _hyena_filters(l, width, w1, b1, w2, b2, w3, freq, fwd_re, fwd_im):
    hid = w2.shape[0]
    hp = LANE * pl.cdiv(hid, LANE)
    n_bands = (FILTER_EMB - 1) // 2
    t = jnp.linspace(0.0, 1.0, l, dtype=F32)[:, None]
    w = (2.0 * math.pi / l) * jnp.arange(l, dtype=F32)[:, None]
    bands = jnp.linspace(1e-4, n_bands - 1, n_bands, dtype=F32)[None, :]
    z = _pad_to(jnp.concatenate([t, jnp.cos(bands * w), -jnp.sin(bands * w)], axis=-1), (l, LANE))
    deltas = jnp.abs(jnp.linspace(math.log(DECAY_TARGET) / FAST_DECAY_PCT,
                                  math.log(DECAY_TARGET) / SLOW_DECAY_PCT, width, dtype=F32))[None, :]
    tc = _tile(width, 512)
    nc = width // tc
    small = lambda shape: pl.BlockSpec(shape, lambda o, c: (0,) * len(shape))
    out = jax.ShapeDtypeStruct((HYENA_ORDER, l, width), F32)
    return pl.pallas_call(
        _filter_kernel,
        grid=(HYENA_ORDER, nc),
        in_specs=[small((l, LANE)), small((l, 1)), small((LANE, hp)), small((1, hp)), small((hp, hp)),
                  small((1, hp)), small((2, hp)),
                  pl.BlockSpec((hp, tc), lambda o, c: (0, 2 * o * nc + c)),
                  pl.BlockSpec((hp, tc), lambda o, c: (0, (2 * o + 1) * nc + c)),
                  pl.BlockSpec((1, tc), lambda o, c: (0, c)),
                  small((l, l)), small((l, l))],
        out_specs=[pl.BlockSpec((None, l, tc), lambda o, c: (o, 0, c))] * 3,
        out_shape=[out, out, out],
        compiler_params=_params("arbitrary", "arbitrary"),
        name="hyena_filters",
    )(z, t, _pad_to(w1, (LANE, hp)), _pad_to(b1[None, :], (1, hp)), _pad_to(w2, (hp, hp)),
      _pad_to(b2[None, :], (1, hp)), _pad_to(freq, (2, hp)), _pad_to(w3, (hp, w3.shape[1])),
      _pad_to(w3, (hp, w3.shape[1])), deltas, fwd_re, fwd_im)


HYENA_ROW_CHUNK = 256


def _hyena_kernel(pv_ref, p1_ref, p2_ref, wv_ref, w1_ref, w2_ref, bv_ref, b1_ref, b2_ref,
                  fre_ref, fim_ref, gre_ref, gim_ref, kr_ref, ki_ref, kn_ref, fbias_ref, o_ref, *, l):
    tc = o_ref.shape[1]
    row = lax.broadcasted_iota(jnp.int32, (l, tc), 0)

    def short_conv(p, w_ref, b_ref):
        prev = jnp.where(row == 0, 0.0, pltpu.roll(p, 1, 0))
        nxt = jnp.where(row == l - 1, 0.0, pltpu.roll(p, l - 1, 0))
        w = w_ref[...]
        out = b_ref[...] + prev * w[0:1]
        out = out + p * w[1:2]
        return out + nxt * w[2:3]

    chunks = [slice(r, r + HYENA_ROW_CHUNK) for r in range(0, l, HYENA_ROW_CHUNK)] if l > HYENA_ROW_CHUNK \
        else [slice(0, l)]

    def gated_long_conv(x, u, o, bias):
        ub = u.astype(BF16)
        w_re, w_im = [], []
        for rs in chunks:
            u_re = _dot(fre_ref[rs, :], ub)
            u_im = _dot(fim_ref[rs, :], ub)
            kr, ki, kn = kr_ref[o, rs, :], ki_ref[o, rs, :], kn_ref[o, rs, :]
            w_re.append((u_re * kr - u_im * ki).astype(BF16))
            w_im.append((u_re * ki + u_im * kn).astype(BF16))
        w_re = jnp.concatenate(w_re, axis=0)
        w_im = jnp.concatenate(w_im, axis=0)
        out = []
        for rs in chunks:
            y = _dot(gre_ref[rs, :], w_re) + _dot(gim_ref[rs, :], w_im)
            out.append(x[rs] * (y + u[rs] * bias))
        return jnp.concatenate(out, axis=0)

    fbias = fbias_ref[...]
    for s in range(o_ref.shape[0] // l):
        rs = slice(s * l, (s + 1) * l)
        v = short_conv(pv_ref[rs, :], wv_ref, bv_ref)
        z = gated_long_conv(short_conv(p1_ref[rs, :], w1_ref, b1_ref), v, 0, fbias[0:1])
        z = gated_long_conv(short_conv(p2_ref[rs, :], w2_ref, b2_ref), z, 1, fbias[1:2])
        o_ref[rs, :] = z.astype(o_ref.dtype)


def _hyena(st, proj, col0, width, e, w_short, b_short, mats, filters, fbias):
    l = st.l
    tc = _tile(width, 1024 if l <= 256 else 256)
    per_step = 2 if l > 256 and st.n_seq % 2 == 0 else 1
    nc = width // tc
    c0 = col0 // tc
    rows = per_step * l
    once = pl.Buffered(1)
    p_specs = [pl.BlockSpec((rows, tc), lambda c, b, part=part: (b, c0 + part * nc + c)) for part in range(3)]
    w_specs = [pl.BlockSpec((None, SHORT_CONV, tc), lambda c, b, part=part: (e, 0, part * nc + c))
               for part in range(3)]
    b_specs = [pl.BlockSpec((None, 1, tc), lambda c, b, part=part: (e, 0, part * nc + c)) for part in range(3)]
    mat_specs = [pl.BlockSpec((l, l), lambda c, b: (0, 0), pipeline_mode=once)] * 4
    k_specs = [pl.BlockSpec((HYENA_ORDER, l, tc), lambda c, b: (0, 0, c), pipeline_mode=once)] * 3
    return pl.pallas_call(
        functools.partial(_hyena_kernel, l=l),
        grid=(nc, st.n_seq // per_step),
        in_specs=p_specs + w_specs + b_specs + mat_specs + k_specs
        + [pl.BlockSpec((None, HYENA_ORDER, tc), lambda c, b: (e, 0, c))],
        out_specs=pl.BlockSpec((rows, tc), lambda c, b: (b, c)),
        out_shape=jax.ShapeDtypeStruct((st.n, width), BF16),
        compiler_params=_params("parallel", "arbitrary"),
        name="hyena",
    )(proj, proj, proj, w_short, w_short, w_short, *([b_short.reshape(b_short.shape[0], 1, -1)] * 3),
      *[a.astype(BF16) for a in mats], *filters, fbias)


def _seq_dft_kernel(cos_ref, sin_ref, h_ref, a_ref, b_ref):
    h = h_ref[...]
    a_ref[...] = _dot(cos_ref[...], h).astype(a_ref.dtype)
    b_ref[...] = _dot(sin_ref[...], h).astype(b_ref.dtype)


def _seq_dft(st, h):
    l = st.l
    d = h.shape[1]
    tn = _tile(d, 2048 if l <= 256 else 512)
    cos, sin = _cos_sin(l, l, l)
    cos = (cos * l ** -0.5).astype(BF16)
    sin = (sin * l ** -0.5).astype(BF16)
    out = jax.ShapeDtypeStruct((st.n, d), BF16)
    return pl.pallas_call(
        _seq_dft_kernel,
        grid=(st.n_seq, d // tn),
        in_specs=[pl.BlockSpec((l, l), lambda b, j: (0, 0))] * 2 + [pl.BlockSpec((l, tn), lambda b, j: (b, j))],
        out_specs=[pl.BlockSpec((l, tn), lambda b, j: (b, j))] * 2,
        out_shape=[out, out],
        compiler_params=_params("parallel", "arbitrary"),
        name="seq_dft",
    )(cos, sin, h)


def _fnet_out_kernel(a_ref, b_ref, cos_ref, sin_ref, w_ref, res_ref, mod_ref, o_ref, mix_ref, *, gi, gw):
    @pl.when(pl.program_id(1) == 0)
    def _():
        cos = cos_ref[...]
        sin = sin_ref[...]
        for g in range(a_ref.shape[1] // gw):
            sl = slice(g * gw, (g + 1) * gw)
            mix_ref[:, sl] = (_dot(a_ref[:, sl], cos) - _dot(b_ref[:, sl], sin)).astype(mix_ref.dtype)

    o_ref[...] = res_ref[...] + mod_ref[0][gi:gi + 1] * _dot(mix_ref[...], w_ref[...])


def _fnet_out(st, a, b, w, o, res, mods, gi):
    m, d = a.shape
    gw = d // FNET_GROUPS
    tm = st.row_tile(512)
    tn = _tile(d, 1024)
    cos, sin = _cos_sin(gw, gw, gw)
    cos = (cos * gw ** -0.5).astype(BF16)
    sin = (sin * gw ** -0.5).astype(BF16)
    return pl.pallas_call(
        functools.partial(_fnet_out_kernel, gi=gi, gw=gw),
        grid=(m // tm, d // tn),
        in_specs=[
            pl.BlockSpec((tm, d), lambda i, j: (i, 0)),
            pl.BlockSpec((tm, d), lambda i, j: (i, 0)),
            pl.BlockSpec((gw, gw), lambda i, j: (0, 0)),
            pl.BlockSpec((gw, gw), lambda i, j: (0, 0)),
            pl.BlockSpec((None, d, tn), lambda i, j: (o, 0, j)),
            pl.BlockSpec((tm, tn), lambda i, j: (i, j)),
            pl.BlockSpec((1, N_MOD, tn), lambda i, j: (st.cond(i, tm), 0, j)),
        ],
        out_specs=pl.BlockSpec((tm, tn), lambda i, j: (i, j)),
        out_shape=jax.ShapeDtypeStruct((m, d), F32),
        scratch_shapes=[pltpu.VMEM((tm, d), BF16)],
        compiler_params=_params("parallel", "arbitrary"),
        name="fnet_out",
    )(a, b, cos, sin, w, res, mods)


def _rms_kernel(x_ref, g_ref, o_ref):
    rows = BF16_SUBLANES
    g = g_ref[...]

    def body(c, carry):
        r = pl.multiple_of(c * rows, rows)
        x = x_ref[pl.ds(r, rows), :]
        o_ref[pl.ds(r, rows), :] = x * lax.rsqrt(jnp.mean(x * x, axis=-1, keepdims=True) + EPS) * g
        return carry

    lax.fori_loop(0, x_ref.shape[0] // rows, body, 0)


def _rms_norm(x, g):
    m, d = x.shape
    tm = _tile(m, 256, BF16_SUBLANES)
    return pl.pallas_call(
        _rms_kernel,
        grid=(m // tm,),
        in_specs=[pl.BlockSpec((tm, d), lambda i: (i, 0)), pl.BlockSpec((1, d), lambda i: (0, 0))],
        out_specs=pl.BlockSpec((tm, d), lambda i: (i, 0)),
        out_shape=jax.ShapeDtypeStruct((m, d), F32),
        compiler_params=_params("parallel"),
        name="final_norm",
    )(x, g.reshape(1, d))


def kernel(x_prompt, x_sample, cache_k, cache_v, c, c_ctx, w_mod, b_mod, norm_g, ffn_w1, ffn_w3, ffn_w2,
           w_in, w_out, rpb, w_short, b_short, filt_w1, filt_b1, filt_w2, filt_b2, filt_w3, filt_freq,
           filt_bias, w_fnet, final_g):
    b_p, l_p, d = x_prompt.shape
    b_s, l_s, _ = x_sample.shape
    depth = w_mod.shape[0]
    attn_w = cache_k.shape[-2] * cache_k.shape[-1]
    hyena_w = filt_bias.shape[-1]
    n_heads = attn_w // HEAD_DIM
    past = cache_k.shape[2]
    assert 1 + b_s <= COND_ROWS

    streams = (_Stream(b_p, l_p, 0, True), _Stream(b_s, l_s, 1, False))
    xs = [x_prompt.reshape(b_p * l_p, d), x_sample.reshape(b_s * l_s, d)]
    conds = jnp.concatenate([c_ctx[None, :], c, jnp.zeros((COND_ROWS - 1 - b_s, d), F32)], axis=0)
    w_in_b, w_out_b, w_fnet_b = w_in.astype(BF16), w_out.astype(BF16), w_fnet.astype(BF16)
    rfft_mats = {st.l: _rfft_matrices(st.l) for st in streams}

    ride_along = _ffn_cast_ok(streams[0], d, ffn_w1.shape[-1])
    ffn_w = {(0, 0): (ffn_w1[0, 0].astype(BF16), ffn_w3[0, 0].astype(BF16), ffn_w2[0, 0].astype(BF16))}

    def ffn_half_step(xs, mods, l, s):
        nxt = (l, 1) if s == 0 else (l + 1, 0)
        w = ffn_w[(l, s)]
        g, mi = norm_g[l, 2 * s], 6 * s
        if nxt[0] == depth:
            return [_ffn(st, x, mods, g, *w, mi) for st, x in zip(streams, xs)]
        if ride_along:
            x0, ffn_w[nxt] = _ffn(streams[0], xs[0], mods, g, *w, mi, cast_next=(ffn_w1, ffn_w3, ffn_w2, *nxt))
        else:
            x0 = _ffn(streams[0], xs[0], mods, g, *w, mi)
            ffn_w[nxt] = tuple(a[nxt].astype(BF16) for a in (ffn_w1, ffn_w3, ffn_w2))
        return [x0, _ffn(streams[1], xs[1], mods, g, *w, mi)]

    new_k, new_v = [], []
    for l in range(depth):
        mods = _adaln(conds, w_mod, b_mod, l)
        xs = ffn_half_step(xs, mods, l, 0)
        if l % 2 == 0:
            e = l // 2
            projs = [_norm_mm(st, x, mods, norm_g[l, 1], w_in_b, e, 3) for st, x in zip(streams, xs)]
            attn = [_ctx_attn(streams[0], projs[0], attn_w),
                    _na_attn(streams[1], projs[1], attn_w, cache_k[:, e].reshape(b_s * past, attn_w),
                             cache_v[:, e].reshape(b_s * past, attn_w), rpb[e])]
            for i, st in enumerate(streams):
                mats = rfft_mats[st.l]
                filters = _hyena_filters(st.l, hyena_w, filt_w1[e], filt_b1[e], filt_w2[e], filt_b2[e],
                                         filt_w3[e], filt_freq[e], mats[0], mats[1])
                hy = _hyena(st, projs[i], 3 * attn_w, hyena_w, e, w_short, b_short, mats, filters, filt_bias)
                xs[i] = _mm_res(st, attn[i], hy, w_out_b, e, xs[i], mods, 5)
            new_k.append(projs[0][:, attn_w:2 * attn_w].reshape(b_p, l_p, n_heads, HEAD_DIM))
            new_v.append(projs[0][:, 2 * attn_w:3 * attn_w].reshape(b_p, l_p, n_heads, HEAD_DIM))
        else:
            o = l // 2
            for i, st in enumerate(streams):
                h = _mod_norm(st, xs[i], mods, norm_g[l, 1], 3)
                a, b = _seq_dft(st, h)
                xs[i] = _fnet_out(st, a, b, w_fnet_b, o, xs[i], mods, 5)
        xs = ffn_half_step(xs, mods, l, 1)

    y_prompt = _rms_norm(xs[0], final_g).reshape(b_p, l_p, d)
    y_sample = _rms_norm(xs[1], final_g).reshape(b_s, l_s, d)
    return y_prompt, y_sample, jnp.stack(new_k, axis=1), jnp.stack(new_v, axis=1)
```

```python
import dataclasses
import functools
import math

import jax
import jax.numpy as jnp
from jax import lax
from jax.experimental import pallas as pl
from jax.experimental.pallas import tpu as pltpu

F32 = jnp.float32
BF16 = jnp.bfloat16
HIGHEST = lax.Precision.HIGHEST

N_MOD = 9
GRID_W = 64
WIN_H = 8
WIN_W = 16
HEAD_DIM = 128
SHORT_CONV = 3
FILTER_EMB = 33
HYENA_ORDER = 2
DECAY_TARGET = 1e-2
FAST_DECAY_PCT = 0.3
SLOW_DECAY_PCT = 1.5
FNET_GROUPS = 8
EPS = 1e-6
NEG_INF = -1e30

V7X_VMEM_BYTES = 64 * 2**20
VMEM_LIMIT = V7X_VMEM_BYTES - 8 * 2**20
LANE = 128
BF16_SUBLANES = 16
COND_ROWS = 16


def _tile(n, pref, unit=LANE):
    if n <= pref:
        return n
    t = (pref // unit) * unit
    while t > unit and n % t:
        t -= unit
    assert n % t == 0, (n, pref)
    return t


def _params(*sem):
    return pltpu.CompilerParams(dimension_semantics=sem, vmem_limit_bytes=VMEM_LIMIT)


@dataclasses.dataclass(frozen=True)
class _Stream:
    n_seq: int
    l: int
    cond0: int
    shared: bool

    @property
    def n(self):
        return self.n_seq * self.l

    def row_tile(self, pref):
        return _tile(self.n if self.shared else self.l, pref, BF16_SUBLANES)

    def cond(self, i, tm):
        return self.cond0 if self.shared else self.cond0 + i // (self.l // tm)


def _fill_mod_norm(x_ref, g_ref, mod_ref, h_ref, rs_ref, mi):
    rows = BF16_SUBLANES
    tm, d = x_ref.shape
    cw = _tile(d, 512)

    def stats(c, carry):
        r = pl.multiple_of(c * rows, rows)
        x = x_ref[pl.ds(r, rows), :]
        rs = lax.rsqrt(jnp.mean(x * x, axis=-1, keepdims=True) + EPS)
        rs_ref[pl.ds(r, rows), :] = jnp.broadcast_to(rs, (rows, LANE))
        return carry

    lax.fori_loop(0, tm // rows, stats, 0, unroll=4)

    m = mod_ref[0]
    for t in range(d // cw):
        cs = slice(t * cw, (t + 1) * cw)
        gain = g_ref[:, cs] * (1.0 + m[mi + 1:mi + 2, cs])
        shift = m[mi:mi + 1, cs]

        def apply(c, carry):
            r = pl.multiple_of(c * rows, rows)
            rs = rs_ref[pl.ds(r, rows), :]
            rs = jnp.concatenate([rs] * (cw // LANE), axis=1) if cw > LANE else rs[:, :cw]
            h_ref[pl.ds(r, rows), cs] = (x_ref[pl.ds(r, rows), cs] * rs * gain + shift).astype(h_ref.dtype)
            return carry

        lax.fori_loop(0, tm // rows, apply, 0)


def _dot(a, b):
    return jnp.dot(a, b, preferred_element_type=F32)


def _dot_hp(a, b):
    return jnp.dot(a, b, preferred_element_type=F32, precision=HIGHEST)


def _dot_nt(a, b):
    return lax.dot_general(a, b, (((1,), (1,)), ((), ())), preferred_element_type=F32)


def _adaln_kernel(c_ref, w_ref, b_ref, o_ref):
    c = c_ref[...]
    s = c * (1.0 / (1.0 + jnp.exp(-c)))
    s_hi = s.astype(BF16)
    s_lo = (s - s_hi.astype(F32)).astype(BF16)
    w = w_ref[...].astype(BF16)
    o_ref[...] = _dot(s_hi, w) + _dot(s_lo, w) + b_ref[...]


def _adaln(conds, w_mod, b_mod, l):
    d = conds.shape[1]
    n = w_mod.shape[-1]
    tn = _tile(n, 1024)
    out = pl.pallas_call(
        _adaln_kernel,
        grid=(n // tn,),
        in_specs=[
            pl.BlockSpec((COND_ROWS, d), lambda j: (0, 0)),
            pl.BlockSpec((None, d, tn), lambda j: (l, 0, j)),
            pl.BlockSpec((None, 1, tn), lambda j: (l, 0, j)),
        ],
        out_specs=pl.BlockSpec((COND_ROWS, tn), lambda j: (0, j)),
        out_shape=jax.ShapeDtypeStruct((COND_ROWS, n), F32),
        compiler_params=_params("arbitrary"),
        name="adaln",
    )(conds, w_mod, b_mod.reshape(b_mod.shape[0], 1, n))
    return out.reshape(COND_ROWS, N_MOD, d)


def _ffn_kernel(x_ref, mod_ref, g_ref, w1_ref, w3_ref, w2_ref, *rest, mi):
    j = pl.program_id(1)
    f32_tiles, bf16_tiles = (), ()
    if len(rest) > 3:
        f32_tiles, (o_ref, *bf16_tiles), (h_ref, rs_ref) = rest[:3], rest[3:7], rest[7:]
    else:
        o_ref, h_ref, rs_ref = rest

    @pl.when(j == 0)
    def _():
        _fill_mod_norm(x_ref, g_ref, mod_ref, h_ref, rs_ref, mi)
        o_ref[...] = jnp.zeros_like(o_ref)

    for src, dst in zip(f32_tiles, bf16_tiles):
        dst[...] = src[...].astype(dst.dtype)

    h = h_ref[...]
    a = _dot(h, w1_ref[...])
    b = _dot(h, w3_ref[...])
    act = (a * (1.0 / (1.0 + jnp.exp(-a))) * b).astype(BF16)
    o_ref[...] += _dot(act, w2_ref[...])

    @pl.when(j == pl.num_programs(1) - 1)
    def _():
        rows = BF16_SUBLANES
        half_gate = 0.5 * mod_ref[0][mi + 2:mi + 3]

        def body(c, carry):
            r = pl.multiple_of(c * rows, rows)
            o_ref[pl.ds(r, rows), :] = x_ref[pl.ds(r, rows), :] + half_gate * o_ref[pl.ds(r, rows), :]
            return carry

        lax.fori_loop(0, o_ref.shape[0] // rows, body, 0)


FFN_TF = 256


def _ffn_cast_ok(st, d, f):
    tm = st.row_tile(512)
    nm = st.n // tm
    return f % FFN_TF == 0 and d % nm == 0 and (d // nm) % LANE == 0


def _ffn(st, x, mods, g, w1, w3, w2, mi, cast_next=None):
    m, d = x.shape
    f = w1.shape[-1]
    tm = st.row_tile(512)
    tf = _tile(f, FFN_TF)
    nm, nf = m // tm, f // tf
    in_specs = [
        pl.BlockSpec((tm, d), lambda i, j: (i, 0)),
        pl.BlockSpec((1, N_MOD, d), lambda i, j: (st.cond(i, tm), 0, 0)),
        pl.BlockSpec((1, d), lambda i, j: (0, 0)),
        pl.BlockSpec((d, tf), lambda i, j: (0, j)),
        pl.BlockSpec((d, tf), lambda i, j: (0, j)),
        pl.BlockSpec((tf, d), lambda i, j: (j, 0)),
    ]
    out_specs = [pl.BlockSpec((tm, d), lambda i, j: (i, 0))]
    out_shape = [jax.ShapeDtypeStruct((m, d), F32)]
    args = [x, mods, g.reshape(1, d), w1, w3, w2]
    if cast_next is not None:
        n1, n3, n2, ln, sn = cast_next
        dr = d // nm
        in_specs += [pl.BlockSpec((None, None, dr, tf), lambda i, j: (ln, sn, i, j)),
                     pl.BlockSpec((None, None, dr, tf), lambda i, j: (ln, sn, i, j)),
                     pl.BlockSpec((None, None, tf, dr), lambda i, j: (ln, sn, j, i))]
        out_specs += [pl.BlockSpec((dr, tf), lambda i, j: (i, j)), pl.BlockSpec((dr, tf), lambda i, j: (i, j)),
                      pl.BlockSpec((tf, dr), lambda i, j: (j, i))]
        out_shape += [jax.ShapeDtypeStruct((d, f), BF16), jax.ShapeDtypeStruct((d, f), BF16),
                      jax.ShapeDtypeStruct((f, d), BF16)]
        args += [n1, n3, n2]
    out = pl.pallas_call(
        functools.partial(_ffn_kernel, mi=mi),
        grid=(nm, nf),
        in_specs=in_specs,
        out_specs=out_specs,
        out_shape=out_shape,
        scratch_shapes=[pltpu.VMEM((tm, d), BF16), pltpu.VMEM((tm, LANE), F32)],
        compiler_params=_params("parallel", "arbitrary"),
        name="ffn",
    )(*args)
    return out[0] if cast_next is None else (out[0], tuple(out[1:]))


def _norm_mm_kernel(x_ref, mod_ref, g_ref, w_ref, o_ref, h_ref, rs_ref, *, mi):
    @pl.when(pl.program_id(1) == 0)
    def _():
        _fill_mod_norm(x_ref, g_ref, mod_ref, h_ref, rs_ref, mi)

    o_ref[...] = _dot(h_ref[...], w_ref[...])


def _norm_mm(st, x, mods, g, w, e, mi):
    m, d = x.shape
    n = w.shape[-1]
    tm = st.row_tile(1024)
    tn = _tile(n, 512)
    return pl.pallas_call(
        functools.partial(_norm_mm_kernel, mi=mi),
        grid=(m // tm, n // tn),
        in_specs=[
            pl.BlockSpec((tm, d), lambda i, j: (i, 0)),
            pl.BlockSpec((1, N_MOD, d), lambda i, j: (st.cond(i, tm), 0, 0)),
            pl.BlockSpec((1, d), lambda i, j: (0, 0)),
            pl.BlockSpec((None, d, tn), lambda i, j: (e, 0, j)),
        ],
        out_specs=pl.BlockSpec((tm, tn), lambda i, j: (i, j)),
        out_shape=jax.ShapeDtypeStruct((m, n), F32),
        scratch_shapes=[pltpu.VMEM((tm, d), BF16), pltpu.VMEM((tm, LANE), F32)],
        compiler_params=_params("parallel", "arbitrary"),
        name="in_proj",
    )(x, mods, g.reshape(1, d), w)


def _mod_norm_kernel(x_ref, mod_ref, g_ref, h_ref, rs_ref, *, mi):
    _fill_mod_norm(x_ref, g_ref, mod_ref, h_ref, rs_ref, mi)


def _mod_norm(st, x, mods, g, mi):
    m, d = x.shape
    tm = st.row_tile(256)
    return pl.pallas_call(
        functools.partial(_mod_norm_kernel, mi=mi),
        grid=(m // tm,),
        in_specs=[
            pl.BlockSpec((tm, d), lambda i: (i, 0)),
            pl.BlockSpec((1, N_MOD, d), lambda i: (st.cond(i, tm), 0, 0)),
            pl.BlockSpec((1, d), lambda i: (0, 0)),
        ],
        out_specs=pl.BlockSpec((tm, d), lambda i: (i, 0)),
        out_shape=jax.ShapeDtypeStruct((m, d), BF16),
        scratch_shapes=[pltpu.VMEM((tm, LANE), F32)],
        compiler_params=_params("parallel"),
        name="mod_norm",
    )(x, mods, g.reshape(1, d))


def _mm_res_kernel(a0_ref, a1_ref, w0_ref, w1_ref, res_ref, mod_ref, o_ref, *, gi):
    mix = _dot(a0_ref[...], w0_ref[...]) + _dot(a1_ref[...], w1_ref[...])
    o_ref[...] = res_ref[...] + mod_ref[0][gi:gi + 1] * mix


def _mm_res(st, a0, a1, w, e, res, mods, gi, name):
    k = w.shape[1] // 2
    m = a0.shape[0]
    c1 = 0
    if a1 is None:
        a1, c1 = a0, 1
    assert a0.shape[1] >= k and a1.shape == (m, (c1 + 1) * k)
    n = w.shape[-1]
    tm = st.row_tile(1024)
    tn = _tile(n, 1024)
    return pl.pallas_call(
        functools.partial(_mm_res_kernel, gi=gi),
        grid=(m // tm, n // tn),
        in_specs=[
            pl.BlockSpec((tm, k), lambda i, j: (i, 0)),
            pl.BlockSpec((tm, k), lambda i, j: (i, c1)),
            pl.BlockSpec((None, k, tn), lambda i, j: (e, 0, j)),
            pl.BlockSpec((None, k, tn), lambda i, j: (e, 1, j)),
            pl.BlockSpec((tm, tn), lambda i, j: (i, j)),
            pl.BlockSpec((1, N_MOD, tn), lambda i, j: (st.cond(i, tm), 0, j)),
        ],
        out_specs=pl.BlockSpec((tm, tn), lambda i, j: (i, j)),
        out_shape=jax.ShapeDtypeStruct((m, n), F32),
        compiler_params=_params("parallel", "arbitrary"),
        name=name,
    )(a0, a1, w, w, res, mods)


def _ctx_attn_kernel(q_ref, k_ref, v_ref, o_ref, *, n_heads, scale):
    for h in range(n_heads):
        sl = slice(h * HEAD_DIM, (h + 1) * HEAD_DIM)
        q = q_ref[:, sl].astype(BF16)
        k = k_ref[:, sl].astype(BF16)
        v = v_ref[:, sl].astype(BF16)
        s = _dot_nt(q, k) * scale
        e = jnp.exp(s - jnp.max(s, axis=-1, keepdims=True))
        p = e * (1.0 / jnp.sum(e, axis=-1, keepdims=True))
        o_ref[:, sl] = _dot(p.astype(BF16), v).astype(o_ref.dtype)


def _ctx_attn(st, proj, width):
    l = st.l
    return pl.pallas_call(
        functools.partial(_ctx_attn_kernel, n_heads=width // HEAD_DIM, scale=HEAD_DIM ** -0.5),
        grid=(st.n_seq,),
        in_specs=[pl.BlockSpec((l, width), lambda b, part=part: (b, part)) for part in range(3)],
        out_specs=pl.BlockSpec((l, width), lambda b: (b, 0)),
        out_shape=jax.ShapeDtypeStruct((st.n, width), BF16),
        compiler_params=_params("parallel"),
        name="ctx_attn",
    )(proj, proj, proj)


def _na_window(r, rows, kh):
    start = min(max(r - kh // 2, 0), rows - kh)
    lo = start - start % 2
    hi = start + kh + (start + kh) % 2
    return start, lo, hi


NA_ROW_GROUP = 4


def _na_attn_kernel(q_ref, k_ref, v_ref, ck_ref, cv_ref, tab_ref, o_ref, *, rows, kh, scale):
    ck = ck_ref[...].astype(BF16)
    cv = cv_ref[...].astype(BF16)
    for g0 in range(0, rows, NA_ROW_GROUP):
        group = range(g0, min(g0 + NA_ROW_GROUP, rows))
        windows = [_na_window(r, rows, kh) for r in group]
        glo = min(w[1] for w in windows) * GRID_W
        ghi = max(w[2] for w in windows) * GRID_W
        q = q_ref[g0 * GRID_W:(g0 + len(group)) * GRID_W, :].astype(BF16)
        s_lat_g = _dot_nt(q, k_ref[glo:ghi, :].astype(BF16))
        s_ctx_g = _dot_nt(q, ck)
        p_lat_g, p_ctx_g = [], []
        for i, (r, (start, lo, hi)) in enumerate(zip(group, windows)):
            qs = slice(i * GRID_W, (i + 1) * GRID_W)
            pieces = []
            for p in range(lo // 2, hi // 2):
                ok0 = start <= 2 * p < start + kh
                ok1 = start <= 2 * p + 1 < start + kh
                dr = 2 * p - r + WIN_H - 1
                pieces.append(tab_ref[0, 0, dr] if ok0 and ok1 else tab_ref[0, 1, dr] if ok0
                              else tab_ref[0, 2, dr + 1])
            lo, hi = lo * GRID_W, hi * GRID_W
            s_lat = s_lat_g[qs, lo - glo:hi - glo] * scale + jnp.concatenate(pieces, axis=1)
            s_ctx = s_ctx_g[qs, :] * scale
            mx = jnp.maximum(jnp.max(s_lat, axis=-1, keepdims=True), jnp.max(s_ctx, axis=-1, keepdims=True))
            e_lat = jnp.exp(s_lat - mx)
            e_ctx = jnp.exp(s_ctx - mx)
            inv = 1.0 / (jnp.sum(e_lat, axis=-1, keepdims=True) + jnp.sum(e_ctx, axis=-1, keepdims=True))
            parts = [(e_lat * inv).astype(BF16)]
            if lo > glo:
                parts.insert(0, jnp.zeros((GRID_W, lo - glo), BF16))
            if hi < ghi:
                parts.append(jnp.zeros((GRID_W, ghi - hi), BF16))
            p_lat_g.append(jnp.concatenate(parts, axis=1) if len(parts) > 1 else parts[0])
            p_ctx_g.append((e_ctx * inv).astype(BF16))
        o = (_dot(jnp.concatenate(p_lat_g, axis=0), v_ref[glo:ghi, :].astype(BF16))
             + _dot(jnp.concatenate(p_ctx_g, axis=0), cv))
        o_ref[g0 * GRID_W:(g0 + len(group)) * GRID_W, :] = o.astype(o_ref.dtype)


def _na_bias_table(rpb):
    col = jnp.arange(GRID_W)
    col_start = jnp.clip(col - WIN_W // 2, 0, GRID_W - WIN_W)
    col_ok = (col[None, :] >= col_start[:, None]) & (col[None, :] < col_start[:, None] + WIN_W)
    dc_idx = jnp.clip(col[None, :] - col[:, None] + WIN_W - 1, 0, 2 * WIN_W - 2)
    t = jnp.where(col_ok, rpb.astype(F32)[:, :, dc_idx], NEG_INF)
    neg = jnp.full_like(t, NEG_INF)
    nxt = jnp.concatenate([t[:, 1:], neg[:, :1]], axis=1)
    return jnp.stack([jnp.concatenate([t, nxt], axis=-1), jnp.concatenate([t, neg], axis=-1),
                      jnp.concatenate([neg, t], axis=-1)], axis=1)


def _na_attn(st, proj, width, cache_k, cache_v, rpb):
    l = st.l
    n_heads = width // HEAD_DIM
    rows = l // GRID_W
    kh = min(WIN_H, rows)
    assert rows % 2 == 0 and kh % 2 == 0
    past = cache_k.shape[0] // st.n_seq
    tab = _na_bias_table(rpb)
    qkv_specs = [pl.BlockSpec((l, HEAD_DIM), lambda b, h, part=part: (b, part * n_heads + h)) for part in range(3)]
    return pl.pallas_call(
        functools.partial(_na_attn_kernel, rows=rows, kh=kh, scale=HEAD_DIM ** -0.5),
        grid=(st.n_seq, n_heads),
        in_specs=qkv_specs + [
            pl.BlockSpec((past, HEAD_DIM), lambda b, h: (b, h)),
            pl.BlockSpec((past, HEAD_DIM), lambda b, h: (b, h)),
            pl.BlockSpec((1,) + tab.shape[1:], lambda b, h: (h, 0, 0, 0, 0)),
        ],
        out_specs=pl.BlockSpec((l, HEAD_DIM), lambda b, h: (b, h)),
        out_shape=jax.ShapeDtypeStruct((st.n, width), BF16),
        compiler_params=_params("parallel", "arbitrary"),
        name="na_attn",
    )(proj, proj, proj, cache_k, cache_v, tab)


def _cos_sin(n_rows, n_cols, period):
    r = lax.broadcasted_iota(jnp.int32, (n_rows, n_cols), 0)
    c = lax.broadcasted_iota(jnp.int32, (n_rows, n_cols), 1)
    ang = ((r * c) % period).astype(F32) * (2.0 * math.pi / period)
    return jnp.cos(ang), jnp.sin(ang)


def _rfft_matrices(l):
    n = 2 * l
    cos, sin = _cos_sin(l, l, n)
    alt = jnp.where(lax.broadcasted_iota(jnp.int32, (l, l), 1) % 2 == 0, 1.0, -1.0).astype(F32)
    k_is0 = lax.broadcasted_iota(jnp.int32, (l, l), 0) == 0
    fwd_re = cos
    fwd_im = jnp.where(k_is0, alt, -sin)
    bin_is0 = lax.broadcasted_iota(jnp.int32, (l, l), 1) == 0
    inv_re = jnp.where(bin_is0, 1.0 / n, 2.0 / n) * cos
    inv_im = jnp.where(bin_is0, alt.T / n, (-2.0 / n) * sin)
    return fwd_re, fwd_im, inv_re, inv_im


def _split_bf16(a):
    hi = a.astype(BF16)
    return hi, (a - hi.astype(F32)).astype(BF16)


def _dot_3pass(a_hi_ref, a_lo_ref, b):
    b_hi, b_lo = _split_bf16(b)
    a_hi = a_hi_ref[...]
    return _dot(a_hi, b_hi) + (_dot(a_hi, b_lo) + _dot(a_lo_ref[...], b_hi))


def _filter_kernel(z_ref, t_ref, w1_ref, b1_ref, w2_ref, b2_ref, fq_ref, w3f_ref, w3b_ref, dl_ref,
                   fre_hi_ref, fre_lo_ref, fim_hi_ref, fim_lo_ref, kr_ref, ki_ref, kn_ref):
    fq = fq_ref[...]
    hid = jnp.sin(fq[0:1] * (_dot_hp(z_ref[...], w1_ref[...]) + b1_ref[...]))
    hid = jnp.sin(fq[1:2] * (_dot_hp(hid, w2_ref[...]) + b2_ref[...]))
    decay = jnp.exp(-t_ref[...] * dl_ref[...])
    row = lax.broadcasted_iota(jnp.int32, kr_ref.shape, 0)
    h_fwd = _dot_hp(hid, w3f_ref[...]) * decay
    h_bwd = jnp.where(row == 0, 0.0, _dot_hp(hid, w3b_ref[...]) * decay)
    nrm = lax.rsqrt(jnp.sum(h_fwd * h_fwd + h_bwd * h_bwd, axis=0, keepdims=True) + EPS)
    even = (h_fwd + h_bwd) * nrm
    odd = (h_fwd - h_bwd) * nrm
    k_re = _dot_3pass(fre_hi_ref, fre_lo_ref, even)
    k_im = _dot_3pass(fim_hi_ref, fim_lo_ref, odd)
    nyq = jnp.sum(jnp.where(row % 2 == 0, even, -even), axis=0, keepdims=True)
    kr_ref[...] = k_re
    ki_ref[...] = jnp.where(row == 0, 0.0, k_im)
    kn_ref[...] = jnp.where(row == 0, nyq, k_re)


def _pad_to(a, shape):
    return jnp.pad(a.astype(F32), [(0, s - n) for n, s in zip(a.shape, shape)])


def _hyena_filters(l, width, w1, b1, w2, b2, w3, freq, fwd_re, fwd_im):
    hid = w2.shape[0]
    hp = LANE * pl.cdiv(hid, LANE)
    n_bands = (FILTER_EMB - 1) // 2
    t = jnp.linspace(0.0, 1.0, l, dtype=F32)[:, None]
    w = (2.0 * math.pi / l) * jnp.arange(l, dtype=F32)[:, None]
    bands = jnp.linspace(1e-4, n_bands - 1, n_bands, dtype=F32)[None, :]
    z = _pad_to(jnp.concatenate([t, jnp.cos(bands * w), -jnp.sin(bands * w)], axis=-1), (l, LANE))
    deltas = jnp.abs(jnp.linspace(math.log(DECAY_TARGET) / FAST_DECAY_PCT,
                                  math.log(DECAY_TARGET) / SLOW_DECAY_PCT, width, dtype=F32))[None, :]
    tc = _tile(width, 512)
    nc = width // tc
    small = lambda shape: pl.BlockSpec(shape, lambda o, c: (0,) * len(shape))
    out = jax.ShapeDtypeStruct((HYENA_ORDER, l, width), F32)
    return pl.pallas_call(
        _filter_kernel,
        grid=(HYENA_ORDER, nc),
        in_specs=[small((l, LANE)), small((l, 1)), small((LANE, hp)), small((1, hp)), small((hp, hp)),
                  small((1, hp)), small((2, hp)),
                  pl.BlockSpec((hp, tc), lambda o, c: (0, 2 * o * nc + c)),
                  pl.BlockSpec((hp, tc), lambda o, c: (0, (2 * o + 1) * nc + c)),
                  pl.BlockSpec((1, tc), lambda o, c: (0, c))] + [small((l, l))] * 4,
        out_specs=[pl.BlockSpec((None, l, tc), lambda o, c: (o, 0, c))] * 3,
        out_shape=[out, out, out],
        compiler_params=_params("arbitrary", "arbitrary"),
        name="hyena_filters",
    )(z, t, _pad_to(w1, (LANE, hp)), _pad_to(b1[None, :], (1, hp)), _pad_to(w2, (hp, hp)),
      _pad_to(b2[None, :], (1, hp)), _pad_to(freq, (2, hp)), _pad_to(w3, (hp, w3.shape[1])),
      _pad_to(w3, (hp, w3.shape[1])), deltas, *_split_bf16(fwd_re), *_split_bf16(fwd_im))


HYENA_ROW_CHUNK = 256


def _hyena_kernel(pv_ref, p1_ref, p2_ref, wv_ref, w1_ref, w2_ref, bv_ref, b1_ref, b2_ref,
                  fre_ref, fim_ref, gre_ref, gim_ref, kr_ref, ki_ref, kn_ref, fbias_ref, o_ref, *, l):
    tc = o_ref.shape[1]
    row = lax.broadcasted_iota(jnp.int32, (l, tc), 0)

    def short_conv(p, w_ref, b_ref):
        prev = jnp.where(row == 0, 0.0, pltpu.roll(p, 1, 0))
        nxt = jnp.where(row == l - 1, 0.0, pltpu.roll(p, l - 1, 0))
        w = w_ref[...]
        out = b_ref[...] + prev * w[0:1]
        out = out + p * w[1:2]
        return out + nxt * w[2:3]

    chunks = [slice(r, r + HYENA_ROW_CHUNK) for r in range(0, l, HYENA_ROW_CHUNK)] if l > HYENA_ROW_CHUNK \
        else [slice(0, l)]

    def gated_long_conv(x, u, o, bias):
        ub = u.astype(BF16)
        w_re, w_im = [], []
        for rs in chunks:
            u_re = _dot(fre_ref[rs, :], ub)
            u_im = _dot(fim_ref[rs, :], ub)
            kr, ki, kn = kr_ref[o, rs, :], ki_ref[o, rs, :], kn_ref[o, rs, :]
            w_re.append((u_re * kr - u_im * ki).astype(BF16))
            w_im.append((u_re * ki + u_im * kn).astype(BF16))
        w_re = jnp.concatenate(w_re, axis=0)
        w_im = jnp.concatenate(w_im, axis=0)
        out = []
        for rs in chunks:
            y = _dot(gre_ref[rs, :], w_re) + _dot(gim_ref[rs, :], w_im)
            out.append(x[rs] * (y + u[rs] * bias))
        return jnp.concatenate(out, axis=0)

    fbias = fbias_ref[...]
    for s in range(o_ref.shape[0] // l):
        rs = slice(s * l, (s + 1) * l)
        v = short_conv(pv_ref[rs, :], wv_ref, bv_ref)
        z = gated_long_conv(short_conv(p1_ref[rs, :], w1_ref, b1_ref), v, 0, fbias[0:1])
        z = gated_long_conv(short_conv(p2_ref[rs, :], w2_ref, b2_ref), z, 1, fbias[1:2])
        o_ref[rs, :] = z.astype(o_ref.dtype)


def _hyena(st, proj, col0, width, e, w_short, b_short, mats, filters, fbias):
    l = st.l
    tc = _tile(width, 1024 if l <= 256 else 256)
    per_step = 2 if l > 256 and st.n_seq % 2 == 0 else 1
    nc = width // tc
    c0 = col0 // tc
    rows = per_step * l
    once = pl.Buffered(1)
    p_specs = [pl.BlockSpec((rows, tc), lambda c, b, part=part: (b, c0 + part * nc + c)) for part in range(3)]
    w_specs = [pl.BlockSpec((None, SHORT_CONV, tc), lambda c, b, part=part: (e, 0, part * nc + c))
               for part in range(3)]
    b_specs = [pl.BlockSpec((None, 1, tc), lambda c, b, part=part: (e, 0, part * nc + c)) for part in range(3)]
    mat_specs = [pl.BlockSpec((l, l), lambda c, b: (0, 0), pipeline_mode=once)] * 4
    k_specs = [pl.BlockSpec((HYENA_ORDER, l, tc), lambda c, b: (0, 0, c), pipeline_mode=once)] * 3
    return pl.pallas_call(
        functools.partial(_hyena_kernel, l=l),
        grid=(nc, st.n_seq // per_step),
        in_specs=p_specs + w_specs + b_specs + mat_specs + k_specs
        + [pl.BlockSpec((None, HYENA_ORDER, tc), lambda c, b: (e, 0, c))],
        out_specs=pl.BlockSpec((rows, tc), lambda c, b: (b, c)),
        out_shape=jax.ShapeDtypeStruct((st.n, width), BF16),
        compiler_params=_params("parallel", "arbitrary"),
        name="hyena",
    )(proj, proj, proj, w_short, w_short, w_short, *([b_short.reshape(b_short.shape[0], 1, -1)] * 3),
      *[a.astype(BF16) for a in mats], *filters, fbias)


FNET_ROW_CHUNK = 256


def _fnet_mix_kernel(cos_l_ref, sin_l_ref, cos_c_ref, sin_c_ref, h_ref, o_ref, *, gw):
    l, tn = h_ref.shape
    cos_c = cos_c_ref[...]
    sin_c = sin_c_ref[...]
    for g in range(tn // gw):
        cs = slice(g * gw, (g + 1) * gw)
        h = h_ref[:, cs]
        for r in range(0, l, FNET_ROW_CHUNK):
            rs = slice(r, min(r + FNET_ROW_CHUNK, l))
            a = _dot(cos_l_ref[rs, :], h).astype(BF16)
            b = _dot(sin_l_ref[rs, :], h).astype(BF16)
            o_ref[rs, cs] = (_dot(a, cos_c) - _dot(b, sin_c)).astype(o_ref.dtype)


def _fnet_mix(st, h):
    l = st.l
    d = h.shape[1]
    gw = d // FNET_GROUPS
    groups_per_step = min(FNET_GROUPS, max(1, 2048 // gw)) if l <= 256 else 1
    tn = gw * groups_per_step
    cos_l, sin_l = _cos_sin(l, l, l)
    cos_c, sin_c = _cos_sin(gw, gw, gw)
    mats = [(cos_l * l ** -0.5).astype(BF16), (sin_l * l ** -0.5).astype(BF16),
            (cos_c * gw ** -0.5).astype(BF16), (sin_c * gw ** -0.5).astype(BF16)]
    once = pl.Buffered(1)
    return pl.pallas_call(
        functools.partial(_fnet_mix_kernel, gw=gw),
        grid=(st.n_seq, d // tn),
        in_specs=[pl.BlockSpec((l, l), lambda b, j: (0, 0), pipeline_mode=once)] * 2
        + [pl.BlockSpec((gw, gw), lambda b, j: (0, 0), pipeline_mode=once)] * 2
        + [pl.BlockSpec((l, tn), lambda b, j: (b, j))],
        out_specs=pl.BlockSpec((l, tn), lambda b, j: (b, j)),
        out_shape=jax.ShapeDtypeStruct((st.n, d), BF16),
        compiler_params=_params("parallel", "arbitrary"),
        name="fnet_mix",
    )(*mats, h)


def _rms_kernel(x_ref, g_ref, o_ref):
    rows = BF16_SUBLANES
    g = g_ref[...]

    def body(c, carry):
        r = pl.multiple_of(c * rows, rows)
        x = x_ref[pl.ds(r, rows), :]
        o_ref[pl.ds(r, rows), :] = x * lax.rsqrt(jnp.mean(x * x, axis=-1, keepdims=True) + EPS) * g
        return carry

    lax.fori_loop(0, x_ref.shape[0] // rows, body, 0)


def _rms_norm(x, g):
    m, d = x.shape
    tm = _tile(m, 256, BF16_SUBLANES)
    return pl.pallas_call(
        _rms_kernel,
        grid=(m // tm,),
        in_specs=[pl.BlockSpec((tm, d), lambda i: (i, 0)), pl.BlockSpec((1, d), lambda i: (0, 0))],
        out_specs=pl.BlockSpec((tm, d), lambda i: (i, 0)),
        out_shape=jax.ShapeDtypeStruct((m, d), F32),
        compiler_params=_params("parallel"),
        name="final_norm",
    )(x, g.reshape(1, d))


def kernel(x_prompt, x_sample, cache_k, cache_v, c, c_ctx, w_mod, b_mod, norm_g, ffn_w1, ffn_w3, ffn_w2,
           w_in, w_out, rpb, w_short, b_short, filt_w1, filt_b1, filt_w2, filt_b2, filt_w3, filt_freq,
           filt_bias, w_fnet, final_g):
    b_p, l_p, d = x_prompt.shape
    b_s, l_s, _ = x_sample.shape
    depth = w_mod.shape[0]
    attn_w = cache_k.shape[-2] * cache_k.shape[-1]
    hyena_w = filt_bias.shape[-1]
    n_heads = attn_w // HEAD_DIM
    past = cache_k.shape[2]
    assert 1 + b_s <= COND_ROWS

    streams = (_Stream(b_p, l_p, 0, True), _Stream(b_s, l_s, 1, False))
    xs = [x_prompt.reshape(b_p * l_p, d), x_sample.reshape(b_s * l_s, d)]
    conds = jnp.concatenate([c_ctx[None, :], c, jnp.zeros((COND_ROWS - 1 - b_s, d), F32)], axis=0)
    w_in_b, w_out_b, w_fnet_b = w_in.astype(BF16), w_out.astype(BF16), w_fnet.astype(BF16)
    rfft_mats = {st.l: _rfft_matrices(st.l) for st in streams}

    ride_along = _ffn_cast_ok(streams[0], d, ffn_w1.shape[-1])
    ffn_w = {(0, 0): (ffn_w1[0, 0].astype(BF16), ffn_w3[0, 0].astype(BF16), ffn_w2[0, 0].astype(BF16))}

    def ffn_half_step(xs, mods, l, s):
        nxt = (l, 1) if s == 0 else (l + 1, 0)
        w = ffn_w[(l, s)]
        g, mi = norm_g[l, 2 * s], 6 * s
        if nxt[0] == depth:
            return [_ffn(st, x, mods, g, *w, mi) for st, x in zip(streams, xs)]
        if ride_along:
            x0, ffn_w[nxt] = _ffn(streams[0], xs[0], mods, g, *w, mi, cast_next=(ffn_w1, ffn_w3, ffn_w2, *nxt))
        else:
            x0 = _ffn(streams[0], xs[0], mods, g, *w, mi)
            ffn_w[nxt] = tuple(a[nxt].astype(BF16) for a in (ffn_w1, ffn_w3, ffn_w2))
        return [x0, _ffn(streams[1], xs[1], mods, g, *w, mi)]

    new_k, new_v = [], []
    for l in range(depth):
        mods = _adaln(conds, w_mod, b_mod, l)
        xs = ffn_half_step(xs, mods, l, 0)
        if l % 2 == 0:
            e = l // 2
            projs = [_norm_mm(st, x, mods, norm_g[l, 1], w_in_b, e, 3) for st, x in zip(streams, xs)]
            attn = [_ctx_attn(streams[0], projs[0], attn_w),
                    _na_attn(streams[1], projs[1], attn_w, cache_k[:, e].reshape(b_s * past, attn_w),
                             cache_v[:, e].reshape(b_s * past, attn_w), rpb[e])]
            for i, st in enumerate(streams):
                mats = rfft_mats[st.l]
                filters = _hyena_filters(st.l, hyena_w, filt_w1[e], filt_b1[e], filt_w2[e], filt_b2[e],
                                         filt_w3[e], filt_freq[e], mats[0], mats[1])
                hy = _hyena(st, projs[i], 3 * attn_w, hyena_w, e, w_short, b_short, mats, filters, filt_bias)
                xs[i] = _mm_res(st, attn[i], hy, w_out_b, e, xs[i], mods, 5, "out_proj")
            new_k.append(projs[0][:, attn_w:2 * attn_w].reshape(b_p, l_p, n_heads, HEAD_DIM))
            new_v.append(projs[0][:, 2 * attn_w:3 * attn_w].reshape(b_p, l_p, n_heads, HEAD_DIM))
        else:
            o = l // 2
            for i, st in enumerate(streams):
                mixed = _fnet_mix(st, _mod_norm(st, xs[i], mods, norm_g[l, 1], 3))
                xs[i] = _mm_res(st, mixed, None, w_fnet_b, o, xs[i], mods, 5, "fnet_out")
        xs = ffn_half_step(xs, mods, l, 1)

    y_prompt = _rms_norm(xs[0], final_g).reshape(b_p, l_p, d)
    y_sample = _rms_norm(xs[1], final_g).reshape(b_s, l_s, d)
    return y_prompt, y_sample, jnp.stack(new_k, axis=1), jnp.stack(new_v, axis=1)
```

```python
import dataclasses
import functools
import math

import jax
import jax.numpy as jnp
from jax import lax
from jax.experimental import pallas as pl
from jax.experimental.pallas import tpu as pltpu

F32 = jnp.float32
BF16 = jnp.bfloat16
HIGHEST = lax.Precision.HIGHEST

N_MOD = 9
GRID_W = 64
WIN_H = 8
WIN_W = 16
HEAD_DIM = 128
SHORT_CONV = 3
FILTER_EMB = 33
HYENA_ORDER = 2
DECAY_TARGET = 1e-2
FAST_DECAY_PCT = 0.3
SLOW_DECAY_PCT = 1.5
FNET_GROUPS = 8
EPS = 1e-6
NEG_INF = -1e30

V7X_VMEM_BYTES = 64 * 2**20
VMEM_LIMIT = V7X_VMEM_BYTES - 8 * 2**20
LANE = 128
BF16_SUBLANES = 16
COND_ROWS = 16


def _tile(n, pref, unit=LANE):
    if n <= pref:
        return n
    t = (pref // unit) * unit
    while t > unit and n % t:
        t -= unit
    assert n % t == 0, (n, pref)
    return t


def _params(*sem):
    return pltpu.CompilerParams(dimension_semantics=sem, vmem_limit_bytes=VMEM_LIMIT)


@dataclasses.dataclass(frozen=True)
class _Stream:
    n_seq: int
    l: int
    cond0: int
    shared: bool

    @property
    def n(self):
        return self.n_seq * self.l

    def row_tile(self, pref):
        return _tile(self.n if self.shared else self.l, pref, BF16_SUBLANES)

    def cond(self, i, tm):
        return self.cond0 if self.shared else self.cond0 + i // (self.l // tm)


def _fill_mod_norm(x_ref, g_ref, mod_ref, h_ref, rs_ref, mi):
    rows = BF16_SUBLANES
    tm, d = x_ref.shape
    cw = _tile(d, 512)

    def stats(c, carry):
        r = pl.multiple_of(c * rows, rows)
        x = x_ref[pl.ds(r, rows), :]
        rs = lax.rsqrt(jnp.mean(x * x, axis=-1, keepdims=True) + EPS)
        rs_ref[pl.ds(r, rows), :] = jnp.broadcast_to(rs, (rows, LANE))
        return carry

    lax.fori_loop(0, tm // rows, stats, 0, unroll=4)

    m = mod_ref[0]
    for t in range(d // cw):
        cs = slice(t * cw, (t + 1) * cw)
        gain = g_ref[:, cs] * (1.0 + m[mi + 1:mi + 2, cs])
        shift = m[mi:mi + 1, cs]

        def apply(c, carry):
            r = pl.multiple_of(c * rows, rows)
            rs = rs_ref[pl.ds(r, rows), :]
            rs = jnp.concatenate([rs] * (cw // LANE), axis=1) if cw > LANE else rs[:, :cw]
            h_ref[pl.ds(r, rows), cs] = (x_ref[pl.ds(r, rows), cs] * rs * gain + shift).astype(h_ref.dtype)
            return carry

        lax.fori_loop(0, tm // rows, apply, 0)


def _dot(a, b):
    return jnp.dot(a, b, preferred_element_type=F32)


def _dot_hp(a, b):
    return jnp.dot(a, b, preferred_element_type=F32, precision=HIGHEST)


def _dot_nt(a, b):
    return lax.dot_general(a, b, (((1,), (1,)), ((), ())), preferred_element_type=F32)


def _adaln_kernel(c_ref, w_ref, b_ref, o_ref):
    c = c_ref[...]
    s = c * (1.0 / (1.0 + jnp.exp(-c)))
    s_hi = s.astype(BF16)
    s_lo = (s - s_hi.astype(F32)).astype(BF16)
    w = w_ref[...].astype(BF16)
    o_ref[...] = _dot(s_hi, w) + _dot(s_lo, w) + b_ref[...]


def _adaln(conds, w_mod, b_mod, l):
    d = conds.shape[1]
    n = w_mod.shape[-1]
    tn = _tile(n, 1024)
    out = pl.pallas_call(
        _adaln_kernel,
        grid=(n // tn,),
        in_specs=[
            pl.BlockSpec((COND_ROWS, d), lambda j: (0, 0)),
            pl.BlockSpec((None, d, tn), lambda j: (l, 0, j)),
            pl.BlockSpec((None, 1, tn), lambda j: (l, 0, j)),
        ],
        out_specs=pl.BlockSpec((COND_ROWS, tn), lambda j: (0, j)),
        out_shape=jax.ShapeDtypeStruct((COND_ROWS, n), F32),
        compiler_params=_params("arbitrary"),
        name="adaln",
    )(conds, w_mod, b_mod.reshape(b_mod.shape[0], 1, n))
    return out.reshape(COND_ROWS, N_MOD, d)


def _ffn_kernel(x_ref, mod_ref, g_ref, fin_ref, w1_ref, w3_ref, w2_ref, *rest, mi, final_norm):
    j = pl.program_id(1)
    f32_tiles, bf16_tiles = (), ()
    if len(rest) > 3:
        f32_tiles, (o_ref, *bf16_tiles), (h_ref, rs_ref) = rest[:3], rest[3:7], rest[7:]
    else:
        o_ref, h_ref, rs_ref = rest

    @pl.when(j == 0)
    def _():
        _fill_mod_norm(x_ref, g_ref, mod_ref, h_ref, rs_ref, mi)
        o_ref[...] = jnp.zeros_like(o_ref)

    for src, dst in zip(f32_tiles, bf16_tiles):
        dst[...] = src[...].astype(dst.dtype)

    h = h_ref[...]
    a = _dot(h, w1_ref[...])
    b = _dot(h, w3_ref[...])
    act = (a * (1.0 / (1.0 + jnp.exp(-a))) * b).astype(BF16)
    o_ref[...] += _dot(act, w2_ref[...])

    @pl.when(j == pl.num_programs(1) - 1)
    def _():
        rows = BF16_SUBLANES
        tm, d = o_ref.shape
        half_gate = 0.5 * mod_ref[0][mi + 2:mi + 3]

        def body(c, carry):
            r = pl.multiple_of(c * rows, rows)
            y = x_ref[pl.ds(r, rows), :] + half_gate * o_ref[pl.ds(r, rows), :]
            o_ref[pl.ds(r, rows), :] = y
            if final_norm:
                rs = lax.rsqrt(jnp.mean(y * y, axis=-1, keepdims=True) + EPS)
                rs_ref[pl.ds(r, rows), :] = jnp.broadcast_to(rs, (rows, LANE))
            return carry

        lax.fori_loop(0, tm // rows, body, 0, unroll=4 if final_norm else 1)

        if final_norm:
            cw = _tile(d, 512)
            for t in range(d // cw):
                cs = slice(t * cw, (t + 1) * cw)
                gain = fin_ref[:, cs]

                def scale(c, carry):
                    r = pl.multiple_of(c * rows, rows)
                    rs = jnp.concatenate([rs_ref[pl.ds(r, rows), :]] * (cw // LANE), axis=1)
                    o_ref[pl.ds(r, rows), cs] = o_ref[pl.ds(r, rows), cs] * rs * gain
                    return carry

                lax.fori_loop(0, tm // rows, scale, 0)


FFN_TF = 256


def _ffn_cast_ok(st, d, f):
    tm = st.row_tile(512)
    nm = st.n // tm
    return f % FFN_TF == 0 and d % nm == 0 and (d // nm) % LANE == 0


def _ffn(st, x, mods, g, final_g, w1, w3, w2, mi, final_norm=False, cast_next=None):
    m, d = x.shape
    f = w1.shape[-1]
    tm = st.row_tile(512)
    tf = _tile(f, FFN_TF)
    nm, nf = m // tm, f // tf
    in_specs = [
        pl.BlockSpec((tm, d), lambda i, j: (i, 0)),
        pl.BlockSpec((1, N_MOD, d), lambda i, j: (st.cond(i, tm), 0, 0)),
        pl.BlockSpec((1, d), lambda i, j: (0, 0)),
        pl.BlockSpec((1, d), lambda i, j: (0, 0)),
        pl.BlockSpec((d, tf), lambda i, j: (0, j)),
        pl.BlockSpec((d, tf), lambda i, j: (0, j)),
        pl.BlockSpec((tf, d), lambda i, j: (j, 0)),
    ]
    out_specs = [pl.BlockSpec((tm, d), lambda i, j: (i, 0))]
    out_shape = [jax.ShapeDtypeStruct((m, d), F32)]
    args = [x, mods, g.reshape(1, d), final_g.reshape(1, d), w1, w3, w2]
    if cast_next is not None:
        n1, n3, n2, ln, sn = cast_next
        dr = d // nm
        in_specs += [pl.BlockSpec((None, None, dr, tf), lambda i, j: (ln, sn, i, j)),
                     pl.BlockSpec((None, None, dr, tf), lambda i, j: (ln, sn, i, j)),
                     pl.BlockSpec((None, None, tf, dr), lambda i, j: (ln, sn, j, i))]
        out_specs += [pl.BlockSpec((dr, tf), lambda i, j: (i, j)), pl.BlockSpec((dr, tf), lambda i, j: (i, j)),
                      pl.BlockSpec((tf, dr), lambda i, j: (j, i))]
        out_shape += [jax.ShapeDtypeStruct((d, f), BF16), jax.ShapeDtypeStruct((d, f), BF16),
                      jax.ShapeDtypeStruct((f, d), BF16)]
        args += [n1, n3, n2]
    out = pl.pallas_call(
        functools.partial(_ffn_kernel, mi=mi, final_norm=final_norm),
        grid=(nm, nf),
        in_specs=in_specs,
        out_specs=out_specs,
        out_shape=out_shape,
        scratch_shapes=[pltpu.VMEM((tm, d), BF16), pltpu.VMEM((tm, LANE), F32)],
        compiler_params=_params("parallel", "arbitrary"),
        name="ffn",
    )(*args)
    return out[0] if cast_next is None else (out[0], tuple(out[1:]))


def _norm_mm_kernel(x_ref, mod_ref, g_ref, w_ref, *rest, mi, tile_ranges):
    outs, (h_ref, rs_ref) = rest[:-2], rest[-2:]
    j = pl.program_id(1)

    @pl.when(j == 0)
    def _():
        _fill_mod_norm(x_ref, g_ref, mod_ref, h_ref, rs_ref, mi)

    for o_ref, (lo, hi) in zip(outs, tile_ranges):
        @pl.when((j >= lo) & (j < hi))
        def _(o_ref=o_ref):
            o_ref[...] = _dot(h_ref[...], w_ref[...]).astype(o_ref.dtype)


def _norm_mm(st, x, mods, g, w, e, mi, splits):
    m, d = x.shape
    n = w.shape[-1]
    tm = st.row_tile(512)
    tn = _tile(math.gcd(*[width for width, _ in splits]), 1024)
    assert sum(width for width, _ in splits) == n
    tile_ranges, out_specs, lo = [], [], 0
    for width, _ in splits:
        hi = lo + width // tn
        tile_ranges.append((lo, hi))
        out_specs.append(pl.BlockSpec((tm, tn), lambda i, j, lo=lo, hi=hi: (i, jnp.clip(j - lo, 0, hi - lo - 1))))
        lo = hi
    return pl.pallas_call(
        functools.partial(_norm_mm_kernel, mi=mi, tile_ranges=tuple(tile_ranges)),
        grid=(m // tm, n // tn),
        in_specs=[
            pl.BlockSpec((tm, d), lambda i, j: (i, 0)),
            pl.BlockSpec((1, N_MOD, d), lambda i, j: (st.cond(i, tm), 0, 0)),
            pl.BlockSpec((1, d), lambda i, j: (0, 0)),
            pl.BlockSpec((None, d, tn), lambda i, j: (e, 0, j)),
        ],
        out_specs=out_specs,
        out_shape=[jax.ShapeDtypeStruct((m, width), dtype) for width, dtype in splits],
        scratch_shapes=[pltpu.VMEM((tm, d), BF16), pltpu.VMEM((tm, LANE), F32)],
        compiler_params=_params("parallel", "arbitrary"),
        name="in_proj",
    )(x, mods, g.reshape(1, d), w)


def _mod_norm_kernel(x_ref, mod_ref, g_ref, h_ref, rs_ref, *, mi):
    _fill_mod_norm(x_ref, g_ref, mod_ref, h_ref, rs_ref, mi)


def _mod_norm(st, x, mods, g, mi):
    m, d = x.shape
    tm = st.row_tile(256)
    return pl.pallas_call(
        functools.partial(_mod_norm_kernel, mi=mi),
        grid=(m // tm,),
        in_specs=[
            pl.BlockSpec((tm, d), lambda i: (i, 0)),
            pl.BlockSpec((1, N_MOD, d), lambda i: (st.cond(i, tm), 0, 0)),
            pl.BlockSpec((1, d), lambda i: (0, 0)),
        ],
        out_specs=pl.BlockSpec((tm, d), lambda i: (i, 0)),
        out_shape=jax.ShapeDtypeStruct((m, d), BF16),
        scratch_shapes=[pltpu.VMEM((tm, LANE), F32)],
        compiler_params=_params("parallel"),
        name="mod_norm",
    )(x, mods, g.reshape(1, d))


def _mm_res_kernel(a0_ref, a1_ref, w0_ref, w1_ref, res_ref, mod_ref, o_ref, *, gi):
    mix = _dot(a0_ref[...], w0_ref[...]) + _dot(a1_ref[...], w1_ref[...])
    o_ref[...] = res_ref[...] + mod_ref[0][gi:gi + 1] * mix


def _mm_res(st, a0, a1, w, e, res, mods, gi, name):
    k = w.shape[1] // 2
    m = a0.shape[0]
    c1 = 0
    if a1 is None:
        a1, c1 = a0, 1
    assert a0.shape[1] >= k and a1.shape == (m, (c1 + 1) * k)
    n = w.shape[-1]
    tm = st.row_tile(1024)
    tn = _tile(n, 1024)
    return pl.pallas_call(
        functools.partial(_mm_res_kernel, gi=gi),
        grid=(m // tm, n // tn),
        in_specs=[
            pl.BlockSpec((tm, k), lambda i, j: (i, 0)),
            pl.BlockSpec((tm, k), lambda i, j: (i, c1)),
            pl.BlockSpec((None, k, tn), lambda i, j: (e, 0, j)),
            pl.BlockSpec((None, k, tn), lambda i, j: (e, 1, j)),
            pl.BlockSpec((tm, tn), lambda i, j: (i, j)),
            pl.BlockSpec((1, N_MOD, tn), lambda i, j: (st.cond(i, tm), 0, j)),
        ],
        out_specs=pl.BlockSpec((tm, tn), lambda i, j: (i, j)),
        out_shape=jax.ShapeDtypeStruct((m, n), F32),
        compiler_params=_params("parallel", "arbitrary"),
        name=name,
    )(a0, a1, w, w, res, mods)


def _ctx_attn_kernel(q_ref, k_ref, v_ref, o_ref, *, n_heads, scale):
    for h in range(n_heads):
        sl = slice(h * HEAD_DIM, (h + 1) * HEAD_DIM)
        q = q_ref[:, sl].astype(BF16)
        k = k_ref[:, sl].astype(BF16)
        v = v_ref[:, sl].astype(BF16)
        s = _dot_nt(q, k) * scale
        e = jnp.exp(s - jnp.max(s, axis=-1, keepdims=True))
        p = e * (1.0 / jnp.sum(e, axis=-1, keepdims=True))
        o_ref[:, sl] = _dot(p.astype(BF16), v).astype(o_ref.dtype)


def _ctx_attn(st, q, k, v):
    l = st.l
    width = q.shape[1]
    return pl.pallas_call(
        functools.partial(_ctx_attn_kernel, n_heads=width // HEAD_DIM, scale=HEAD_DIM ** -0.5),
        grid=(st.n_seq,),
        in_specs=[pl.BlockSpec((l, width), lambda b: (b, 0))] * 3,
        out_specs=pl.BlockSpec((l, width), lambda b: (b, 0)),
        out_shape=jax.ShapeDtypeStruct((st.n, width), BF16),
        compiler_params=_params("parallel"),
        name="ctx_attn",
    )(q, k, v)


def _na_window(r, rows, kh):
    start = min(max(r - kh // 2, 0), rows - kh)
    lo = start - start % 2
    hi = start + kh + (start + kh) % 2
    return start, lo, hi


NA_ROW_GROUP = 4


def _na_attn_kernel(q_ref, k_ref, v_ref, ck_ref, cv_ref, tab_ref, o_ref, *, rows, kh, scale):
    ck = ck_ref[...].astype(BF16)
    cv = cv_ref[...].astype(BF16)
    for g0 in range(0, rows, NA_ROW_GROUP):
        group = range(g0, min(g0 + NA_ROW_GROUP, rows))
        windows = [_na_window(r, rows, kh) for r in group]
        glo = min(w[1] for w in windows) * GRID_W
        ghi = max(w[2] for w in windows) * GRID_W
        q = q_ref[g0 * GRID_W:(g0 + len(group)) * GRID_W, :].astype(BF16)
        s_lat_g = _dot_nt(q, k_ref[glo:ghi, :].astype(BF16))
        s_ctx_g = _dot_nt(q, ck)
        p_lat_g, p_ctx_g = [], []
        for i, (r, (start, lo, hi)) in enumerate(zip(group, windows)):
            qs = slice(i * GRID_W, (i + 1) * GRID_W)
            pieces = []
            for p in range(lo // 2, hi // 2):
                ok0 = start <= 2 * p < start + kh
                ok1 = start <= 2 * p + 1 < start + kh
                dr = 2 * p - r + WIN_H - 1
                pieces.append(tab_ref[0, 0, dr] if ok0 and ok1 else tab_ref[0, 1, dr] if ok0
                              else tab_ref[0, 2, dr + 1])
            lo, hi = lo * GRID_W, hi * GRID_W
            s_lat = s_lat_g[qs, lo - glo:hi - glo] * scale + jnp.concatenate(pieces, axis=1)
            s_ctx = s_ctx_g[qs, :] * scale
            mx = jnp.maximum(jnp.max(s_lat, axis=-1, keepdims=True), jnp.max(s_ctx, axis=-1, keepdims=True))
            e_lat = jnp.exp(s_lat - mx)
            e_ctx = jnp.exp(s_ctx - mx)
            inv = 1.0 / (jnp.sum(e_lat, axis=-1, keepdims=True) + jnp.sum(e_ctx, axis=-1, keepdims=True))
            parts = [(e_lat * inv).astype(BF16)]
            if lo > glo:
                parts.insert(0, jnp.zeros((GRID_W, lo - glo), BF16))
            if hi < ghi:
                parts.append(jnp.zeros((GRID_W, ghi - hi), BF16))
            p_lat_g.append(jnp.concatenate(parts, axis=1) if len(parts) > 1 else parts[0])
            p_ctx_g.append((e_ctx * inv).astype(BF16))
        o = (_dot(jnp.concatenate(p_lat_g, axis=0), v_ref[glo:ghi, :].astype(BF16))
             + _dot(jnp.concatenate(p_ctx_g, axis=0), cv))
        o_ref[g0 * GRID_W:(g0 + len(group)) * GRID_W, :] = o.astype(o_ref.dtype)


def _na_bias_table(rpb):
    col = jnp.arange(GRID_W)
    col_start = jnp.clip(col - WIN_W // 2, 0, GRID_W - WIN_W)
    col_ok = (col[None, :] >= col_start[:, None]) & (col[None, :] < col_start[:, None] + WIN_W)
    dc_idx = jnp.clip(col[None, :] - col[:, None] + WIN_W - 1, 0, 2 * WIN_W - 2)
    t = jnp.where(col_ok, rpb.astype(F32)[:, :, dc_idx], NEG_INF)
    neg = jnp.full_like(t, NEG_INF)
    nxt = jnp.concatenate([t[:, 1:], neg[:, :1]], axis=1)
    return jnp.stack([jnp.concatenate([t, nxt], axis=-1), jnp.concatenate([t, neg], axis=-1),
                      jnp.concatenate([neg, t], axis=-1)], axis=1)


def _na_attn(st, q, k, v, cache_k, cache_v, rpb):
    l = st.l
    width = q.shape[1]
    n_heads = width // HEAD_DIM
    rows = l // GRID_W
    kh = min(WIN_H, rows)
    assert rows % 2 == 0 and kh % 2 == 0
    past = cache_k.shape[0] // st.n_seq
    tab = _na_bias_table(rpb)
    return pl.pallas_call(
        functools.partial(_na_attn_kernel, rows=rows, kh=kh, scale=HEAD_DIM ** -0.5),
        grid=(st.n_seq, n_heads),
        in_specs=[pl.BlockSpec((l, HEAD_DIM), lambda b, h: (b, h))] * 3 + [
            pl.BlockSpec((past, HEAD_DIM), lambda b, h: (b, h)),
            pl.BlockSpec((past, HEAD_DIM), lambda b, h: (b, h)),
            pl.BlockSpec((1,) + tab.shape[1:], lambda b, h: (h, 0, 0, 0, 0)),
        ],
        out_specs=pl.BlockSpec((l, HEAD_DIM), lambda b, h: (b, h)),
        out_shape=jax.ShapeDtypeStruct((st.n, width), BF16),
        compiler_params=_params("parallel", "arbitrary"),
        name="na_attn",
    )(q, k, v, cache_k, cache_v, tab)


def _cos_sin(n_rows, n_cols, period):
    r = lax.broadcasted_iota(jnp.int32, (n_rows, n_cols), 0)
    c = lax.broadcasted_iota(jnp.int32, (n_rows, n_cols), 1)
    ang = ((r * c) % period).astype(F32) * (2.0 * math.pi / period)
    return jnp.cos(ang), jnp.sin(ang)


def _rfft_matrices(l):
    n = 2 * l
    cos, sin = _cos_sin(l, l, n)
    alt = jnp.where(lax.broadcasted_iota(jnp.int32, (l, l), 1) % 2 == 0, 1.0, -1.0).astype(F32)
    k_is0 = lax.broadcasted_iota(jnp.int32, (l, l), 0) == 0
    fwd_re = cos
    fwd_im = jnp.where(k_is0, alt, -sin)
    bin_is0 = lax.broadcasted_iota(jnp.int32, (l, l), 1) == 0
    inv_re = jnp.where(bin_is0, 1.0 / n, 2.0 / n) * cos
    inv_im = jnp.where(bin_is0, alt.T / n, (-2.0 / n) * sin)
    return fwd_re, fwd_im, inv_re, inv_im


def _split_bf16(a):
    hi = a.astype(BF16)
    return hi, (a - hi.astype(F32)).astype(BF16)


def _dot_3pass(a_hi_ref, a_lo_ref, b):
    b_hi, b_lo = _split_bf16(b)
    a_hi = a_hi_ref[...]
    return _dot(a_hi, b_hi) + (_dot(a_hi, b_lo) + _dot(a_lo_ref[...], b_hi))


def _filter_kernel(z_ref, t_ref, w1_ref, b1_ref, w2_ref, b2_ref, fq_ref, w3f_ref, w3b_ref, dl_ref,
                   fre_hi_ref, fre_lo_ref, fim_hi_ref, fim_lo_ref, kr_ref, ki_ref, kn_ref):
    fq = fq_ref[...]
    hid = jnp.sin(fq[0:1] * (_dot_hp(z_ref[...], w1_ref[...]) + b1_ref[...]))
    hid = jnp.sin(fq[1:2] * (_dot_hp(hid, w2_ref[...]) + b2_ref[...]))
    decay = jnp.exp(-t_ref[...] * dl_ref[...])
    row = lax.broadcasted_iota(jnp.int32, kr_ref.shape, 0)
    h_fwd = _dot_hp(hid, w3f_ref[...]) * decay
    h_bwd = jnp.where(row == 0, 0.0, _dot_hp(hid, w3b_ref[...]) * decay)
    nrm = lax.rsqrt(jnp.sum(h_fwd * h_fwd + h_bwd * h_bwd, axis=0, keepdims=True) + EPS)
    even = (h_fwd + h_bwd) * nrm
    odd = (h_fwd - h_bwd) * nrm
    k_re = _dot_3pass(fre_hi_ref, fre_lo_ref, even)
    k_im = _dot_3pass(fim_hi_ref, fim_lo_ref, odd)
    nyq = jnp.sum(jnp.where(row % 2 == 0, even, -even), axis=0, keepdims=True)
    kr_ref[...] = k_re
    ki_ref[...] = jnp.where(row == 0, 0.0, k_im)
    kn_ref[...] = jnp.where(row == 0, nyq, k_re)


def _pad_to(a, shape):
    return jnp.pad(a.astype(F32), [(0, s - n) for n, s in zip(a.shape, shape)])


def _hyena_filters(l, width, w1, b1, w2, b2, w3, freq, fwd_re, fwd_im):
    hid = w2.shape[0]
    hp = LANE * pl.cdiv(hid, LANE)
    n_bands = (FILTER_EMB - 1) // 2
    t = jnp.linspace(0.0, 1.0, l, dtype=F32)[:, None]
    w = (2.0 * math.pi / l) * jnp.arange(l, dtype=F32)[:, None]
    bands = jnp.linspace(1e-4, n_bands - 1, n_bands, dtype=F32)[None, :]
    z = _pad_to(jnp.concatenate([t, jnp.cos(bands * w), -jnp.sin(bands * w)], axis=-1), (l, LANE))
    deltas = jnp.abs(jnp.linspace(math.log(DECAY_TARGET) / FAST_DECAY_PCT,
                                  math.log(DECAY_TARGET) / SLOW_DECAY_PCT, width, dtype=F32))[None, :]
    tc = _tile(width, 512)
    nc = width // tc
    small = lambda shape: pl.BlockSpec(shape, lambda o, c: (0,) * len(shape))
    out = jax.ShapeDtypeStruct((HYENA_ORDER, l, width), F32)
    return pl.pallas_call(
        _filter_kernel,
        grid=(HYENA_ORDER, nc),
        in_specs=[small((l, LANE)), small((l, 1)), small((LANE, hp)), small((1, hp)), small((hp, hp)),
                  small((1, hp)), small((2, hp)),
                  pl.BlockSpec((hp, tc), lambda o, c: (0, 2 * o * nc + c)),
                  pl.BlockSpec((hp, tc), lambda o, c: (0, (2 * o + 1) * nc + c)),
                  pl.BlockSpec((1, tc), lambda o, c: (0, c))] + [small((l, l))] * 4,
        out_specs=[pl.BlockSpec((None, l, tc), lambda o, c: (o, 0, c))] * 3,
        out_shape=[out, out, out],
        compiler_params=_params("arbitrary", "arbitrary"),
        name="hyena_filters",
    )(z, t, _pad_to(w1, (LANE, hp)), _pad_to(b1[None, :], (1, hp)), _pad_to(w2, (hp, hp)),
      _pad_to(b2[None, :], (1, hp)), _pad_to(freq, (2, hp)), _pad_to(w3, (hp, w3.shape[1])),
      _pad_to(w3, (hp, w3.shape[1])), deltas, *_split_bf16(fwd_re), *_split_bf16(fwd_im))


HYENA_ROW_CHUNK = 256


def _hyena_kernel(pv_ref, p1_ref, p2_ref, wv_ref, w1_ref, w2_ref, bv_ref, b1_ref, b2_ref,
                  fre_ref, fim_ref, gre_ref, gim_ref, kr_ref, ki_ref, kn_ref, fbias_ref, o_ref, *, l):
    tc = o_ref.shape[1]
    row = lax.broadcasted_iota(jnp.int32, (l, tc), 0)

    def short_conv(p, w_ref, b_ref):
        prev = jnp.where(row == 0, 0.0, pltpu.roll(p, 1, 0))
        nxt = jnp.where(row == l - 1, 0.0, pltpu.roll(p, l - 1, 0))
        w = w_ref[...]
        out = b_ref[...] + prev * w[0:1]
        out = out + p * w[1:2]
        return out + nxt * w[2:3]

    chunks = [slice(r, r + HYENA_ROW_CHUNK) for r in range(0, l, HYENA_ROW_CHUNK)] if l > HYENA_ROW_CHUNK \
        else [slice(0, l)]

    def gated_long_conv(x, u, o, bias):
        ub = u.astype(BF16)
        w_re, w_im = [], []
        for rs in chunks:
            u_re = _dot(fre_ref[rs, :], ub)
            u_im = _dot(fim_ref[rs, :], ub)
            kr, ki, kn = kr_ref[o, rs, :], ki_ref[o, rs, :], kn_ref[o, rs, :]
            w_re.append((u_re * kr - u_im * ki).astype(BF16))
            w_im.append((u_re * ki + u_im * kn).astype(BF16))
        w_re = jnp.concatenate(w_re, axis=0)
        w_im = jnp.concatenate(w_im, axis=0)
        out = []
        for rs in chunks:
            y = _dot(gre_ref[rs, :], w_re) + _dot(gim_ref[rs, :], w_im)
            out.append(x[rs] * (y + u[rs] * bias))
        return jnp.concatenate(out, axis=0)

    fbias = fbias_ref[...]
    for s in range(o_ref.shape[0] // l):
        rs = slice(s * l, (s + 1) * l)
        v = short_conv(pv_ref[rs, :], wv_ref, bv_ref)
        z = gated_long_conv(short_conv(p1_ref[rs, :], w1_ref, b1_ref), v, 0, fbias[0:1])
        z = gated_long_conv(short_conv(p2_ref[rs, :], w2_ref, b2_ref), z, 1, fbias[1:2])
        o_ref[rs, :] = z.astype(o_ref.dtype)


def _hyena(st, proj, e, w_short, b_short, mats, filters, fbias):
    l = st.l
    width = proj.shape[1] // 3
    tc = _tile(width, 1024 if l <= 256 else 256)
    per_step = 2 if l > 256 and st.n_seq % 2 == 0 else 1
    nc = width // tc
    rows = per_step * l
    once = pl.Buffered(1)
    p_specs = [pl.BlockSpec((rows, tc), lambda c, b, part=part: (b, part * nc + c)) for part in range(3)]
    w_specs = [pl.BlockSpec((None, SHORT_CONV, tc), lambda c, b, part=part: (e, 0, part * nc + c))
               for part in range(3)]
    b_specs = [pl.BlockSpec((None, 1, tc), lambda c, b, part=part: (e, 0, part * nc + c)) for part in range(3)]
    mat_specs = [pl.BlockSpec((l, l), lambda c, b: (0, 0), pipeline_mode=once)] * 4
    k_specs = [pl.BlockSpec((HYENA_ORDER, l, tc), lambda c, b: (0, 0, c), pipeline_mode=once)] * 3
    return pl.pallas_call(
        functools.partial(_hyena_kernel, l=l),
        grid=(nc, st.n_seq // per_step),
        in_specs=p_specs + w_specs + b_specs + mat_specs + k_specs
        + [pl.BlockSpec((None, HYENA_ORDER, tc), lambda c, b: (e, 0, c))],
        out_specs=pl.BlockSpec((rows, tc), lambda c, b: (b, c)),
        out_shape=jax.ShapeDtypeStruct((st.n, width), BF16),
        compiler_params=_params("parallel", "arbitrary"),
        name="hyena",
    )(proj, proj, proj, w_short, w_short, w_short, *([b_short.reshape(b_short.shape[0], 1, -1)] * 3),
      *[a.astype(BF16) for a in mats], *filters, fbias)


FNET_ROW_CHUNK = 256


def _fnet_mix_kernel(cos_l_ref, sin_l_ref, cos_c_ref, sin_c_ref, h_ref, o_ref, *, gw):
    l, tn = h_ref.shape
    cos_c = cos_c_ref[...]
    sin_c = sin_c_ref[...]
    for g in range(tn // gw):
        cs = slice(g * gw, (g + 1) * gw)
        h = h_ref[:, cs]
        for r in range(0, l, FNET_ROW_CHUNK):
            rs = slice(r, min(r + FNET_ROW_CHUNK, l))
            a = _dot(cos_l_ref[rs, :], h).astype(BF16)
            b = _dot(sin_l_ref[rs, :], h).astype(BF16)
            o_ref[rs, cs] = (_dot(a, cos_c) - _dot(b, sin_c)).astype(o_ref.dtype)


def _fnet_mix(st, h):
    l = st.l
    d = h.shape[1]
    gw = d // FNET_GROUPS
    groups_per_step = min(FNET_GROUPS, max(1, 2048 // gw)) if l <= 256 else 1
    tn = gw * groups_per_step
    cos_l, sin_l = _cos_sin(l, l, l)
    cos_c, sin_c = _cos_sin(gw, gw, gw)
    mats = [(cos_l * l ** -0.5).astype(BF16), (sin_l * l ** -0.5).astype(BF16),
            (cos_c * gw ** -0.5).astype(BF16), (sin_c * gw ** -0.5).astype(BF16)]
    once = pl.Buffered(1)
    return pl.pallas_call(
        functools.partial(_fnet_mix_kernel, gw=gw),
        grid=(st.n_seq, d // tn),
        in_specs=[pl.BlockSpec((l, l), lambda b, j: (0, 0), pipeline_mode=once)] * 2
        + [pl.BlockSpec((gw, gw), lambda b, j: (0, 0), pipeline_mode=once)] * 2
        + [pl.BlockSpec((l, tn), lambda b, j: (b, j))],
        out_specs=pl.BlockSpec((l, tn), lambda b, j: (b, j)),
        out_shape=jax.ShapeDtypeStruct((st.n, d), BF16),
        compiler_params=_params("parallel", "arbitrary"),
        name="fnet_mix",
    )(*mats, h)


def kernel(x_prompt, x_sample, cache_k, cache_v, c, c_ctx, w_mod, b_mod, norm_g, ffn_w1, ffn_w3, ffn_w2,
           w_in, w_out, rpb, w_short, b_short, filt_w1, filt_b1, filt_w2, filt_b2, filt_w3, filt_freq,
           filt_bias, w_fnet, final_g):
    b_p, l_p, d = x_prompt.shape
    b_s, l_s, _ = x_sample.shape
    depth = w_mod.shape[0]
    attn_w = cache_k.shape[-2] * cache_k.shape[-1]
    hyena_w = filt_bias.shape[-1]
    n_heads = attn_w // HEAD_DIM
    past = cache_k.shape[2]
    assert 1 + b_s <= COND_ROWS

    streams = (_Stream(b_p, l_p, 0, True), _Stream(b_s, l_s, 1, False))
    xs = [x_prompt.reshape(b_p * l_p, d), x_sample.reshape(b_s * l_s, d)]
    conds = jnp.concatenate([c_ctx[None, :], c, jnp.zeros((COND_ROWS - 1 - b_s, d), F32)], axis=0)
    w_in_b, w_out_b, w_fnet_b = w_in.astype(BF16), w_out.astype(BF16), w_fnet.astype(BF16)
    rfft_mats = {st.l: _rfft_matrices(st.l) for st in streams}

    ride_along = _ffn_cast_ok(streams[0], d, ffn_w1.shape[-1])
    ffn_w = {(0, 0): (ffn_w1[0, 0].astype(BF16), ffn_w3[0, 0].astype(BF16), ffn_w2[0, 0].astype(BF16))}

    def ffn_half_step(xs, mods, l, s):
        nxt = (l, 1) if s == 0 else (l + 1, 0)
        w = ffn_w[(l, s)]
        g, mi = norm_g[l, 2 * s], 6 * s
        if nxt[0] == depth:
            return [_ffn(st, x, mods, g, final_g, *w, mi, final_norm=True) for st, x in zip(streams, xs)]
        if ride_along:
            x0, ffn_w[nxt] = _ffn(streams[0], xs[0], mods, g, final_g, *w, mi,
                                  cast_next=(ffn_w1, ffn_w3, ffn_w2, *nxt))
        else:
            x0 = _ffn(streams[0], xs[0], mods, g, final_g, *w, mi)
            ffn_w[nxt] = tuple(a[nxt].astype(BF16) for a in (ffn_w1, ffn_w3, ffn_w2))
        return [x0, _ffn(streams[1], xs[1], mods, g, final_g, *w, mi)]

    new_k, new_v = [], []
    for l in range(depth):
        mods = _adaln(conds, w_mod, b_mod, l)
        xs = ffn_half_step(xs, mods, l, 0)
        if l % 2 == 0:
            e = l // 2
            kv_dtypes = (F32, BF16)
            qkvh = [_norm_mm(st, x, mods, norm_g[l, 1], w_in_b, e, 3,
                             ((attn_w, BF16), (attn_w, kv), (attn_w, kv), (3 * hyena_w, F32)))
                    for st, x, kv in zip(streams, xs, kv_dtypes)]
            attn = [_ctx_attn(streams[0], *qkvh[0][:3]),
                    _na_attn(streams[1], *qkvh[1][:3], cache_k[:, e].reshape(b_s * past, attn_w),
                             cache_v[:, e].reshape(b_s * past, attn_w), rpb[e])]
            for i, st in enumerate(streams):
                mats = rfft_mats[st.l]
                filters = _hyena_filters(st.l, hyena_w, filt_w1[e], filt_b1[e], filt_w2[e], filt_b2[e],
                                         filt_w3[e], filt_freq[e], mats[0], mats[1])
                hy = _hyena(st, qkvh[i][3], e, w_short, b_short, mats, filters, filt_bias)
                xs[i] = _mm_res(st, attn[i], hy, w_out_b, e, xs[i], mods, 5, "out_proj")
            new_k.append(qkvh[0][1].reshape(b_p, l_p, n_heads, HEAD_DIM))
            new_v.append(qkvh[0][2].reshape(b_p, l_p, n_heads, HEAD_DIM))
        else:
            o = l // 2
            for i, st in enumerate(streams):
                mixed = _fnet_mix(st, _mod_norm(st, xs[i], mods, norm_g[l, 1], 3))
                xs[i] = _mm_res(st, mixed, None, w_fnet_b, o, xs[i], mods, 5, "fnet_out")
        xs = ffn_half_step(xs, mods, l, 1)

    y_prompt = xs[0].reshape(b_p, l_p, d)
    y_sample = xs[1].reshape(b_s, l_s, d)
    return y_prompt, y_sample, jnp.stack(new_k, axis=1), jnp.stack(new_v, axis=1)
```

```python
import dataclasses
import functools
import math

import jax
import jax.numpy as jnp
from jax import lax
from jax.experimental import pallas as pl
from jax.experimental.pallas import tpu as pltpu

F32 = jnp.float32
BF16 = jnp.bfloat16
HIGHEST = lax.Precision.HIGHEST

N_MOD = 9
GRID_W = 64
WIN_H = 8
WIN_W = 16
HEAD_DIM = 128
SHORT_CONV = 3
FILTER_EMB = 33
HYENA_ORDER = 2
DECAY_TARGET = 1e-2
FAST_DECAY_PCT = 0.3
SLOW_DECAY_PCT = 1.5
FNET_GROUPS = 8
EPS = 1e-6
NEG_INF = -1e30

V7X_VMEM_BYTES = 64 * 2**20
VMEM_LIMIT = V7X_VMEM_BYTES - 8 * 2**20
LANE = 128
BF16_SUBLANES = 16
COND_ROWS = 16


def _tile(n, pref, unit=LANE):
    if n <= pref:
        return n
    t = (pref // unit) * unit
    while t > unit and n % t:
        t -= unit
    assert n % t == 0, (n, pref)
    return t


def _params(*sem):
    return pltpu.CompilerParams(dimension_semantics=sem, vmem_limit_bytes=VMEM_LIMIT)


@dataclasses.dataclass(frozen=True)
class _Stream:
    n_seq: int
    l: int
    cond0: int
    shared: bool

    @property
    def n(self):
        return self.n_seq * self.l

    def row_tile(self, pref):
        return _tile(self.n if self.shared else self.l, pref, BF16_SUBLANES)

    def cond(self, i, tm):
        return self.cond0 if self.shared else self.cond0 + i // (self.l // tm)


def _fill_mod_norm(x_ref, g_ref, mod_ref, h_ref, rs_ref, mi, zero_ref=None):
    rows = BF16_SUBLANES
    tm, d = x_ref.shape
    cw = _tile(d, 512)

    def stats(c, carry):
        r = pl.multiple_of(c * rows, rows)
        x = x_ref[pl.ds(r, rows), :]
        rs = lax.rsqrt(jnp.mean(x * x, axis=-1, keepdims=True) + EPS)
        rs_ref[pl.ds(r, rows), :] = jnp.broadcast_to(rs, (rows, LANE))
        return carry

    lax.fori_loop(0, tm // rows, stats, 0, unroll=4)

    m = mod_ref[0]
    for t in range(d // cw):
        cs = slice(t * cw, (t + 1) * cw)
        gain = g_ref[:, cs] * (1.0 + m[mi + 1:mi + 2, cs])
        shift = m[mi:mi + 1, cs]

        def apply(c, carry):
            r = pl.multiple_of(c * rows, rows)
            rs = rs_ref[pl.ds(r, rows), :]
            rs = jnp.concatenate([rs] * (cw // LANE), axis=1) if cw > LANE else rs[:, :cw]
            h_ref[pl.ds(r, rows), cs] = (x_ref[pl.ds(r, rows), cs] * rs * gain + shift).astype(h_ref.dtype)
            if zero_ref is not None:
                zero_ref[pl.ds(r, rows), cs] = jnp.zeros((rows, cw), zero_ref.dtype)
            return carry

        lax.fori_loop(0, tm // rows, apply, 0)


def _dot(a, b):
    return jnp.dot(a, b, preferred_element_type=F32)


def _dot_hp(a, b):
    return jnp.dot(a, b, preferred_element_type=F32, precision=HIGHEST)


def _dot_nt(a, b):
    return lax.dot_general(a, b, (((1,), (1,)), ((), ())), preferred_element_type=F32)


def _adaln_kernel(c_ref, w_ref, b_ref, o_ref):
    c = c_ref[...]
    s = c * (1.0 / (1.0 + jnp.exp(-c)))
    s_hi = s.astype(BF16)
    s_lo = (s - s_hi.astype(F32)).astype(BF16)
    w = w_ref[...].astype(BF16)
    o_ref[...] = _dot(s_hi, w) + _dot(s_lo, w) + b_ref[...]


def _adaln(conds, w_mod, b_mod, l):
    d = conds.shape[1]
    n = w_mod.shape[-1]
    tn = _tile(n, 1024)
    out = pl.pallas_call(
        _adaln_kernel,
        grid=(n // tn,),
        in_specs=[
            pl.BlockSpec((COND_ROWS, d), lambda j: (0, 0)),
            pl.BlockSpec((None, d, tn), lambda j: (l, 0, j)),
            pl.BlockSpec((None, 1, tn), lambda j: (l, 0, j)),
        ],
        out_specs=pl.BlockSpec((COND_ROWS, tn), lambda j: (0, j)),
        out_shape=jax.ShapeDtypeStruct((COND_ROWS, n), F32),
        compiler_params=_params("arbitrary"),
        name="adaln",
    )(conds, w_mod, b_mod.reshape(b_mod.shape[0], 1, n))
    return out.reshape(COND_ROWS, N_MOD, d)


def _ffn_kernel(x_ref, mod_ref, g_ref, fin_ref, w1_ref, w3_ref, w2_ref, *rest, mi, final_norm):
    j = pl.program_id(1)
    n_cast = (len(rest) - 3) // 2
    f32_tiles, o_ref, bf16_tiles = rest[:n_cast], rest[n_cast], rest[n_cast + 1:2 * n_cast + 1]
    h_ref, rs_ref = rest[2 * n_cast + 1:]

    @pl.when(j == 0)
    def _():
        _fill_mod_norm(x_ref, g_ref, mod_ref, h_ref, rs_ref, mi, zero_ref=o_ref)

    for src, dst in zip(f32_tiles, bf16_tiles):
        dst[...] = src[...].astype(dst.dtype)

    h = h_ref[...]
    a = _dot(h, w1_ref[...])
    b = _dot(h, w3_ref[...])
    act = (a * (1.0 / (1.0 + jnp.exp(-a))) * b).astype(BF16)
    o_ref[...] += _dot(act, w2_ref[...])

    @pl.when(j == pl.num_programs(1) - 1)
    def _():
        rows = BF16_SUBLANES
        tm, d = o_ref.shape
        half_gate = 0.5 * mod_ref[0][mi + 2:mi + 3]

        def body(c, carry):
            r = pl.multiple_of(c * rows, rows)
            y = x_ref[pl.ds(r, rows), :] + half_gate * o_ref[pl.ds(r, rows), :]
            o_ref[pl.ds(r, rows), :] = y
            if final_norm:
                rs = lax.rsqrt(jnp.mean(y * y, axis=-1, keepdims=True) + EPS)
                rs_ref[pl.ds(r, rows), :] = jnp.broadcast_to(rs, (rows, LANE))
            return carry

        lax.fori_loop(0, tm // rows, body, 0, unroll=4 if final_norm else 1)

        if final_norm:
            cw = _tile(d, 512)
            for t in range(d // cw):
                cs = slice(t * cw, (t + 1) * cw)
                gain = fin_ref[:, cs]

                def scale(c, carry):
                    r = pl.multiple_of(c * rows, rows)
                    rs = jnp.concatenate([rs_ref[pl.ds(r, rows), :]] * (cw // LANE), axis=1)
                    o_ref[pl.ds(r, rows), cs] = o_ref[pl.ds(r, rows), cs] * rs * gain
                    return carry

                lax.fori_loop(0, tm // rows, scale, 0)


FFN_TF = 256


def _ffn_grid(st, f):
    tm = st.row_tile(512)
    return st.n // tm, f // _tile(f, FFN_TF)


def _cast_tiling(grid, src, lead, rows_on_i=True):
    nm, nf = grid
    r, c = src.shape[-2:]
    ni_len, nj_len = (r, c) if rows_on_i else (c, r)
    unit_i, unit_j = (BF16_SUBLANES, LANE) if rows_on_i else (LANE, BF16_SUBLANES)
    nj = next((n for n in range(nf, 0, -1) if nj_len % n == 0 and (nj_len // n) % unit_j == 0), None)
    if ni_len % nm or (ni_len // nm) % unit_i or nj is None:
        return None
    ti, tj = ni_len // nm, nj_len // nj
    pick = (lambda i, j: (i, jnp.minimum(j, nj - 1))) if rows_on_i else (lambda i, j: (jnp.minimum(j, nj - 1), i))
    block = (ti, tj) if rows_on_i else (tj, ti)
    return (pl.BlockSpec((None,) * len(lead) + block, lambda i, j: tuple(lead) + pick(i, j)),
            pl.BlockSpec(block, pick), jax.ShapeDtypeStruct((r, c), BF16))


def _ffn(st, x, mods, g, final_g, w1, w3, w2, mi, final_norm=False, casts=()):
    m, d = x.shape
    f = w1.shape[-1]
    tm = st.row_tile(512)
    tf = _tile(f, FFN_TF)
    nm, nf = _ffn_grid(st, f)
    in_specs = [
        pl.BlockSpec((tm, d), lambda i, j: (i, 0)),
        pl.BlockSpec((1, N_MOD, d), lambda i, j: (st.cond(i, tm), 0, 0)),
        pl.BlockSpec((1, d), lambda i, j: (0, 0)),
        pl.BlockSpec((1, d), lambda i, j: (0, 0)),
        pl.BlockSpec((d, tf), lambda i, j: (0, j)),
        pl.BlockSpec((d, tf), lambda i, j: (0, j)),
        pl.BlockSpec((tf, d), lambda i, j: (j, 0)),
    ]
    out_specs = [pl.BlockSpec((tm, d), lambda i, j: (i, 0))]
    out_shape = [jax.ShapeDtypeStruct((m, d), F32)]
    args = [x, mods, g.reshape(1, d), final_g.reshape(1, d), w1, w3, w2]
    for src, (src_spec, dst_spec, dst_shape) in casts:
        in_specs.append(src_spec)
        out_specs.append(dst_spec)
        out_shape.append(dst_shape)
        args.append(src)
    out = pl.pallas_call(
        functools.partial(_ffn_kernel, mi=mi, final_norm=final_norm),
        grid=(nm, nf),
        in_specs=in_specs,
        out_specs=out_specs,
        out_shape=out_shape,
        scratch_shapes=[pltpu.VMEM((tm, d), BF16), pltpu.VMEM((tm, LANE), F32)],
        compiler_params=_params("parallel", "arbitrary"),
        name="ffn",
    )(*args)
    return out[0], list(out[1:])


def _norm_mm_kernel(x_ref, mod_ref, g_ref, w_ref, *rest, mi, tile_ranges):
    outs, (h_ref, rs_ref) = rest[:-2], rest[-2:]
    j = pl.program_id(1)

    @pl.when(j == 0)
    def _():
        _fill_mod_norm(x_ref, g_ref, mod_ref, h_ref, rs_ref, mi)

    for o_ref, (lo, hi) in zip(outs, tile_ranges):
        @pl.when((j >= lo) & (j < hi))
        def _(o_ref=o_ref):
            o_ref[...] = _dot(h_ref[...], w_ref[...]).astype(o_ref.dtype)


def _norm_mm(st, x, mods, g, w, e, mi, splits):
    m, d = x.shape
    n = w.shape[-1]
    tm = st.row_tile(512)
    tn = _tile(math.gcd(*[width for width, _ in splits]), 1024)
    assert sum(width for width, _ in splits) == n
    tile_ranges, out_specs, lo = [], [], 0
    for width, _ in splits:
        hi = lo + width // tn
        tile_ranges.append((lo, hi))
        out_specs.append(pl.BlockSpec((tm, tn), lambda i, j, lo=lo, hi=hi: (i, jnp.clip(j - lo, 0, hi - lo - 1))))
        lo = hi
    return pl.pallas_call(
        functools.partial(_norm_mm_kernel, mi=mi, tile_ranges=tuple(tile_ranges)),
        grid=(m // tm, n // tn),
        in_specs=[
            pl.BlockSpec((tm, d), lambda i, j: (i, 0)),
            pl.BlockSpec((1, N_MOD, d), lambda i, j: (st.cond(i, tm), 0, 0)),
            pl.BlockSpec((1, d), lambda i, j: (0, 0)),
            pl.BlockSpec((None, d, tn), lambda i, j: (e, 0, j)),
        ],
        out_specs=out_specs,
        out_shape=[jax.ShapeDtypeStruct((m, width), dtype) for width, dtype in splits],
        scratch_shapes=[pltpu.VMEM((tm, d), BF16), pltpu.VMEM((tm, LANE), F32)],
        compiler_params=_params("parallel", "arbitrary"),
        name="in_proj",
    )(x, mods, g.reshape(1, d), w)


def _mod_norm_kernel(x_ref, mod_ref, g_ref, h_ref, rs_ref, *, mi):
    _fill_mod_norm(x_ref, g_ref, mod_ref, h_ref, rs_ref, mi)


def _mod_norm(st, x, mods, g, mi):
    m, d = x.shape
    tm = st.row_tile(256)
    return pl.pallas_call(
        functools.partial(_mod_norm_kernel, mi=mi),
        grid=(m // tm,),
        in_specs=[
            pl.BlockSpec((tm, d), lambda i: (i, 0)),
            pl.BlockSpec((1, N_MOD, d), lambda i: (st.cond(i, tm), 0, 0)),
            pl.BlockSpec((1, d), lambda i: (0, 0)),
        ],
        out_specs=pl.BlockSpec((tm, d), lambda i: (i, 0)),
        out_shape=jax.ShapeDtypeStruct((m, d), BF16),
        scratch_shapes=[pltpu.VMEM((tm, LANE), F32)],
        compiler_params=_params("parallel"),
        name="mod_norm",
    )(x, mods, g.reshape(1, d))


def _mm_res_kernel(a0_ref, a1_ref, w0_ref, w1_ref, res_ref, mod_ref, o_ref, *, gi):
    mix = _dot(a0_ref[...], w0_ref[...]) + _dot(a1_ref[...], w1_ref[...])
    o_ref[...] = res_ref[...] + mod_ref[0][gi:gi + 1] * mix


def _mm_res(st, a0, a1, w, e, res, mods, gi, name):
    k = w.shape[1] // 2
    m = a0.shape[0]
    c1 = 0
    if a1 is None:
        a1, c1 = a0, 1
    assert a0.shape[1] >= k and a1.shape == (m, (c1 + 1) * k)
    n = w.shape[-1]
    tm = st.row_tile(1024)
    tn = _tile(n, 1024)
    return pl.pallas_call(
        functools.partial(_mm_res_kernel, gi=gi),
        grid=(m // tm, n // tn),
        in_specs=[
            pl.BlockSpec((tm, k), lambda i, j: (i, 0)),
            pl.BlockSpec((tm, k), lambda i, j: (i, c1)),
            pl.BlockSpec((None, k, tn), lambda i, j: (e, 0, j)),
            pl.BlockSpec((None, k, tn), lambda i, j: (e, 1, j)),
            pl.BlockSpec((tm, tn), lambda i, j: (i, j)),
            pl.BlockSpec((1, N_MOD, tn), lambda i, j: (st.cond(i, tm), 0, j)),
        ],
        out_specs=pl.BlockSpec((tm, tn), lambda i, j: (i, j)),
        out_shape=jax.ShapeDtypeStruct((m, n), F32),
        compiler_params=_params("parallel", "arbitrary"),
        name=name,
    )(a0, a1, w, w, res, mods)


def _ctx_attn_kernel(q_ref, k_ref, v_ref, o_ref, *, n_heads, scale):
    for h in range(n_heads):
        sl = slice(h * HEAD_DIM, (h + 1) * HEAD_DIM)
        q = q_ref[:, sl].astype(BF16)
        k = k_ref[:, sl].astype(BF16)
        v = v_ref[:, sl].astype(BF16)
        s = _dot_nt(q, k) * scale
        e = jnp.exp(s - jnp.max(s, axis=-1, keepdims=True))
        p = e * (1.0 / jnp.sum(e, axis=-1, keepdims=True))
        o_ref[:, sl] = _dot(p.astype(BF16), v).astype(o_ref.dtype)


def _ctx_attn(st, q, k, v):
    l = st.l
    width = q.shape[1]
    return pl.pallas_call(
        functools.partial(_ctx_attn_kernel, n_heads=width // HEAD_DIM, scale=HEAD_DIM ** -0.5),
        grid=(st.n_seq,),
        in_specs=[pl.BlockSpec((l, width), lambda b: (b, 0))] * 3,
        out_specs=pl.BlockSpec((l, width), lambda b: (b, 0)),
        out_shape=jax.ShapeDtypeStruct((st.n, width), BF16),
        compiler_params=_params("parallel"),
        name="ctx_attn",
    )(q, k, v)


def _na_window(r, rows, kh):
    start = min(max(r - kh // 2, 0), rows - kh)
    lo = start - start % 2
    hi = start + kh + (start + kh) % 2
    return start, lo, hi


NA_ROW_GROUP = 4


def _na_attn_kernel(q_ref, k_ref, v_ref, ck_ref, cv_ref, tab_ref, o_ref, *, rows, kh, scale):
    ck = ck_ref[...].astype(BF16)
    cv = cv_ref[...].astype(BF16)
    for g0 in range(0, rows, NA_ROW_GROUP):
        group = range(g0, min(g0 + NA_ROW_GROUP, rows))
        windows = [_na_window(r, rows, kh) for r in group]
        glo = min(w[1] for w in windows) * GRID_W
        ghi = max(w[2] for w in windows) * GRID_W
        q = q_ref[g0 * GRID_W:(g0 + len(group)) * GRID_W, :].astype(BF16)
        s_lat_g = _dot_nt(q, k_ref[glo:ghi, :].astype(BF16))
        s_ctx_g = _dot_nt(q, ck)
        p_lat_g, p_ctx_g = [], []
        for i, (r, (start, lo, hi)) in enumerate(zip(group, windows)):
            qs = slice(i * GRID_W, (i + 1) * GRID_W)
            pieces = []
            for p in range(lo // 2, hi // 2):
                ok0 = start <= 2 * p < start + kh
                ok1 = start <= 2 * p + 1 < start + kh
                dr = 2 * p - r + WIN_H - 1
                pieces.append(tab_ref[0, 0, dr] if ok0 and ok1 else tab_ref[0, 1, dr] if ok0
                              else tab_ref[0, 2, dr + 1])
            lo, hi = lo * GRID_W, hi * GRID_W
            s_lat = s_lat_g[qs, lo - glo:hi - glo] * scale + jnp.concatenate(pieces, axis=1)
            s_ctx = s_ctx_g[qs, :] * scale
            mx = jnp.maximum(jnp.max(s_lat, axis=-1, keepdims=True), jnp.max(s_ctx, axis=-1, keepdims=True))
            e_lat = jnp.exp(s_lat - mx)
            e_ctx = jnp.exp(s_ctx - mx)
            inv = 1.0 / (jnp.sum(e_lat, axis=-1, keepdims=True) + jnp.sum(e_ctx, axis=-1, keepdims=True))
            parts = [(e_lat * inv).astype(BF16)]
            if lo > glo:
                parts.insert(0, jnp.zeros((GRID_W, lo - glo), BF16))
            if hi < ghi:
                parts.append(jnp.zeros((GRID_W, ghi - hi), BF16))
            p_lat_g.append(jnp.concatenate(parts, axis=1) if len(parts) > 1 else parts[0])
            p_ctx_g.append((e_ctx * inv).astype(BF16))
        o = (_dot(jnp.concatenate(p_lat_g, axis=0), v_ref[glo:ghi, :].astype(BF16))
             + _dot(jnp.concatenate(p_ctx_g, axis=0), cv))
        o_ref[g0 * GRID_W:(g0 + len(group)) * GRID_W, :] = o.astype(o_ref.dtype)


def _na_bias_table(rpb):
    col = jnp.arange(GRID_W)
    col_start = jnp.clip(col - WIN_W // 2, 0, GRID_W - WIN_W)
    col_ok = (col[None, :] >= col_start[:, None]) & (col[None, :] < col_start[:, None] + WIN_W)
    dc_idx = jnp.clip(col[None, :] - col[:, None] + WIN_W - 1, 0, 2 * WIN_W - 2)
    t = jnp.where(col_ok, rpb.astype(F32)[:, :, dc_idx], NEG_INF)
    neg = jnp.full_like(t, NEG_INF)
    nxt = jnp.concatenate([t[:, 1:], neg[:, :1]], axis=1)
    return jnp.stack([jnp.concatenate([t, nxt], axis=-1), jnp.concatenate([t, neg], axis=-1),
                      jnp.concatenate([neg, t], axis=-1)], axis=1)


def _na_attn(st, q, k, v, cache_k, cache_v, rpb):
    l = st.l
    width = q.shape[1]
    n_heads = width // HEAD_DIM
    rows = l // GRID_W
    kh = min(WIN_H, rows)
    assert rows % 2 == 0 and kh % 2 == 0
    past = cache_k.shape[0] // st.n_seq
    tab = _na_bias_table(rpb)
    return pl.pallas_call(
        functools.partial(_na_attn_kernel, rows=rows, kh=kh, scale=HEAD_DIM ** -0.5),
        grid=(st.n_seq, n_heads),
        in_specs=[pl.BlockSpec((l, HEAD_DIM), lambda b, h: (b, h))] * 3 + [
            pl.BlockSpec((past, HEAD_DIM), lambda b, h: (b, h)),
            pl.BlockSpec((past, HEAD_DIM), lambda b, h: (b, h)),
            pl.BlockSpec((1,) + tab.shape[1:], lambda b, h: (h, 0, 0, 0, 0)),
        ],
        out_specs=pl.BlockSpec((l, HEAD_DIM), lambda b, h: (b, h)),
        out_shape=jax.ShapeDtypeStruct((st.n, width), BF16),
        compiler_params=_params("parallel", "arbitrary"),
        name="na_attn",
    )(q, k, v, cache_k, cache_v, tab)


def _cos_sin(n_rows, n_cols, period):
    r = lax.broadcasted_iota(jnp.int32, (n_rows, n_cols), 0)
    c = lax.broadcasted_iota(jnp.int32, (n_rows, n_cols), 1)
    ang = ((r * c) % period).astype(F32) * (2.0 * math.pi / period)
    return jnp.cos(ang), jnp.sin(ang)


def _rfft_matrices(l):
    n = 2 * l
    cos, sin = _cos_sin(l, l, n)
    alt = jnp.where(lax.broadcasted_iota(jnp.int32, (l, l), 1) % 2 == 0, 1.0, -1.0).astype(F32)
    k_is0 = lax.broadcasted_iota(jnp.int32, (l, l), 0) == 0
    fwd_re = cos
    fwd_im = jnp.where(k_is0, alt, -sin)
    bin_is0 = lax.broadcasted_iota(jnp.int32, (l, l), 1) == 0
    inv_re = jnp.where(bin_is0, 1.0 / n, 2.0 / n) * cos
    inv_im = jnp.where(bin_is0, alt.T / n, (-2.0 / n) * sin)
    return fwd_re, fwd_im, inv_re, inv_im


def _split_bf16(a):
    hi = a.astype(BF16)
    return hi, (a - hi.astype(F32)).astype(BF16)


def _dot_3pass(a_hi_ref, a_lo_ref, b):
    b_hi, b_lo = _split_bf16(b)
    a_hi = a_hi_ref[...]
    return _dot(a_hi, b_hi) + (_dot(a_hi, b_lo) + _dot(a_lo_ref[...], b_hi))


def _filter_kernel(z_ref, t_ref, w1_ref, b1_ref, w2_ref, b2_ref, fq_ref, w3f_ref, w3b_ref, dl_ref,
                   fre_hi_ref, fre_lo_ref, fim_hi_ref, fim_lo_ref, kr_ref, ki_ref, kn_ref):
    fq = fq_ref[...]
    hid = jnp.sin(fq[0:1] * (_dot_hp(z_ref[...], w1_ref[...]) + b1_ref[...]))
    hid = jnp.sin(fq[1:2] * (_dot_hp(hid, w2_ref[...]) + b2_ref[...]))
    decay = jnp.exp(-t_ref[...] * dl_ref[...])
    row = lax.broadcasted_iota(jnp.int32, kr_ref.shape, 0)
    h_fwd = _dot_hp(hid, w3f_ref[...]) * decay
    h_bwd = jnp.where(row == 0, 0.0, _dot_hp(hid, w3b_ref[...]) * decay)
    nrm = lax.rsqrt(jnp.sum(h_fwd * h_fwd + h_bwd * h_bwd, axis=0, keepdims=True) + EPS)
    even = (h_fwd + h_bwd) * nrm
    odd = (h_fwd - h_bwd) * nrm
    k_re = _dot_3pass(fre_hi_ref, fre_lo_ref, even)
    k_im = _dot_3pass(fim_hi_ref, fim_lo_ref, odd)
    nyq = jnp.sum(jnp.where(row % 2 == 0, even, -even), axis=0, keepdims=True)
    kr_ref[...] = k_re
    ki_ref[...] = jnp.where(row == 0, 0.0, k_im)
    kn_ref[...] = jnp.where(row == 0, nyq, k_re)


def _pad_to(a, shape):
    return jnp.pad(a.astype(F32), [(0, s - n) for n, s in zip(a.shape, shape)])


def _hyena_filters(l, width, w1, b1, w2, b2, w3, freq, fwd_re, fwd_im):
    hid = w2.shape[0]
    hp = LANE * pl.cdiv(hid, LANE)
    n_bands = (FILTER_EMB - 1) // 2
    t = jnp.linspace(0.0, 1.0, l, dtype=F32)[:, None]
    w = (2.0 * math.pi / l) * jnp.arange(l, dtype=F32)[:, None]
    bands = jnp.linspace(1e-4, n_bands - 1, n_bands, dtype=F32)[None, :]
    z = _pad_to(jnp.concatenate([t, jnp.cos(bands * w), -jnp.sin(bands * w)], axis=-1), (l, LANE))
    deltas = jnp.abs(jnp.linspace(math.log(DECAY_TARGET) / FAST_DECAY_PCT,
                                  math.log(DECAY_TARGET) / SLOW_DECAY_PCT, width, dtype=F32))[None, :]
    tc = _tile(width, 512)
    nc = width // tc
    small = lambda shape: pl.BlockSpec(shape, lambda o, c: (0,) * len(shape))
    out = jax.ShapeDtypeStruct((HYENA_ORDER, l, width), F32)
    return pl.pallas_call(
        _filter_kernel,
        grid=(HYENA_ORDER, nc),
        in_specs=[small((l, LANE)), small((l, 1)), small((LANE, hp)), small((1, hp)), small((hp, hp)),
                  small((1, hp)), small((2, hp)),
                  pl.BlockSpec((hp, tc), lambda o, c: (0, 2 * o * nc + c)),
                  pl.BlockSpec((hp, tc), lambda o, c: (0, (2 * o + 1) * nc + c)),
                  pl.BlockSpec((1, tc), lambda o, c: (0, c))] + [small((l, l))] * 4,
        out_specs=[pl.BlockSpec((None, l, tc), lambda o, c: (o, 0, c))] * 3,
        out_shape=[out, out, out],
        compiler_params=_params("arbitrary", "arbitrary"),
        name="hyena_filters",
    )(z, t, _pad_to(w1, (LANE, hp)), _pad_to(b1[None, :], (1, hp)), _pad_to(w2, (hp, hp)),
      _pad_to(b2[None, :], (1, hp)), _pad_to(freq, (2, hp)), _pad_to(w3, (hp, w3.shape[1])),
      _pad_to(w3, (hp, w3.shape[1])), deltas, *_split_bf16(fwd_re), *_split_bf16(fwd_im))


HYENA_ROW_CHUNK = 256


def _hyena_kernel(pv_ref, p1_ref, p2_ref, wv_ref, w1_ref, w2_ref, bv_ref, b1_ref, b2_ref,
                  fre_ref, fim_ref, gre_ref, gim_ref, kr_ref, ki_ref, kn_ref, fbias_ref, o_ref, *, l):
    tc = o_ref.shape[1]
    row = lax.broadcasted_iota(jnp.int32, (l, tc), 0)

    def short_conv(p, w_ref, b_ref):
        prev = jnp.where(row == 0, 0.0, pltpu.roll(p, 1, 0))
        nxt = jnp.where(row == l - 1, 0.0, pltpu.roll(p, l - 1, 0))
        w = w_ref[...]
        out = b_ref[...] + prev * w[0:1]
        out = out + p * w[1:2]
        return out + nxt * w[2:3]

    chunks = [slice(r, r + HYENA_ROW_CHUNK) for r in range(0, l, HYENA_ROW_CHUNK)] if l > HYENA_ROW_CHUNK \
        else [slice(0, l)]

    def gated_long_conv(x, u, o, bias):
        ub = u.astype(BF16)
        w_re, w_im = [], []
        for rs in chunks:
            u_re = _dot(fre_ref[rs, :], ub)
            u_im = _dot(fim_ref[rs, :], ub)
            kr, ki, kn = kr_ref[o, rs, :], ki_ref[o, rs, :], kn_ref[o, rs, :]
            w_re.append((u_re * kr - u_im * ki).astype(BF16))
            w_im.append((u_re * ki + u_im * kn).astype(BF16))
        w_re = jnp.concatenate(w_re, axis=0)
        w_im = jnp.concatenate(w_im, axis=0)
        out = []
        for rs in chunks:
            y = _dot(gre_ref[rs, :], w_re) + _dot(gim_ref[rs, :], w_im)
            out.append(x[rs] * (y + u[rs] * bias))
        return jnp.concatenate(out, axis=0)

    fbias = fbias_ref[...]
    for s in range(o_ref.shape[0] // l):
        rs = slice(s * l, (s + 1) * l)
        v = short_conv(pv_ref[rs, :], wv_ref, bv_ref)
        z = gated_long_conv(short_conv(p1_ref[rs, :], w1_ref, b1_ref), v, 0, fbias[0:1])
        z = gated_long_conv(short_conv(p2_ref[rs, :], w2_ref, b2_ref), z, 1, fbias[1:2])
        o_ref[rs, :] = z.astype(o_ref.dtype)


def _hyena(st, proj, e, w_short, b_short, mats, filters, fbias):
    l = st.l
    width = proj.shape[1] // 3
    tc = _tile(width, 1024 if l <= 256 else 256)
    per_step = 2 if l > 256 and st.n_seq % 2 == 0 else 1
    nc = width // tc
    rows = per_step * l
    once = pl.Buffered(1)
    p_specs = [pl.BlockSpec((rows, tc), lambda c, b, part=part: (b, part * nc + c)) for part in range(3)]
    w_specs = [pl.BlockSpec((None, SHORT_CONV, tc), lambda c, b, part=part: (e, 0, part * nc + c))
               for part in range(3)]
    b_specs = [pl.BlockSpec((None, 1, tc), lambda c, b, part=part: (e, 0, part * nc + c)) for part in range(3)]
    mat_specs = [pl.BlockSpec((l, l), lambda c, b: (0, 0), pipeline_mode=once)] * 4
    k_specs = [pl.BlockSpec((HYENA_ORDER, l, tc), lambda c, b: (0, 0, c), pipeline_mode=once)] * 3
    return pl.pallas_call(
        functools.partial(_hyena_kernel, l=l),
        grid=(nc, st.n_seq // per_step),
        in_specs=p_specs + w_specs + b_specs + mat_specs + k_specs
        + [pl.BlockSpec((None, HYENA_ORDER, tc), lambda c, b: (e, 0, c))],
        out_specs=pl.BlockSpec((rows, tc), lambda c, b: (b, c)),
        out_shape=jax.ShapeDtypeStruct((st.n, width), BF16),
        compiler_params=_params("parallel", "arbitrary"),
        name="hyena",
    )(proj, proj, proj, w_short, w_short, w_short, *([b_short.reshape(b_short.shape[0], 1, -1)] * 3),
      *[a.astype(BF16) for a in mats], *filters, fbias)


FNET_ROW_CHUNK = 256


def _fnet_mix_kernel(cos_l_ref, sin_l_ref, cos_c_ref, sin_c_ref, h_ref, o_ref, *, gw):
    l, tn = h_ref.shape
    cos_c = cos_c_ref[...]
    sin_c = sin_c_ref[...]
    for g in range(tn // gw):
        cs = slice(g * gw, (g + 1) * gw)
        h = h_ref[:, cs]
        for r in range(0, l, FNET_ROW_CHUNK):
            rs = slice(r, min(r + FNET_ROW_CHUNK, l))
            a = _dot(cos_l_ref[rs, :], h).astype(BF16)
            b = _dot(sin_l_ref[rs, :], h).astype(BF16)
            o_ref[rs, cs] = (_dot(a, cos_c) - _dot(b, sin_c)).astype(o_ref.dtype)


def _fnet_mix(st, h):
    l = st.l
    d = h.shape[1]
    gw = d // FNET_GROUPS
    groups_per_step = min(FNET_GROUPS, max(1, 2048 // gw)) if l <= 256 else 1
    tn = gw * groups_per_step
    cos_l, sin_l = _cos_sin(l, l, l)
    cos_c, sin_c = _cos_sin(gw, gw, gw)
    mats = [(cos_l * l ** -0.5).astype(BF16), (sin_l * l ** -0.5).astype(BF16),
            (cos_c * gw ** -0.5).astype(BF16), (sin_c * gw ** -0.5).astype(BF16)]
    once = pl.Buffered(1)
    return pl.pallas_call(
        functools.partial(_fnet_mix_kernel, gw=gw),
        grid=(st.n_seq, d // tn),
        in_specs=[pl.BlockSpec((l, l), lambda b, j: (0, 0), pipeline_mode=once)] * 2
        + [pl.BlockSpec((gw, gw), lambda b, j: (0, 0), pipeline_mode=once)] * 2
        + [pl.BlockSpec((l, tn), lambda b, j: (b, j))],
        out_specs=pl.BlockSpec((l, tn), lambda b, j: (b, j)),
        out_shape=jax.ShapeDtypeStruct((st.n, d), BF16),
        compiler_params=_params("parallel", "arbitrary"),
        name="fnet_mix",
    )(*mats, h)


def kernel(x_prompt, x_sample, cache_k, cache_v, c, c_ctx, w_mod, b_mod, norm_g, ffn_w1, ffn_w3, ffn_w2,
           w_in, w_out, rpb, w_short, b_short, filt_w1, filt_b1, filt_w2, filt_b2, filt_w3, filt_freq,
           filt_bias, w_fnet, final_g):
    b_p, l_p, d = x_prompt.shape
    b_s, l_s, _ = x_sample.shape
    depth = w_mod.shape[0]
    attn_w = cache_k.shape[-2] * cache_k.shape[-1]
    hyena_w = filt_bias.shape[-1]
    n_heads = attn_w // HEAD_DIM
    past = cache_k.shape[2]
    assert 1 + b_s <= COND_ROWS

    streams = (_Stream(b_p, l_p, 0, True), _Stream(b_s, l_s, 1, False))
    xs = [x_prompt.reshape(b_p * l_p, d), x_sample.reshape(b_s * l_s, d)]
    conds = jnp.concatenate([c_ctx[None, :], c, jnp.zeros((COND_ROWS - 1 - b_s, d), F32)], axis=0)
    rfft_mats = {st.l: _rfft_matrices(st.l) for st in streams}

    grids = [_ffn_grid(st, ffn_w1.shape[-1]) for st in streams]
    ffn_w = {(0, 0): (ffn_w1[0, 0].astype(BF16), ffn_w3[0, 0].astype(BF16), ffn_w2[0, 0].astype(BF16))}

    def ffn_half_step(xs, mods, l, s, extra=()):
        nxt = (l, 1) if s == 0 else (l + 1, 0)
        last = nxt[0] == depth
        w = ffn_w[(l, s)]
        g, mi = norm_g[l, 2 * s], 6 * s
        jobs0 = [] if last else [(a, _cast_tiling(grids[0], a, nxt, a is not ffn_w2))
                                 for a in (ffn_w1, ffn_w3, ffn_w2)]
        if any(tiling is None for _, tiling in jobs0):
            ffn_w[nxt] = tuple(a[nxt].astype(BF16) for a in (ffn_w1, ffn_w3, ffn_w2))
            jobs0 = []
        jobs1 = [(a, _cast_tiling(grids[1], a, lead)) for a, lead in extra]
        x0, done0 = _ffn(streams[0], xs[0], mods, g, final_g, *w, mi, final_norm=last, casts=jobs0)
        x1, done1 = _ffn(streams[1], xs[1], mods, g, final_g, *w, mi, final_norm=last,
                         casts=[job for job in jobs1 if job[1] is not None])
        if jobs0:
            ffn_w[nxt] = tuple(done0)
        done1 = iter(done1)
        converted = [next(done1) if tiling is not None else a[lead].astype(BF16)
                     for (a, tiling), (_, lead) in zip(jobs1, extra)]
        return [x0, x1], converted

    new_k, new_v = [], []
    for l in range(depth):
        mods = _adaln(conds, w_mod, b_mod, l)
        mixer_w = [(w_in, (l // 2,)), (w_out, (l // 2,))] if l % 2 == 0 else [(w_fnet, (l // 2,))]
        xs, mixer_w = ffn_half_step(xs, mods, l, 0, mixer_w)
        if l % 2 == 0:
            e = l // 2
            w_in_b, w_out_b = (a[None] for a in mixer_w)
            kv_dtypes = (F32, BF16)
            qkvh = [_norm_mm(st, x, mods, norm_g[l, 1], w_in_b, 0, 3,
                             ((attn_w, BF16), (attn_w, kv), (attn_w, kv), (3 * hyena_w, F32)))
                    for st, x, kv in zip(streams, xs, kv_dtypes)]
            attn = [_ctx_attn(streams[0], *qkvh[0][:3]),
                    _na_attn(streams[1], *qkvh[1][:3], cache_k[:, e].reshape(b_s * past, attn_w),
                             cache_v[:, e].reshape(b_s * past, attn_w), rpb[e])]
            for i, st in enumerate(streams):
                mats = rfft_mats[st.l]
                filters = _hyena_filters(st.l, hyena_w, filt_w1[e], filt_b1[e], filt_w2[e], filt_b2[e],
                                         filt_w3[e], filt_freq[e], mats[0], mats[1])
                hy = _hyena(st, qkvh[i][3], e, w_short, b_short, mats, filters, filt_bias)
                xs[i] = _mm_res(st, attn[i], hy, w_out_b, 0, xs[i], mods, 5, "out_proj")
            new_k.append(qkvh[0][1].reshape(b_p, l_p, n_heads, HEAD_DIM))
            new_v.append(qkvh[0][2].reshape(b_p, l_p, n_heads, HEAD_DIM))
        else:
            w_fnet_b = mixer_w[0][None]
            for i, st in enumerate(streams):
                mixed = _fnet_mix(st, _mod_norm(st, xs[i], mods, norm_g[l, 1], 3))
                xs[i] = _mm_res(st, mixed, None, w_fnet_b, 0, xs[i], mods, 5, "fnet_out")
        xs, _ = ffn_half_step(xs, mods, l, 1)

    y_prompt = xs[0].reshape(b_p, l_p, d)
    y_sample = xs[1].reshape(b_s, l_s, d)
    return y_prompt, y_sample, jnp.stack(new_k, axis=1), jnp.stack(new_v, axis=1)
```

```python
import dataclasses
import functools
import math

import jax
import jax.numpy as jnp
from jax import lax
from jax.experimental import pallas as pl
from jax.experimental.pallas import tpu as pltpu

F32 = jnp.float32
BF16 = jnp.bfloat16
HIGHEST = lax.Precision.HIGHEST

N_MOD = 9
GRID_W = 64
WIN_H = 8
WIN_W = 16
HEAD_DIM = 128
SHORT_CONV = 3
FILTER_EMB = 33
HYENA_ORDER = 2
DECAY_TARGET = 1e-2
FAST_DECAY_PCT = 0.3
SLOW_DECAY_PCT = 1.5
FNET_GROUPS = 8
EPS = 1e-6
NEG_INF = -1e30

V7X_VMEM_BYTES = 64 * 2**20
VMEM_LIMIT = V7X_VMEM_BYTES - 8 * 2**20
FFN_VMEM_LIMIT = V7X_VMEM_BYTES - 3 * 2**20
LANE = 128
BF16_SUBLANES = 16
COND_ROWS = 16


def _tile(n, pref, unit=LANE):
    if n <= pref:
        return n
    t = (pref // unit) * unit
    while t > unit and n % t:
        t -= unit
    assert n % t == 0, (n, pref)
    return t


def _params(*sem):
    return pltpu.CompilerParams(dimension_semantics=sem, vmem_limit_bytes=VMEM_LIMIT)


@dataclasses.dataclass(frozen=True)
class _Stream:
    n_seq: int
    l: int
    cond0: int
    shared: bool

    @property
    def n(self):
        return self.n_seq * self.l

    def row_tile(self, pref):
        return _tile(self.n if self.shared else self.l, pref, BF16_SUBLANES)

    def cond(self, i, tm):
        return self.cond0 if self.shared else self.cond0 + i // (self.l // tm)


def _fill_mod_norm(x_ref, g_ref, mod_ref, h_ref, rs_ref, mi, zero_ref=None):
    rows = BF16_SUBLANES
    tm, d = x_ref.shape
    cw = _tile(d, 512)

    def stats(c, carry):
        r = pl.multiple_of(c * rows, rows)
        x = x_ref[pl.ds(r, rows), :]
        rs = lax.rsqrt(jnp.mean(x * x, axis=-1, keepdims=True) + EPS)
        rs_ref[pl.ds(r, rows), :] = jnp.broadcast_to(rs, (rows, LANE))
        return carry

    lax.fori_loop(0, tm // rows, stats, 0, unroll=4)

    m = mod_ref[0]
    for t in range(d // cw):
        cs = slice(t * cw, (t + 1) * cw)
        gain = g_ref[:, cs] * (1.0 + m[mi + 1:mi + 2, cs])
        shift = m[mi:mi + 1, cs]

        def apply(c, carry):
            r = pl.multiple_of(c * rows, rows)
            rs = rs_ref[pl.ds(r, rows), :]
            rs = jnp.concatenate([rs] * (cw // LANE), axis=1) if cw > LANE else rs[:, :cw]
            h_ref[pl.ds(r, rows), cs] = (x_ref[pl.ds(r, rows), cs] * rs * gain + shift).astype(h_ref.dtype)
            if zero_ref is not None:
                zero_ref[pl.ds(r, rows), cs] = jnp.zeros((rows, cw), zero_ref.dtype)
            return carry

        lax.fori_loop(0, tm // rows, apply, 0)


def _dot(a, b):
    return jnp.dot(a, b, preferred_element_type=F32)


def _dot_hp(a, b):
    return jnp.dot(a, b, preferred_element_type=F32, precision=HIGHEST)


def _dot_nt(a, b):
    return lax.dot_general(a, b, (((1,), (1,)), ((), ())), preferred_element_type=F32)


def _adaln_kernel(c_ref, w_ref, b_ref, o_ref):
    c = c_ref[...]
    s = c * (1.0 / (1.0 + jnp.exp(-c)))
    s_hi = s.astype(BF16)
    s_lo = (s - s_hi.astype(F32)).astype(BF16)
    w = w_ref[...].astype(BF16)
    o_ref[...] = _dot(s_hi, w) + _dot(s_lo, w) + b_ref[...]


def _adaln(conds, w_mod, b_mod, l):
    d = conds.shape[1]
    n = w_mod.shape[-1]
    tn = _tile(n, 1024)
    out = pl.pallas_call(
        _adaln_kernel,
        grid=(n // tn,),
        in_specs=[
            pl.BlockSpec((COND_ROWS, d), lambda j: (0, 0)),
            pl.BlockSpec((None, d, tn), lambda j: (l, 0, j)),
            pl.BlockSpec((None, 1, tn), lambda j: (l, 0, j)),
        ],
        out_specs=pl.BlockSpec((COND_ROWS, tn), lambda j: (0, j)),
        out_shape=jax.ShapeDtypeStruct((COND_ROWS, n), F32),
        compiler_params=_params("arbitrary"),
        name="adaln",
    )(conds, w_mod, b_mod.reshape(b_mod.shape[0], 1, n))
    return out.reshape(COND_ROWS, N_MOD, d)


FFN_RING = 3


def _ffn_kernel(x_ref, mod_ref, g_ref, fin_ref, w1_hbm, w3_hbm, w2_hbm, *rest, mi, final_norm):
    i, j = pl.program_id(0), pl.program_id(1)
    nf = pl.num_programs(1)
    n_cast = (len(rest) - 7) // 2
    f32_tiles, o_ref, bf16_tiles = rest[:n_cast], rest[n_cast], rest[n_cast + 1:2 * n_cast + 1]
    h_ref, rs_ref, w1_buf, w3_buf, w2_buf, sem = rest[2 * n_cast + 1:]
    tf = w2_buf.shape[1]
    step = i * nf + j
    n_steps = pl.num_programs(0) * nf

    def tile_copies(s):
        slot = s % FFN_RING
        col = pl.multiple_of((s % nf) * tf, tf)
        return (pltpu.make_async_copy(w1_hbm.at[:, pl.ds(col, tf)], w1_buf.at[slot], sem.at[slot, 0]),
                pltpu.make_async_copy(w3_hbm.at[:, pl.ds(col, tf)], w3_buf.at[slot], sem.at[slot, 1]),
                pltpu.make_async_copy(w2_hbm.at[pl.ds(col, tf), :], w2_buf.at[slot], sem.at[slot, 2]))

    @pl.when(step == 0)
    def _():
        for s in range(FFN_RING - 1):
            for copy in tile_copies(s):
                copy.start()

    @pl.when(step + (FFN_RING - 1) < n_steps)
    def _():
        for copy in tile_copies(step + (FFN_RING - 1)):
            copy.start()

    @pl.when(j == 0)
    def _():
        _fill_mod_norm(x_ref, g_ref, mod_ref, h_ref, rs_ref, mi, zero_ref=o_ref)

    for src, dst in zip(f32_tiles, bf16_tiles):
        dst[...] = src[...].astype(dst.dtype)

    for copy in tile_copies(step):
        copy.wait()
    slot = step % FFN_RING
    h = h_ref[...]
    a = _dot(h, w1_buf[slot])
    b = _dot(h, w3_buf[slot])
    act = (a * (1.0 / (1.0 + jnp.exp(-a))) * b).astype(BF16)
    o_ref[...] += _dot(act, w2_buf[slot])

    @pl.when(j == pl.num_programs(1) - 1)
    def _():
        rows = BF16_SUBLANES
        tm, d = o_ref.shape
        half_gate = 0.5 * mod_ref[0][mi + 2:mi + 3]

        def body(c, carry):
            r = pl.multiple_of(c * rows, rows)
            y = x_ref[pl.ds(r, rows), :] + half_gate * o_ref[pl.ds(r, rows), :]
            o_ref[pl.ds(r, rows), :] = y
            if final_norm:
                rs = lax.rsqrt(jnp.mean(y * y, axis=-1, keepdims=True) + EPS)
                rs_ref[pl.ds(r, rows), :] = jnp.broadcast_to(rs, (rows, LANE))
            return carry

        lax.fori_loop(0, tm // rows, body, 0, unroll=4 if final_norm else 1)

        if final_norm:
            cw = _tile(d, 512)
            for t in range(d // cw):
                cs = slice(t * cw, (t + 1) * cw)
                gain = fin_ref[:, cs]

                def scale(c, carry):
                    r = pl.multiple_of(c * rows, rows)
                    rs = jnp.concatenate([rs_ref[pl.ds(r, rows), :]] * (cw // LANE), axis=1)
                    o_ref[pl.ds(r, rows), cs] = o_ref[pl.ds(r, rows), cs] * rs * gain
                    return carry

                lax.fori_loop(0, tm // rows, scale, 0)


FFN_TF = 256


def _ffn_grid(st, f):
    tm = st.row_tile(512)
    return st.n // tm, f // _tile(f, FFN_TF)


def _cast_tiling(grid, src, lead, rows_on_i=True):
    nm, nf = grid
    r, c = src.shape[-2:]
    ni_len, nj_len = (r, c) if rows_on_i else (c, r)
    unit_i, unit_j = (BF16_SUBLANES, LANE) if rows_on_i else (LANE, BF16_SUBLANES)
    nj = next((n for n in range(nf, 0, -1) if nj_len % n == 0 and (nj_len // n) % unit_j == 0), None)
    if ni_len % nm or (ni_len // nm) % unit_i or nj is None:
        return None
    ti, tj = ni_len // nm, nj_len // nj
    pick = (lambda i, j: (i, jnp.minimum(j, nj - 1))) if rows_on_i else (lambda i, j: (jnp.minimum(j, nj - 1), i))
    block = (ti, tj) if rows_on_i else (tj, ti)
    return (pl.BlockSpec((None,) * len(lead) + block, lambda i, j: tuple(lead) + pick(i, j)),
            pl.BlockSpec(block, pick), jax.ShapeDtypeStruct((r, c), BF16))


def _ffn(st, x, mods, g, final_g, w1, w3, w2, mi, final_norm=False, casts=()):
    m, d = x.shape
    f = w1.shape[-1]
    tm = st.row_tile(512)
    tf = _tile(f, FFN_TF)
    nm, nf = _ffn_grid(st, f)
    in_specs = [
        pl.BlockSpec((tm, d), lambda i, j: (i, 0)),
        pl.BlockSpec((1, N_MOD, d), lambda i, j: (st.cond(i, tm), 0, 0)),
        pl.BlockSpec((1, d), lambda i, j: (0, 0)),
        pl.BlockSpec((1, d), lambda i, j: (0, 0)),
        pl.BlockSpec(memory_space=pl.ANY),
        pl.BlockSpec(memory_space=pl.ANY),
        pl.BlockSpec(memory_space=pl.ANY),
    ]
    assert nm * nf >= FFN_RING - 1
    out_specs = [pl.BlockSpec((tm, d), lambda i, j: (i, 0))]
    out_shape = [jax.ShapeDtypeStruct((m, d), F32)]
    args = [x, mods, g.reshape(1, d), final_g.reshape(1, d), w1, w3, w2]
    for src, (src_spec, dst_spec, dst_shape) in casts:
        in_specs.append(src_spec)
        out_specs.append(dst_spec)
        out_shape.append(dst_shape)
        args.append(src)
    out = pl.pallas_call(
        functools.partial(_ffn_kernel, mi=mi, final_norm=final_norm),
        grid=(nm, nf),
        in_specs=in_specs,
        out_specs=out_specs,
        out_shape=out_shape,
        scratch_shapes=[pltpu.VMEM((tm, d), BF16), pltpu.VMEM((tm, LANE), F32),
                        pltpu.VMEM((FFN_RING, d, tf), BF16), pltpu.VMEM((FFN_RING, d, tf), BF16),
                        pltpu.VMEM((FFN_RING, tf, d), BF16), pltpu.SemaphoreType.DMA((FFN_RING, 3))],
        compiler_params=pltpu.CompilerParams(dimension_semantics=("arbitrary", "arbitrary"),
                                             vmem_limit_bytes=FFN_VMEM_LIMIT),
        name="ffn",
    )(*args)
    return out[0], list(out[1:])


def _norm_mm_kernel(x_ref, mod_ref, g_ref, w_ref, *rest, mi, tile_ranges):
    outs, (h_ref, rs_ref) = rest[:-2], rest[-2:]
    j = pl.program_id(1)

    @pl.when(j == 0)
    def _():
        _fill_mod_norm(x_ref, g_ref, mod_ref, h_ref, rs_ref, mi)

    for o_ref, (lo, hi) in zip(outs, tile_ranges):
        @pl.when((j >= lo) & (j < hi))
        def _(o_ref=o_ref):
            o_ref[...] = _dot(h_ref[...], w_ref[...]).astype(o_ref.dtype)


def _norm_mm(st, x, mods, g, w, e, mi, splits):
    m, d = x.shape
    n = w.shape[-1]
    tm = st.row_tile(512)
    tn = _tile(math.gcd(*[width for width, _ in splits]), 1024)
    assert sum(width for width, _ in splits) == n
    tile_ranges, out_specs, lo = [], [], 0
    for width, _ in splits:
        hi = lo + width // tn
        tile_ranges.append((lo, hi))
        out_specs.append(pl.BlockSpec((tm, tn), lambda i, j, lo=lo, hi=hi: (i, jnp.clip(j - lo, 0, hi - lo - 1))))
        lo = hi
    return pl.pallas_call(
        functools.partial(_norm_mm_kernel, mi=mi, tile_ranges=tuple(tile_ranges)),
        grid=(m // tm, n // tn),
        in_specs=[
            pl.BlockSpec((tm, d), lambda i, j: (i, 0)),
            pl.BlockSpec((1, N_MOD, d), lambda i, j: (st.cond(i, tm), 0, 0)),
            pl.BlockSpec((1, d), lambda i, j: (0, 0)),
            pl.BlockSpec((None, d, tn), lambda i, j: (e, 0, j)),
        ],
        out_specs=out_specs,
        out_shape=[jax.ShapeDtypeStruct((m, width), dtype) for width, dtype in splits],
        scratch_shapes=[pltpu.VMEM((tm, d), BF16), pltpu.VMEM((tm, LANE), F32)],
        compiler_params=_params("parallel", "arbitrary"),
        name="in_proj",
    )(x, mods, g.reshape(1, d), w)


def _mod_norm_kernel(x_ref, mod_ref, g_ref, h_ref, rs_ref, *, mi):
    _fill_mod_norm(x_ref, g_ref, mod_ref, h_ref, rs_ref, mi)


def _mod_norm(st, x, mods, g, mi):
    m, d = x.shape
    tm = st.row_tile(256)
    return pl.pallas_call(
        functools.partial(_mod_norm_kernel, mi=mi),
        grid=(m // tm,),
        in_specs=[
            pl.BlockSpec((tm, d), lambda i: (i, 0)),
            pl.BlockSpec((1, N_MOD, d), lambda i: (st.cond(i, tm), 0, 0)),
            pl.BlockSpec((1, d), lambda i: (0, 0)),
        ],
        out_specs=pl.BlockSpec((tm, d), lambda i: (i, 0)),
        out_shape=jax.ShapeDtypeStruct((m, d), BF16),
        scratch_shapes=[pltpu.VMEM((tm, LANE), F32)],
        compiler_params=_params("parallel"),
        name="mod_norm",
    )(x, mods, g.reshape(1, d))


def _mm_res_kernel(a0_ref, a1_ref, w0_ref, w1_ref, res_ref, mod_ref, o_ref, *, gi):
    mix = _dot(a0_ref[...], w0_ref[...]) + _dot(a1_ref[...], w1_ref[...])
    o_ref[...] = res_ref[...] + mod_ref[0][gi:gi + 1] * mix


def _mm_res(st, a0, a1, w, e, res, mods, gi, name):
    k = w.shape[1] // 2
    m = a0.shape[0]
    c1 = 0
    if a1 is None:
        a1, c1 = a0, 1
    assert a0.shape[1] >= k and a1.shape == (m, (c1 + 1) * k)
    n = w.shape[-1]
    tm = st.row_tile(1024)
    tn = _tile(n, 1024)
    return pl.pallas_call(
        functools.partial(_mm_res_kernel, gi=gi),
        grid=(m // tm, n // tn),
        in_specs=[
            pl.BlockSpec((tm, k), lambda i, j: (i, 0)),
            pl.BlockSpec((tm, k), lambda i, j: (i, c1)),
            pl.BlockSpec((None, k, tn), lambda i, j: (e, 0, j)),
            pl.BlockSpec((None, k, tn), lambda i, j: (e, 1, j)),
            pl.BlockSpec((tm, tn), lambda i, j: (i, j)),
            pl.BlockSpec((1, N_MOD, tn), lambda i, j: (st.cond(i, tm), 0, j)),
        ],
        out_specs=pl.BlockSpec((tm, tn), lambda i, j: (i, j)),
        out_shape=jax.ShapeDtypeStruct((m, n), F32),
        compiler_params=_params("parallel", "arbitrary"),
        name=name,
    )(a0, a1, w, w, res, mods)


def _ctx_attn_kernel(q_ref, k_ref, v_ref, o_ref, *, n_heads, scale):
    for h in range(n_heads):
        sl = slice(h * HEAD_DIM, (h + 1) * HEAD_DIM)
        q = q_ref[:, sl].astype(BF16)
        k = k_ref[:, sl].astype(BF16)
        v = v_ref[:, sl].astype(BF16)
        s = _dot_nt(q, k) * scale
        e = jnp.exp(s - jnp.max(s, axis=-1, keepdims=True))
        p = e * (1.0 / jnp.sum(e, axis=-1, keepdims=True))
        o_ref[:, sl] = _dot(p.astype(BF16), v).astype(o_ref.dtype)


def _ctx_attn(st, q, k, v):
    l = st.l
    width = q.shape[1]
    return pl.pallas_call(
        functools.partial(_ctx_attn_kernel, n_heads=width // HEAD_DIM, scale=HEAD_DIM ** -0.5),
        grid=(st.n_seq,),
        in_specs=[pl.BlockSpec((l, width), lambda b: (b, 0))] * 3,
        out_specs=pl.BlockSpec((l, width), lambda b: (b, 0)),
        out_shape=jax.ShapeDtypeStruct((st.n, width), BF16),
        compiler_params=_params("parallel"),
        name="ctx_attn",
    )(q, k, v)


def _na_window(r, rows, kh):
    start = min(max(r - kh // 2, 0), rows - kh)
    lo = start - start % 2
    hi = start + kh + (start + kh) % 2
    return start, lo, hi


NA_ROW_GROUP = 4


def _na_attn_kernel(q_ref, k_ref, v_ref, ck_ref, cv_ref, tab_ref, o_ref, *, rows, kh, scale):
    ck = ck_ref[...].astype(BF16)
    cv = cv_ref[...].astype(BF16)
    for g0 in range(0, rows, NA_ROW_GROUP):
        group = range(g0, min(g0 + NA_ROW_GROUP, rows))
        windows = [_na_window(r, rows, kh) for r in group]
        glo = min(w[1] for w in windows) * GRID_W
        ghi = max(w[2] for w in windows) * GRID_W
        q = q_ref[g0 * GRID_W:(g0 + len(group)) * GRID_W, :].astype(BF16)
        s_lat_g = _dot_nt(q, k_ref[glo:ghi, :].astype(BF16))
        s_ctx_g = _dot_nt(q, ck)
        p_lat_g, p_ctx_g = [], []
        for i, (r, (start, lo, hi)) in enumerate(zip(group, windows)):
            qs = slice(i * GRID_W, (i + 1) * GRID_W)
            pieces = []
            for p in range(lo // 2, hi // 2):
                ok0 = start <= 2 * p < start + kh
                ok1 = start <= 2 * p + 1 < start + kh
                dr = 2 * p - r + WIN_H - 1
                pieces.append(tab_ref[0, 0, dr] if ok0 and ok1 else tab_ref[0, 1, dr] if ok0
                              else tab_ref[0, 2, dr + 1])
            lo, hi = lo * GRID_W, hi * GRID_W
            s_lat = s_lat_g[qs, lo - glo:hi - glo] * scale + jnp.concatenate(pieces, axis=1)
            s_ctx = s_ctx_g[qs, :] * scale
            mx = jnp.maximum(jnp.max(s_lat, axis=-1, keepdims=True), jnp.max(s_ctx, axis=-1, keepdims=True))
            e_lat = jnp.exp(s_lat - mx)
            e_ctx = jnp.exp(s_ctx - mx)
            inv = 1.0 / (jnp.sum(e_lat, axis=-1, keepdims=True) + jnp.sum(e_ctx, axis=-1, keepdims=True))
            parts = [(e_lat * inv).astype(BF16)]
            if lo > glo:
                parts.insert(0, jnp.zeros((GRID_W, lo - glo), BF16))
            if hi < ghi:
                parts.append(jnp.zeros((GRID_W, ghi - hi), BF16))
            p_lat_g.append(jnp.concatenate(parts, axis=1) if len(parts) > 1 else parts[0])
            p_ctx_g.append((e_ctx * inv).astype(BF16))
        o = (_dot(jnp.concatenate(p_lat_g, axis=0), v_ref[glo:ghi, :].astype(BF16))
             + _dot(jnp.concatenate(p_ctx_g, axis=0), cv))
        o_ref[g0 * GRID_W:(g0 + len(group)) * GRID_W, :] = o.astype(o_ref.dtype)


def _na_bias_table(rpb):
    col = jnp.arange(GRID_W)
    col_start = jnp.clip(col - WIN_W // 2, 0, GRID_W - WIN_W)
    col_ok = (col[None, :] >= col_start[:, None]) & (col[None, :] < col_start[:, None] + WIN_W)
    dc_idx = jnp.clip(col[None, :] - col[:, None] + WIN_W - 1, 0, 2 * WIN_W - 2)
    t = jnp.where(col_ok, rpb.astype(F32)[:, :, dc_idx], NEG_INF)
    neg = jnp.full_like(t, NEG_INF)
    nxt = jnp.concatenate([t[:, 1:], neg[:, :1]], axis=1)
    return jnp.stack([jnp.concatenate([t, nxt], axis=-1), jnp.concatenate([t, neg], axis=-1),
                      jnp.concatenate([neg, t], axis=-1)], axis=1)


def _na_attn(st, q, k, v, cache_k, cache_v, rpb):
    l = st.l
    width = q.shape[1]
    n_heads = width // HEAD_DIM
    rows = l // GRID_W
    kh = min(WIN_H, rows)
    assert rows % 2 == 0 and kh % 2 == 0
    past = cache_k.shape[0] // st.n_seq
    tab = _na_bias_table(rpb)
    return pl.pallas_call(
        functools.partial(_na_attn_kernel, rows=rows, kh=kh, scale=HEAD_DIM ** -0.5),
        grid=(st.n_seq, n_heads),
        in_specs=[pl.BlockSpec((l, HEAD_DIM), lambda b, h: (b, h))] * 3 + [
            pl.BlockSpec((past, HEAD_DIM), lambda b, h: (b, h)),
            pl.BlockSpec((past, HEAD_DIM), lambda b, h: (b, h)),
            pl.BlockSpec((1,) + tab.shape[1:], lambda b, h: (h, 0, 0, 0, 0)),
        ],
        out_specs=pl.BlockSpec((l, HEAD_DIM), lambda b, h: (b, h)),
        out_shape=jax.ShapeDtypeStruct((st.n, width), BF16),
        compiler_params=_params("parallel", "arbitrary"),
        name="na_attn",
    )(q, k, v, cache_k, cache_v, tab)


def _cos_sin(n_rows, n_cols, period):
    r = lax.broadcasted_iota(jnp.int32, (n_rows, n_cols), 0)
    c = lax.broadcasted_iota(jnp.int32, (n_rows, n_cols), 1)
    ang = ((r * c) % period).astype(F32) * (2.0 * math.pi / period)
    return jnp.cos(ang), jnp.sin(ang)


def _rfft_matrices(l):
    n = 2 * l
    cos, sin = _cos_sin(l, l, n)
    alt = jnp.where(lax.broadcasted_iota(jnp.int32, (l, l), 1) % 2 == 0, 1.0, -1.0).astype(F32)
    k_is0 = lax.broadcasted_iota(jnp.int32, (l, l), 0) == 0
    fwd_re = cos
    fwd_im = jnp.where(k_is0, alt, -sin)
    bin_is0 = lax.broadcasted_iota(jnp.int32, (l, l), 1) == 0
    inv_re = jnp.where(bin_is0, 1.0 / n, 2.0 / n) * cos
    inv_im = jnp.where(bin_is0, alt.T / n, (-2.0 / n) * sin)
    return fwd_re, fwd_im, inv_re, inv_im


def _split_bf16(a):
    hi = a.astype(BF16)
    return hi, (a - hi.astype(F32)).astype(BF16)


def _dot_3pass(a_hi_ref, a_lo_ref, b):
    b_hi, b_lo = _split_bf16(b)
    a_hi = a_hi_ref[...]
    return _dot(a_hi, b_hi) + (_dot(a_hi, b_lo) + _dot(a_lo_ref[...], b_hi))


def _filter_kernel(z_ref, t_ref, w1_ref, b1_ref, w2_ref, b2_ref, fq_ref, w3f_ref, w3b_ref, dl_ref,
                   fre_hi_ref, fre_lo_ref, fim_hi_ref, fim_lo_ref, kr_ref, ki_ref, kn_ref):
    fq = fq_ref[...]
    hid = jnp.sin(fq[0:1] * (_dot_hp(z_ref[...], w1_ref[...]) + b1_ref[...]))
    hid = jnp.sin(fq[1:2] * (_dot_hp(hid, w2_ref[...]) + b2_ref[...]))
    decay = jnp.exp(-t_ref[...] * dl_ref[...])
    row = lax.broadcasted_iota(jnp.int32, kr_ref.shape, 0)
    h_fwd = _dot_hp(hid, w3f_ref[...]) * decay
    h_bwd = jnp.where(row == 0, 0.0, _dot_hp(hid, w3b_ref[...]) * decay)
    nrm = lax.rsqrt(jnp.sum(h_fwd * h_fwd + h_bwd * h_bwd, axis=0, keepdims=True) + EPS)
    even = (h_fwd + h_bwd) * nrm
    odd = (h_fwd - h_bwd) * nrm
    k_re = _dot_3pass(fre_hi_ref, fre_lo_ref, even)
    k_im = _dot_3pass(fim_hi_ref, fim_lo_ref, odd)
    nyq = jnp.sum(jnp.where(row % 2 == 0, even, -even), axis=0, keepdims=True)
    kr_ref[...] = k_re
    ki_ref[...] = jnp.where(row == 0, 0.0, k_im)
    kn_ref[...] = jnp.where(row == 0, nyq, k_re)


def _pad_to(a, shape):
    return jnp.pad(a.astype(F32), [(0, s - n) for n, s in zip(a.shape, shape)])


def _hyena_filters(l, width, w1, b1, w2, b2, w3, freq, fwd_re, fwd_im):
    hid = w2.shape[0]
    hp = LANE * pl.cdiv(hid, LANE)
    n_bands = (FILTER_EMB - 1) // 2
    t = jnp.linspace(0.0, 1.0, l, dtype=F32)[:, None]
    w = (2.0 * math.pi / l) * jnp.arange(l, dtype=F32)[:, None]
    bands = jnp.linspace(1e-4, n_bands - 1, n_bands, dtype=F32)[None, :]
    z = _pad_to(jnp.concatenate([t, jnp.cos(bands * w), -jnp.sin(bands * w)], axis=-1), (l, LANE))
    deltas = jnp.abs(jnp.linspace(math.log(DECAY_TARGET) / FAST_DECAY_PCT,
                                  math.log(DECAY_TARGET) / SLOW_DECAY_PCT, width, dtype=F32))[None, :]
    tc = _tile(width, 512)
    nc = width // tc
    small = lambda shape: pl.BlockSpec(shape, lambda o, c: (0,) * len(shape))
    out = jax.ShapeDtypeStruct((HYENA_ORDER, l, width), F32)
    return pl.pallas_call(
        _filter_kernel,
        grid=(HYENA_ORDER, nc),
        in_specs=[small((l, LANE)), small((l, 1)), small((LANE, hp)), small((1, hp)), small((hp, hp)),
                  small((1, hp)), small((2, hp)),
                  pl.BlockSpec((hp, tc), lambda o, c: (0, 2 * o * nc + c)),
                  pl.BlockSpec((hp, tc), lambda o, c: (0, (2 * o + 1) * nc + c)),
                  pl.BlockSpec((1, tc), lambda o, c: (0, c))] + [small((l, l))] * 4,
        out_specs=[pl.BlockSpec((None, l, tc), lambda o, c: (o, 0, c))] * 3,
        out_shape=[out, out, out],
        compiler_params=_params("arbitrary", "arbitrary"),
        name="hyena_filters",
    )(z, t, _pad_to(w1, (LANE, hp)), _pad_to(b1[None, :], (1, hp)), _pad_to(w2, (hp, hp)),
      _pad_to(b2[None, :], (1, hp)), _pad_to(freq, (2, hp)), _pad_to(w3, (hp, w3.shape[1])),
      _pad_to(w3, (hp, w3.shape[1])), deltas, *_split_bf16(fwd_re), *_split_bf16(fwd_im))


HYENA_ROW_CHUNK = 256


def _hyena_kernel(pv_ref, p1_ref, p2_ref, wv_ref, w1_ref, w2_ref, bv_ref, b1_ref, b2_ref,
                  fre_ref, fim_ref, gre_ref, gim_ref, kr_ref, ki_ref, kn_ref, fbias_ref, o_ref, *, l):
    tc = o_ref.shape[1]
    row = lax.broadcasted_iota(jnp.int32, (l, tc), 0)

    def short_conv(p, w_ref, b_ref):
        prev = jnp.where(row == 0, 0.0, pltpu.roll(p, 1, 0))
        nxt = jnp.where(row == l - 1, 0.0, pltpu.roll(p, l - 1, 0))
        w = w_ref[...]
        out = b_ref[...] + prev * w[0:1]
        out = out + p * w[1:2]
        return out + nxt * w[2:3]

    chunks = [slice(r, r + HYENA_ROW_CHUNK) for r in range(0, l, HYENA_ROW_CHUNK)] if l > HYENA_ROW_CHUNK \
        else [slice(0, l)]

    def gated_long_conv(x, u, o, bias):
        ub = u.astype(BF16)
        w_re, w_im = [], []
        for rs in chunks:
            u_re = _dot(fre_ref[rs, :], ub)
            u_im = _dot(fim_ref[rs, :], ub)
            kr, ki, kn = kr_ref[o, rs, :], ki_ref[o, rs, :], kn_ref[o, rs, :]
            w_re.append((u_re * kr - u_im * ki).astype(BF16))
            w_im.append((u_re * ki + u_im * kn).astype(BF16))
        w_re = jnp.concatenate(w_re, axis=0)
        w_im = jnp.concatenate(w_im, axis=0)
        out = []
        for rs in chunks:
            y = _dot(gre_ref[rs, :], w_re) + _dot(gim_ref[rs, :], w_im)
            out.append(x[rs] * (y + u[rs] * bias))
        return jnp.concatenate(out, axis=0)

    fbias = fbias_ref[...]
    for s in range(o_ref.shape[0] // l):
        rs = slice(s * l, (s + 1) * l)
        v = short_conv(pv_ref[rs, :], wv_ref, bv_ref)
        z = gated_long_conv(short_conv(p1_ref[rs, :], w1_ref, b1_ref), v, 0, fbias[0:1])
        z = gated_long_conv(short_conv(p2_ref[rs, :], w2_ref, b2_ref), z, 1, fbias[1:2])
        o_ref[rs, :] = z.astype(o_ref.dtype)


def _hyena(st, proj, e, w_short, b_short, mats, filters, fbias):
    l = st.l
    width = proj.shape[1] // 3
    tc = _tile(width, 1024 if l <= 256 else 256)
    per_step = 2 if l > 256 and st.n_seq % 2 == 0 else 1
    nc = width // tc
    rows = per_step * l
    once = pl.Buffered(1)
    p_specs = [pl.BlockSpec((rows, tc), lambda c, b, part=part: (b, part * nc + c)) for part in range(3)]
    w_specs = [pl.BlockSpec((None, SHORT_CONV, tc), lambda c, b, part=part: (e, 0, part * nc + c))
               for part in range(3)]
    b_specs = [pl.BlockSpec((None, 1, tc), lambda c, b, part=part: (e, 0, part * nc + c)) for part in range(3)]
    mat_specs = [pl.BlockSpec((l, l), lambda c, b: (0, 0), pipeline_mode=once)] * 4
    k_specs = [pl.BlockSpec((HYENA_ORDER, l, tc), lambda c, b: (0, 0, c), pipeline_mode=once)] * 3
    return pl.pallas_call(
        functools.partial(_hyena_kernel, l=l),
        grid=(nc, st.n_seq // per_step),
        in_specs=p_specs + w_specs + b_specs + mat_specs + k_specs
        + [pl.BlockSpec((None, HYENA_ORDER, tc), lambda c, b: (e, 0, c))],
        out_specs=pl.BlockSpec((rows, tc), lambda c, b: (b, c)),
        out_shape=jax.ShapeDtypeStruct((st.n, width), BF16),
        compiler_params=_params("parallel", "arbitrary"),
        name="hyena",
    )(proj, proj, proj, w_short, w_short, w_short, *([b_short.reshape(b_short.shape[0], 1, -1)] * 3),
      *[a.astype(BF16) for a in mats], *filters, fbias)


FNET_ROW_CHUNK = 256


def _fnet_mix_kernel(cos_l_ref, sin_l_ref, cos_c_ref, sin_c_ref, h_ref, o_ref, *, gw):
    l, tn = h_ref.shape
    cos_c = cos_c_ref[...]
    sin_c = sin_c_ref[...]
    for g in range(tn // gw):
        cs = slice(g * gw, (g + 1) * gw)
        h = h_ref[:, cs]
        for r in range(0, l, FNET_ROW_CHUNK):
            rs = slice(r, min(r + FNET_ROW_CHUNK, l))
            a = _dot(cos_l_ref[rs, :], h).astype(BF16)
            b = _dot(sin_l_ref[rs, :], h).astype(BF16)
            o_ref[rs, cs] = (_dot(a, cos_c) - _dot(b, sin_c)).astype(o_ref.dtype)


def _fnet_mix(st, h):
    l = st.l
    d = h.shape[1]
    gw = d // FNET_GROUPS
    groups_per_step = min(FNET_GROUPS, max(1, 2048 // gw)) if l <= 256 else 1
    tn = gw * groups_per_step
    cos_l, sin_l = _cos_sin(l, l, l)
    cos_c, sin_c = _cos_sin(gw, gw, gw)
    mats = [(cos_l * l ** -0.5).astype(BF16), (sin_l * l ** -0.5).astype(BF16),
            (cos_c * gw ** -0.5).astype(BF16), (sin_c * gw ** -0.5).astype(BF16)]
    once = pl.Buffered(1)
    return pl.pallas_call(
        functools.partial(_fnet_mix_kernel, gw=gw),
        grid=(st.n_seq, d // tn),
        in_specs=[pl.BlockSpec((l, l), lambda b, j: (0, 0), pipeline_mode=once)] * 2
        + [pl.BlockSpec((gw, gw), lambda b, j: (0, 0), pipeline_mode=once)] * 2
        + [pl.BlockSpec((l, tn), lambda b, j: (b, j))],
        out_specs=pl.BlockSpec((l, tn), lambda b, j: (b, j)),
        out_shape=jax.ShapeDtypeStruct((st.n, d), BF16),
        compiler_params=_params("parallel", "arbitrary"),
        name="fnet_mix",
    )(*mats, h)


def kernel(x_prompt, x_sample, cache_k, cache_v, c, c_ctx, w_mod, b_mod, norm_g, ffn_w1, ffn_w3, ffn_w2,
           w_in, w_out, rpb, w_short, b_short, filt_w1, filt_b1, filt_w2, filt_b2, filt_w3, filt_freq,
           filt_bias, w_fnet, final_g):
    b_p, l_p, d = x_prompt.shape
    b_s, l_s, _ = x_sample.shape
    depth = w_mod.shape[0]
    attn_w = cache_k.shape[-2] * cache_k.shape[-1]
    hyena_w = filt_bias.shape[-1]
    n_heads = attn_w // HEAD_DIM
    past = cache_k.shape[2]
    assert 1 + b_s <= COND_ROWS

    streams = (_Stream(b_p, l_p, 0, True), _Stream(b_s, l_s, 1, False))
    xs = [x_prompt.reshape(b_p * l_p, d), x_sample.reshape(b_s * l_s, d)]
    conds = jnp.concatenate([c_ctx[None, :], c, jnp.zeros((COND_ROWS - 1 - b_s, d), F32)], axis=0)
    rfft_mats = {st.l: _rfft_matrices(st.l) for st in streams}

    grids = [_ffn_grid(st, ffn_w1.shape[-1]) for st in streams]
    ffn_w = {(0, 0): (ffn_w1[0, 0].astype(BF16), ffn_w3[0, 0].astype(BF16), ffn_w2[0, 0].astype(BF16))}

    def ffn_half_step(xs, mods, l, s, extra=()):
        nxt = (l, 1) if s == 0 else (l + 1, 0)
        last = nxt[0] == depth
        w = ffn_w[(l, s)]
        g, mi = norm_g[l, 2 * s], 6 * s
        jobs0 = [] if last else [(a, _cast_tiling(grids[0], a, nxt, a is not ffn_w2))
                                 for a in (ffn_w1, ffn_w3, ffn_w2)]
        if any(tiling is None for _, tiling in jobs0):
            ffn_w[nxt] = tuple(a[nxt].astype(BF16) for a in (ffn_w1, ffn_w3, ffn_w2))
            jobs0 = []
        jobs1 = [(a, _cast_tiling(grids[1], a, lead)) for a, lead in extra]
        x0, done0 = _ffn(streams[0], xs[0], mods, g, final_g, *w, mi, final_norm=last, casts=jobs0)
        x1, done1 = _ffn(streams[1], xs[1], mods, g, final_g, *w, mi, final_norm=last,
                         casts=[job for job in jobs1 if job[1] is not None])
        if jobs0:
            ffn_w[nxt] = tuple(done0)
        done1 = iter(done1)
        converted = [next(done1) if tiling is not None else a[lead].astype(BF16)
                     for (a, tiling), (_, lead) in zip(jobs1, extra)]
        return [x0, x1], converted

    new_k, new_v = [], []
    for l in range(depth):
        mods = _adaln(conds, w_mod, b_mod, l)
        mixer_w = [(w_in, (l // 2,)), (w_out, (l // 2,))] if l % 2 == 0 else [(w_fnet, (l // 2,))]
        xs, mixer_w = ffn_half_step(xs, mods, l, 0, mixer_w)
        if l % 2 == 0:
            e = l // 2
            w_in_b, w_out_b = (a[None] for a in mixer_w)
            kv_dtypes = (F32, BF16)
            qkvh = [_norm_mm(st, x, mods, norm_g[l, 1], w_in_b, 0, 3,
                             ((attn_w, BF16), (attn_w, kv), (attn_w, kv), (3 * hyena_w, F32)))
                    for st, x, kv in zip(streams, xs, kv_dtypes)]
            attn = [_ctx_attn(streams[0], *qkvh[0][:3]),
                    _na_attn(streams[1], *qkvh[1][:3], cache_k[:, e].reshape(b_s * past, attn_w),
                             cache_v[:, e].reshape(b_s * past, attn_w), rpb[e])]
            for i, st in enumerate(streams):
                mats = rfft_mats[st.l]
                filters = _hyena_filters(st.l, hyena_w, filt_w1[e], filt_b1[e], filt_w2[e], filt_b2[e],
                                         filt_w3[e], filt_freq[e], mats[0], mats[1])
                hy = _hyena(st, qkvh[i][3], e, w_short, b_short, mats, filters, filt_bias)
                xs[i] = _mm_res(st, attn[i], hy, w_out_b, 0, xs[i], mods, 5, "out_proj")
            new_k.append(qkvh[0][1].reshape(b_p, l_p, n_heads, HEAD_DIM))
            new_v.append(qkvh[0][2].reshape(b_p, l_p, n_heads, HEAD_DIM))
        else:
            w_fnet_b = mixer_w[0][None]
            for i, st in enumerate(streams):
                mixed = _fnet_mix(st, _mod_norm(st, xs[i], mods, norm_g[l, 1], 3))
                xs[i] = _mm_res(st, mixed, None, w_fnet_b, 0, xs[i], mods, 5, "fnet_out")
        xs, _ = ffn_half_step(xs, mods, l, 1)

    y_prompt = xs[0].reshape(b_p, l_p, d)
    y_sample = xs[1].reshape(b_s, l_s, d)
    return y_prompt, y_sample, jnp.stack(new_k, axis=1), jnp.stack(new_v, axis=1)
```

```python
import dataclasses
import functools
import math

import jax
import jax.numpy as jnp
from jax import lax
from jax.experimental import pallas as pl
from jax.experimental.pallas import tpu as pltpu

F32 = jnp.float32
BF16 = jnp.bfloat16
HIGHEST = lax.Precision.HIGHEST

N_MOD = 9
GRID_W = 64
WIN_H = 8
WIN_W = 16
HEAD_DIM = 128
SHORT_CONV = 3
FILTER_EMB = 33
HYENA_ORDER = 2
DECAY_TARGET = 1e-2
FAST_DECAY_PCT = 0.3
SLOW_DECAY_PCT = 1.5
FNET_GROUPS = 8
EPS = 1e-6
NEG_INF = -1e30

V7X_VMEM_BYTES = 64 * 2**20
VMEM_LIMIT = V7X_VMEM_BYTES - 8 * 2**20
FFN_VMEM_LIMIT = V7X_VMEM_BYTES - 3 * 2**20
LANE = 128
BF16_SUBLANES = 16
COND_ROWS = 16


def _tile(n, pref, unit=LANE):
    if n <= pref:
        return n
    t = (pref // unit) * unit
    while t > unit and n % t:
        t -= unit
    assert n % t == 0, (n, pref)
    return t


def _params(*sem):
    return pltpu.CompilerParams(dimension_semantics=sem, vmem_limit_bytes=VMEM_LIMIT)


@dataclasses.dataclass(frozen=True)
class _Stream:
    n_seq: int
    l: int
    cond0: int
    shared: bool

    @property
    def n(self):
        return self.n_seq * self.l

    def row_tile(self, pref):
        return _tile(self.n if self.shared else self.l, pref, BF16_SUBLANES)

    def cond(self, i, tm):
        return self.cond0 if self.shared else self.cond0 + i // (self.l // tm)


def _fill_mod_norm(x_ref, g_ref, mod_ref, h_ref, rs_ref, mi, zero_ref=None):
    rows = BF16_SUBLANES
    tm, d = x_ref.shape
    cw = _tile(d, 512)

    def stats(c, carry):
        r = pl.multiple_of(c * rows, rows)
        x = x_ref[pl.ds(r, rows), :]
        rs = lax.rsqrt(jnp.mean(x * x, axis=-1, keepdims=True) + EPS)
        rs_ref[pl.ds(r, rows), :] = jnp.broadcast_to(rs, (rows, LANE))
        return carry

    lax.fori_loop(0, tm // rows, stats, 0, unroll=4)

    m = mod_ref[0]
    for t in range(d // cw):
        cs = slice(t * cw, (t + 1) * cw)
        gain = g_ref[:, cs] * (1.0 + m[mi + 1:mi + 2, cs])
        shift = m[mi:mi + 1, cs]

        def apply(c, carry):
            r = pl.multiple_of(c * rows, rows)
            rs = rs_ref[pl.ds(r, rows), :]
            rs = jnp.concatenate([rs] * (cw // LANE), axis=1) if cw > LANE else rs[:, :cw]
            h_ref[pl.ds(r, rows), cs] = (x_ref[pl.ds(r, rows), cs] * rs * gain + shift).astype(h_ref.dtype)
            if zero_ref is not None:
                zero_ref[pl.ds(r, rows), cs] = jnp.zeros((rows, cw), zero_ref.dtype)
            return carry

        lax.fori_loop(0, tm // rows, apply, 0)


def _dot(a, b):
    return jnp.dot(a, b, preferred_element_type=F32)


def _dot_hp(a, b):
    return jnp.dot(a, b, preferred_element_type=F32, precision=HIGHEST)


def _dot_nt(a, b):
    return lax.dot_general(a, b, (((1,), (1,)), ((), ())), preferred_element_type=F32)


def _adaln_kernel(c_ref, w_ref, b_ref, o_ref):
    c = c_ref[...]
    s = c * (1.0 / (1.0 + jnp.exp(-c)))
    s_hi = s.astype(BF16)
    s_lo = (s - s_hi.astype(F32)).astype(BF16)
    w = w_ref[...].astype(BF16)
    o_ref[...] = _dot(s_hi, w) + _dot(s_lo, w) + b_ref[...]


def _adaln(conds, w_mod, b_mod, l):
    d = conds.shape[1]
    n = w_mod.shape[-1]
    tn = _tile(n, 1024)
    out = pl.pallas_call(
        _adaln_kernel,
        grid=(n // tn,),
        in_specs=[
            pl.BlockSpec((COND_ROWS, d), lambda j: (0, 0)),
            pl.BlockSpec((None, d, tn), lambda j: (l, 0, j)),
            pl.BlockSpec((None, 1, tn), lambda j: (l, 0, j)),
        ],
        out_specs=pl.BlockSpec((COND_ROWS, tn), lambda j: (0, j)),
        out_shape=jax.ShapeDtypeStruct((COND_ROWS, n), F32),
        compiler_params=_params("arbitrary"),
        name="adaln",
    )(conds, w_mod, b_mod.reshape(b_mod.shape[0], 1, n))
    return out.reshape(COND_ROWS, N_MOD, d)


FFN_RING = 3


def _ffn_kernel(x_ref, mod_ref, g_ref, fin_ref, w1_hbm, w3_hbm, w2_hbm, *rest, mi, final_norm):
    i, j = pl.program_id(0), pl.program_id(1)
    nf = pl.num_programs(1)
    n_cast = (len(rest) - 7) // 2
    f32_tiles, o_ref, bf16_tiles = rest[:n_cast], rest[n_cast], rest[n_cast + 1:2 * n_cast + 1]
    h_ref, rs_ref, w1_buf, w3_buf, w2_buf, sem = rest[2 * n_cast + 1:]
    tf = w2_buf.shape[1]
    step = i * nf + j
    n_steps = pl.num_programs(0) * nf

    def tile_copies(s):
        slot = s % FFN_RING
        col = pl.multiple_of((s % nf) * tf, tf)
        return (pltpu.make_async_copy(w1_hbm.at[:, pl.ds(col, tf)], w1_buf.at[slot], sem.at[slot, 0]),
                pltpu.make_async_copy(w3_hbm.at[:, pl.ds(col, tf)], w3_buf.at[slot], sem.at[slot, 1]),
                pltpu.make_async_copy(w2_hbm.at[pl.ds(col, tf), :], w2_buf.at[slot], sem.at[slot, 2]))

    @pl.when(step == 0)
    def _():
        for s in range(FFN_RING - 1):
            for copy in tile_copies(s):
                copy.start()

    @pl.when(step + (FFN_RING - 1) < n_steps)
    def _():
        for copy in tile_copies(step + (FFN_RING - 1)):
            copy.start()

    @pl.when(j == 0)
    def _():
        _fill_mod_norm(x_ref, g_ref, mod_ref, h_ref, rs_ref, mi, zero_ref=o_ref)

    for src, dst in zip(f32_tiles, bf16_tiles):
        dst[...] = src[...].astype(dst.dtype)

    for copy in tile_copies(step):
        copy.wait()
    slot = step % FFN_RING
    h = h_ref[...]
    a = _dot(h, w1_buf[slot])
    b = _dot(h, w3_buf[slot])
    act = (a * (1.0 / (1.0 + jnp.exp(-a))) * b).astype(BF16)
    o_ref[...] += _dot(act, w2_buf[slot])

    @pl.when(j == pl.num_programs(1) - 1)
    def _():
        rows = BF16_SUBLANES
        tm, d = o_ref.shape
        half_gate = 0.5 * mod_ref[0][mi + 2:mi + 3]

        def body(c, carry):
            r = pl.multiple_of(c * rows, rows)
            y = x_ref[pl.ds(r, rows), :] + half_gate * o_ref[pl.ds(r, rows), :]
            o_ref[pl.ds(r, rows), :] = y
            if final_norm:
                rs = lax.rsqrt(jnp.mean(y * y, axis=-1, keepdims=True) + EPS)
                rs_ref[pl.ds(r, rows), :] = jnp.broadcast_to(rs, (rows, LANE))
            return carry

        lax.fori_loop(0, tm // rows, body, 0, unroll=4 if final_norm else 1)

        if final_norm:
            cw = _tile(d, 512)
            for t in range(d // cw):
                cs = slice(t * cw, (t + 1) * cw)
                gain = fin_ref[:, cs]

                def scale(c, carry):
                    r = pl.multiple_of(c * rows, rows)
                    rs = jnp.concatenate([rs_ref[pl.ds(r, rows), :]] * (cw // LANE), axis=1)
                    o_ref[pl.ds(r, rows), cs] = o_ref[pl.ds(r, rows), cs] * rs * gain
                    return carry

                lax.fori_loop(0, tm // rows, scale, 0)


FFN_TF = 256


def _ffn_grid(st, f):
    tm = st.row_tile(512)
    return st.n // tm, f // _tile(f, FFN_TF)


def _cast_tiling(grid, src, lead, rows_on_i=True):
    nm, nf = grid
    r, c = src.shape[-2:]
    ni_len, nj_len = (r, c) if rows_on_i else (c, r)
    unit_i, unit_j = (BF16_SUBLANES, LANE) if rows_on_i else (LANE, BF16_SUBLANES)
    nj = next((n for n in range(nf, 0, -1) if nj_len % n == 0 and (nj_len // n) % unit_j == 0), None)
    if ni_len % nm or (ni_len // nm) % unit_i or nj is None:
        return None
    ti, tj = ni_len // nm, nj_len // nj
    pick = (lambda i, j: (i, jnp.minimum(j, nj - 1))) if rows_on_i else (lambda i, j: (jnp.minimum(j, nj - 1), i))
    block = (ti, tj) if rows_on_i else (tj, ti)
    return (pl.BlockSpec((None,) * len(lead) + block, lambda i, j: tuple(lead) + pick(i, j)),
            pl.BlockSpec(block, pick), jax.ShapeDtypeStruct((r, c), BF16))


def _ffn(st, x, mods, g, final_g, w1, w3, w2, mi, final_norm=False, casts=()):
    m, d = x.shape
    f = w1.shape[-1]
    tm = st.row_tile(512)
    tf = _tile(f, FFN_TF)
    nm, nf = _ffn_grid(st, f)
    in_specs = [
        pl.BlockSpec((tm, d), lambda i, j: (i, 0)),
        pl.BlockSpec((1, N_MOD, d), lambda i, j: (st.cond(i, tm), 0, 0)),
        pl.BlockSpec((1, d), lambda i, j: (0, 0)),
        pl.BlockSpec((1, d), lambda i, j: (0, 0)),
        pl.BlockSpec(memory_space=pl.ANY),
        pl.BlockSpec(memory_space=pl.ANY),
        pl.BlockSpec(memory_space=pl.ANY),
    ]
    assert nm * nf >= FFN_RING - 1
    out_specs = [pl.BlockSpec((tm, d), lambda i, j: (i, 0))]
    out_shape = [jax.ShapeDtypeStruct((m, d), F32)]
    args = [x, mods, g.reshape(1, d), final_g.reshape(1, d), w1, w3, w2]
    for src, (src_spec, dst_spec, dst_shape) in casts:
        in_specs.append(src_spec)
        out_specs.append(dst_spec)
        out_shape.append(dst_shape)
        args.append(src)
    out = pl.pallas_call(
        functools.partial(_ffn_kernel, mi=mi, final_norm=final_norm),
        grid=(nm, nf),
        in_specs=in_specs,
        out_specs=out_specs,
        out_shape=out_shape,
        scratch_shapes=[pltpu.VMEM((tm, d), BF16), pltpu.VMEM((tm, LANE), F32),
                        pltpu.VMEM((FFN_RING, d, tf), BF16), pltpu.VMEM((FFN_RING, d, tf), BF16),
                        pltpu.VMEM((FFN_RING, tf, d), BF16), pltpu.SemaphoreType.DMA((FFN_RING, 3))],
        compiler_params=pltpu.CompilerParams(dimension_semantics=("arbitrary", "arbitrary"),
                                             vmem_limit_bytes=FFN_VMEM_LIMIT),
        name="ffn",
    )(*args)
    return out[0], list(out[1:])


def _norm_mm_kernel(x_ref, mod_ref, g_ref, w_hbm, *rest, mi, tile_ranges, e):
    outs, (h_ref, rs_ref, w_buf, sem) = rest[:-4], rest[-4:]
    i, j = pl.program_id(0), pl.program_id(1)
    nj = pl.num_programs(1)
    tn = w_buf.shape[2]
    step = i * nj + j
    n_steps = pl.num_programs(0) * nj

    def tile_copy(s):
        slot = s % FFN_RING
        col = pl.multiple_of((s % nj) * tn, tn)
        return pltpu.make_async_copy(w_hbm.at[e, :, pl.ds(col, tn)], w_buf.at[slot], sem.at[slot])

    @pl.when(step == 0)
    def _():
        for s in range(FFN_RING - 1):
            tile_copy(s).start()

    @pl.when(step + (FFN_RING - 1) < n_steps)
    def _():
        tile_copy(step + (FFN_RING - 1)).start()

    @pl.when(j == 0)
    def _():
        _fill_mod_norm(x_ref, g_ref, mod_ref, h_ref, rs_ref, mi)

    tile_copy(step).wait()
    slot = step % FFN_RING
    for o_ref, (lo, hi) in zip(outs, tile_ranges):
        @pl.when((j >= lo) & (j < hi))
        def _(o_ref=o_ref):
            o_ref[...] = _dot(h_ref[...], w_buf[slot]).astype(o_ref.dtype)


def _norm_mm(st, x, mods, g, w, e, mi, splits):
    m, d = x.shape
    n = w.shape[-1]
    tm = st.row_tile(512)
    tn = _tile(math.gcd(*[width for width, _ in splits]), 512)
    assert sum(width for width, _ in splits) == n
    tile_ranges, out_specs, lo = [], [], 0
    for width, _ in splits:
        hi = lo + width // tn
        tile_ranges.append((lo, hi))
        out_specs.append(pl.BlockSpec((tm, tn), lambda i, j, lo=lo, hi=hi: (i, jnp.clip(j - lo, 0, hi - lo - 1))))
        lo = hi
    return pl.pallas_call(
        functools.partial(_norm_mm_kernel, mi=mi, tile_ranges=tuple(tile_ranges), e=e),
        grid=(m // tm, n // tn),
        in_specs=[
            pl.BlockSpec((tm, d), lambda i, j: (i, 0)),
            pl.BlockSpec((1, N_MOD, d), lambda i, j: (st.cond(i, tm), 0, 0)),
            pl.BlockSpec((1, d), lambda i, j: (0, 0)),
            pl.BlockSpec(memory_space=pl.ANY),
        ],
        out_specs=out_specs,
        out_shape=[jax.ShapeDtypeStruct((m, width), dtype) for width, dtype in splits],
        scratch_shapes=[pltpu.VMEM((tm, d), BF16), pltpu.VMEM((tm, LANE), F32),
                        pltpu.VMEM((FFN_RING, d, tn), BF16), pltpu.SemaphoreType.DMA((FFN_RING,))],
        compiler_params=pltpu.CompilerParams(dimension_semantics=("arbitrary", "arbitrary"),
                                             vmem_limit_bytes=FFN_VMEM_LIMIT),
        name="in_proj",
    )(x, mods, g.reshape(1, d), w)


def _mod_norm_kernel(x_ref, mod_ref, g_ref, h_ref, rs_ref, *, mi):
    _fill_mod_norm(x_ref, g_ref, mod_ref, h_ref, rs_ref, mi)


def _mod_norm(st, x, mods, g, mi):
    m, d = x.shape
    tm = st.row_tile(256)
    return pl.pallas_call(
        functools.partial(_mod_norm_kernel, mi=mi),
        grid=(m // tm,),
        in_specs=[
            pl.BlockSpec((tm, d), lambda i: (i, 0)),
            pl.BlockSpec((1, N_MOD, d), lambda i: (st.cond(i, tm), 0, 0)),
            pl.BlockSpec((1, d), lambda i: (0, 0)),
        ],
        out_specs=pl.BlockSpec((tm, d), lambda i: (i, 0)),
        out_shape=jax.ShapeDtypeStruct((m, d), BF16),
        scratch_shapes=[pltpu.VMEM((tm, LANE), F32)],
        compiler_params=_params("parallel"),
        name="mod_norm",
    )(x, mods, g.reshape(1, d))


def _mm_res_kernel(a0_ref, a1_ref, w0_ref, w1_ref, res_ref, mod_ref, o_ref, *, gi):
    mix = _dot(a0_ref[...], w0_ref[...]) + _dot(a1_ref[...], w1_ref[...])
    o_ref[...] = res_ref[...] + mod_ref[0][gi:gi + 1] * mix


def _mm_res(st, a0, a1, w, e, res, mods, gi, name):
    k = w.shape[1] // 2
    m = a0.shape[0]
    c1 = 0
    if a1 is None:
        a1, c1 = a0, 1
    assert a0.shape[1] >= k and a1.shape == (m, (c1 + 1) * k)
    n = w.shape[-1]
    tm = st.row_tile(1024)
    tn = _tile(n, 1024)
    return pl.pallas_call(
        functools.partial(_mm_res_kernel, gi=gi),
        grid=(m // tm, n // tn),
        in_specs=[
            pl.BlockSpec((tm, k), lambda i, j: (i, 0)),
            pl.BlockSpec((tm, k), lambda i, j: (i, c1)),
            pl.BlockSpec((None, k, tn), lambda i, j: (e, 0, j)),
            pl.BlockSpec((None, k, tn), lambda i, j: (e, 1, j)),
            pl.BlockSpec((tm, tn), lambda i, j: (i, j)),
            pl.BlockSpec((1, N_MOD, tn), lambda i, j: (st.cond(i, tm), 0, j)),
        ],
        out_specs=pl.BlockSpec((tm, tn), lambda i, j: (i, j)),
        out_shape=jax.ShapeDtypeStruct((m, n), F32),
        compiler_params=_params("parallel", "arbitrary"),
        name=name,
    )(a0, a1, w, w, res, mods)


def _ctx_attn_kernel(q_ref, k_ref, v_ref, o_ref, *, n_heads, scale):
    for h in range(n_heads):
        sl = slice(h * HEAD_DIM, (h + 1) * HEAD_DIM)
        q = q_ref[:, sl].astype(BF16)
        k = k_ref[:, sl].astype(BF16)
        v = v_ref[:, sl].astype(BF16)
        s = _dot_nt(q, k) * scale
        e = jnp.exp(s - jnp.max(s, axis=-1, keepdims=True))
        p = e * (1.0 / jnp.sum(e, axis=-1, keepdims=True))
        o_ref[:, sl] = _dot(p.astype(BF16), v).astype(o_ref.dtype)


def _ctx_attn(st, q, k, v):
    l = st.l
    width = q.shape[1]
    return pl.pallas_call(
        functools.partial(_ctx_attn_kernel, n_heads=width // HEAD_DIM, scale=HEAD_DIM ** -0.5),
        grid=(st.n_seq,),
        in_specs=[pl.BlockSpec((l, width), lambda b: (b, 0))] * 3,
        out_specs=pl.BlockSpec((l, width), lambda b: (b, 0)),
        out_shape=jax.ShapeDtypeStruct((st.n, width), BF16),
        compiler_params=_params("parallel"),
        name="ctx_attn",
    )(q, k, v)


def _na_window(r, rows, kh):
    start = min(max(r - kh // 2, 0), rows - kh)
    lo = start - start % 2
    hi = start + kh + (start + kh) % 2
    return start, lo, hi


NA_ROW_GROUP = 4


def _na_attn_kernel(q_ref, k_ref, v_ref, ck_ref, cv_ref, tab_ref, o_ref, *, rows, kh, scale):
    ck = ck_ref[...].astype(BF16)
    cv = cv_ref[...].astype(BF16)
    for g0 in range(0, rows, NA_ROW_GROUP):
        group = range(g0, min(g0 + NA_ROW_GROUP, rows))
        windows = [_na_window(r, rows, kh) for r in group]
        glo = min(w[1] for w in windows) * GRID_W
        ghi = max(w[2] for w in windows) * GRID_W
        q = q_ref[g0 * GRID_W:(g0 + len(group)) * GRID_W, :].astype(BF16)
        s_lat_g = _dot_nt(q, k_ref[glo:ghi, :].astype(BF16))
        s_ctx_g = _dot_nt(q, ck)
        p_lat_g, p_ctx_g = [], []
        for i, (r, (start, lo, hi)) in enumerate(zip(group, windows)):
            qs = slice(i * GRID_W, (i + 1) * GRID_W)
            pieces = []
            for p in range(lo // 2, hi // 2):
                ok0 = start <= 2 * p < start + kh
                ok1 = start <= 2 * p + 1 < start + kh
                dr = 2 * p - r + WIN_H - 1
                pieces.append(tab_ref[0, 0, dr] if ok0 and ok1 else tab_ref[0, 1, dr] if ok0
                              else tab_ref[0, 2, dr + 1])
            lo, hi = lo * GRID_W, hi * GRID_W
            s_lat = s_lat_g[qs, lo - glo:hi - glo] * scale + jnp.concatenate(pieces, axis=1)
            s_ctx = s_ctx_g[qs, :] * scale
            mx = jnp.maximum(jnp.max(s_lat, axis=-1, keepdims=True), jnp.max(s_ctx, axis=-1, keepdims=True))
            e_lat = jnp.exp(s_lat - mx)
            e_ctx = jnp.exp(s_ctx - mx)
            inv = 1.0 / (jnp.sum(e_lat, axis=-1, keepdims=True) + jnp.sum(e_ctx, axis=-1, keepdims=True))
            parts = [(e_lat * inv).astype(BF16)]
            if lo > glo:
                parts.insert(0, jnp.zeros((GRID_W, lo - glo), BF16))
            if hi < ghi:
                parts.append(jnp.zeros((GRID_W, ghi - hi), BF16))
            p_lat_g.append(jnp.concatenate(parts, axis=1) if len(parts) > 1 else parts[0])
            p_ctx_g.append((e_ctx * inv).astype(BF16))
        o = (_dot(jnp.concatenate(p_lat_g, axis=0), v_ref[glo:ghi, :].astype(BF16))
             + _dot(jnp.concatenate(p_ctx_g, axis=0), cv))
        o_ref[g0 * GRID_W:(g0 + len(group)) * GRID_W, :] = o.astype(o_ref.dtype)


def _na_bias_table(rpb):
    col = jnp.arange(GRID_W)
    col_start = jnp.clip(col - WIN_W // 2, 0, GRID_W - WIN_W)
    col_ok = (col[None, :] >= col_start[:, None]) & (col[None, :] < col_start[:, None] + WIN_W)
    dc_idx = jnp.clip(col[None, :] - col[:, None] + WIN_W - 1, 0, 2 * WIN_W - 2)
    t = jnp.where(col_ok, rpb.astype(F32)[:, :, dc_idx], NEG_INF)
    neg = jnp.full_like(t, NEG_INF)
    nxt = jnp.concatenate([t[:, 1:], neg[:, :1]], axis=1)
    return jnp.stack([jnp.concatenate([t, nxt], axis=-1), jnp.concatenate([t, neg], axis=-1),
                      jnp.concatenate([neg, t], axis=-1)], axis=1)


def _na_attn(st, q, k, v, cache_k, cache_v, rpb):
    l = st.l
    width = q.shape[1]
    n_heads = width // HEAD_DIM
    rows = l // GRID_W
    kh = min(WIN_H, rows)
    assert rows % 2 == 0 and kh % 2 == 0
    past = cache_k.shape[0] // st.n_seq
    tab = _na_bias_table(rpb)
    return pl.pallas_call(
        functools.partial(_na_attn_kernel, rows=rows, kh=kh, scale=HEAD_DIM ** -0.5),
        grid=(st.n_seq, n_heads),
        in_specs=[pl.BlockSpec((l, HEAD_DIM), lambda b, h: (b, h))] * 3 + [
            pl.BlockSpec((past, HEAD_DIM), lambda b, h: (b, h)),
            pl.BlockSpec((past, HEAD_DIM), lambda b, h: (b, h)),
            pl.BlockSpec((1,) + tab.shape[1:], lambda b, h: (h, 0, 0, 0, 0)),
        ],
        out_specs=pl.BlockSpec((l, HEAD_DIM), lambda b, h: (b, h)),
        out_shape=jax.ShapeDtypeStruct((st.n, width), BF16),
        compiler_params=_params("parallel", "arbitrary"),
        name="na_attn",
    )(q, k, v, cache_k, cache_v, tab)


def _cos_sin(n_rows, n_cols, period):
    r = lax.broadcasted_iota(jnp.int32, (n_rows, n_cols), 0)
    c = lax.broadcasted_iota(jnp.int32, (n_rows, n_cols), 1)
    ang = ((r * c) % period).astype(F32) * (2.0 * math.pi / period)
    return jnp.cos(ang), jnp.sin(ang)


def _rfft_matrices(l):
    n = 2 * l
    cos, sin = _cos_sin(l, l, n)
    alt = jnp.where(lax.broadcasted_iota(jnp.int32, (l, l), 1) % 2 == 0, 1.0, -1.0).astype(F32)
    k_is0 = lax.broadcasted_iota(jnp.int32, (l, l), 0) == 0
    fwd_re = cos
    fwd_im = jnp.where(k_is0, alt, -sin)
    bin_is0 = lax.broadcasted_iota(jnp.int32, (l, l), 1) == 0
    inv_re = jnp.where(bin_is0, 1.0 / n, 2.0 / n) * cos
    inv_im = jnp.where(bin_is0, alt.T / n, (-2.0 / n) * sin)
    return fwd_re, fwd_im, inv_re, inv_im


def _split_bf16(a):
    hi = a.astype(BF16)
    return hi, (a - hi.astype(F32)).astype(BF16)


def _dot_3pass(a_hi_ref, a_lo_ref, b):
    b_hi, b_lo = _split_bf16(b)
    a_hi = a_hi_ref[...]
    return _dot(a_hi, b_hi) + (_dot(a_hi, b_lo) + _dot(a_lo_ref[...], b_hi))


def _filter_kernel(z_ref, t_ref, w1_ref, b1_ref, w2_ref, b2_ref, fq_ref, w3f_ref, w3b_ref, dl_ref,
                   fre_hi_ref, fre_lo_ref, fim_hi_ref, fim_lo_ref, kr_ref, ki_ref, kn_ref):
    fq = fq_ref[...]
    hid = jnp.sin(fq[0:1] * (_dot_hp(z_ref[...], w1_ref[...]) + b1_ref[...]))
    hid = jnp.sin(fq[1:2] * (_dot_hp(hid, w2_ref[...]) + b2_ref[...]))
    decay = jnp.exp(-t_ref[...] * dl_ref[...])
    row = lax.broadcasted_iota(jnp.int32, kr_ref.shape, 0)
    h_fwd = _dot_hp(hid, w3f_ref[...]) * decay
    h_bwd = jnp.where(row == 0, 0.0, _dot_hp(hid, w3b_ref[...]) * decay)
    nrm = lax.rsqrt(jnp.sum(h_fwd * h_fwd + h_bwd * h_bwd, axis=0, keepdims=True) + EPS)
    even = (h_fwd + h_bwd) * nrm
    odd = (h_fwd - h_bwd) * nrm
    k_re = _dot_3pass(fre_hi_ref, fre_lo_ref, even)
    k_im = _dot_3pass(fim_hi_ref, fim_lo_ref, odd)
    nyq = jnp.sum(jnp.where(row % 2 == 0, even, -even), axis=0, keepdims=True)
    kr_ref[...] = k_re
    ki_ref[...] = jnp.where(row == 0, 0.0, k_im)
    kn_ref[...] = jnp.where(row == 0, nyq, k_re)


def _pad_to(a, shape):
    return jnp.pad(a.astype(F32), [(0, s - n) for n, s in zip(a.shape, shape)])


def _hyena_filters(l, width, w1, b1, w2, b2, w3, freq, fwd_re, fwd_im):
    hid = w2.shape[0]
    hp = LANE * pl.cdiv(hid, LANE)
    n_bands = (FILTER_EMB - 1) // 2
    t = jnp.linspace(0.0, 1.0, l, dtype=F32)[:, None]
    w = (2.0 * math.pi / l) * jnp.arange(l, dtype=F32)[:, None]
    bands = jnp.linspace(1e-4, n_bands - 1, n_bands, dtype=F32)[None, :]
    z = _pad_to(jnp.concatenate([t, jnp.cos(bands * w), -jnp.sin(bands * w)], axis=-1), (l, LANE))
    deltas = jnp.abs(jnp.linspace(math.log(DECAY_TARGET) / FAST_DECAY_PCT,
                                  math.log(DECAY_TARGET) / SLOW_DECAY_PCT, width, dtype=F32))[None, :]
    tc = _tile(width, 512)
    nc = width // tc
    small = lambda shape: pl.BlockSpec(shape, lambda o, c: (0,) * len(shape))
    out = jax.ShapeDtypeStruct((HYENA_ORDER, l, width), F32)
    return pl.pallas_call(
        _filter_kernel,
        grid=(HYENA_ORDER, nc),
        in_specs=[small((l, LANE)), small((l, 1)), small((LANE, hp)), small((1, hp)), small((hp, hp)),
                  small((1, hp)), small((2, hp)),
                  pl.BlockSpec((hp, tc), lambda o, c: (0, 2 * o * nc + c)),
                  pl.BlockSpec((hp, tc), lambda o, c: (0, (2 * o + 1) * nc + c)),
                  pl.BlockSpec((1, tc), lambda o, c: (0, c))] + [small((l, l))] * 4,
        out_specs=[pl.BlockSpec((None, l, tc), lambda o, c: (o, 0, c))] * 3,
        out_shape=[out, out, out],
        compiler_params=_params("arbitrary", "arbitrary"),
        name="hyena_filters",
    )(z, t, _pad_to(w1, (LANE, hp)), _pad_to(b1[None, :], (1, hp)), _pad_to(w2, (hp, hp)),
      _pad_to(b2[None, :], (1, hp)), _pad_to(freq, (2, hp)), _pad_to(w3, (hp, w3.shape[1])),
      _pad_to(w3, (hp, w3.shape[1])), deltas, *_split_bf16(fwd_re), *_split_bf16(fwd_im))


HYENA_ROW_CHUNK = 256


def _hyena_kernel(pv_ref, p1_ref, p2_ref, wv_ref, w1_ref, w2_ref, bv_ref, b1_ref, b2_ref,
                  fre_ref, fim_ref, gre_ref, gim_ref, kr_ref, ki_ref, kn_ref, fbias_ref, o_ref, *, l):
    tc = o_ref.shape[1]
    row = lax.broadcasted_iota(jnp.int32, (l, tc), 0)

    def short_conv(p, w_ref, b_ref):
        prev = jnp.where(row == 0, 0.0, pltpu.roll(p, 1, 0))
        nxt = jnp.where(row == l - 1, 0.0, pltpu.roll(p, l - 1, 0))
        w = w_ref[...]
        out = b_ref[...] + prev * w[0:1]
        out = out + p * w[1:2]
        return out + nxt * w[2:3]

    chunks = [slice(r, r + HYENA_ROW_CHUNK) for r in range(0, l, HYENA_ROW_CHUNK)] if l > HYENA_ROW_CHUNK \
        else [slice(0, l)]

    def gated_long_conv(x, u, o, bias):
        ub = u.astype(BF16)
        w_re, w_im = [], []
        for rs in chunks:
            u_re = _dot(fre_ref[rs, :], ub)
            u_im = _dot(fim_ref[rs, :], ub)
            kr, ki, kn = kr_ref[o, rs, :], ki_ref[o, rs, :], kn_ref[o, rs, :]
            w_re.append((u_re * kr - u_im * ki).astype(BF16))
            w_im.append((u_re * ki + u_im * kn).astype(BF16))
        w_re = jnp.concatenate(w_re, axis=0)
        w_im = jnp.concatenate(w_im, axis=0)
        out = []
        for rs in chunks:
            y = _dot(gre_ref[rs, :], w_re) + _dot(gim_ref[rs, :], w_im)
            out.append(x[rs] * (y + u[rs] * bias))
        return jnp.concatenate(out, axis=0)

    fbias = fbias_ref[...]
    for s in range(o_ref.shape[0] // l):
        rs = slice(s * l, (s + 1) * l)
        v = short_conv(pv_ref[rs, :], wv_ref, bv_ref)
        z = gated_long_conv(short_conv(p1_ref[rs, :], w1_ref, b1_ref), v, 0, fbias[0:1])
        z = gated_long_conv(short_conv(p2_ref[rs, :], w2_ref, b2_ref), z, 1, fbias[1:2])
        o_ref[rs, :] = z.astype(o_ref.dtype)


def _hyena(st, proj, e, w_short, b_short, mats, filters, fbias):
    l = st.l
    width = proj.shape[1] // 3
    tc = _tile(width, 1024 if l <= 256 else 256)
    per_step = 2 if l > 256 and st.n_seq % 2 == 0 else 1
    nc = width // tc
    rows = per_step * l
    once = pl.Buffered(1)
    p_specs = [pl.BlockSpec((rows, tc), lambda c, b, part=part: (b, part * nc + c)) for part in range(3)]
    w_specs = [pl.BlockSpec((None, SHORT_CONV, tc), lambda c, b, part=part: (e, 0, part * nc + c))
               for part in range(3)]
    b_specs = [pl.BlockSpec((None, 1, tc), lambda c, b, part=part: (e, 0, part * nc + c)) for part in range(3)]
    mat_specs = [pl.BlockSpec((l, l), lambda c, b: (0, 0), pipeline_mode=once)] * 4
    k_specs = [pl.BlockSpec((HYENA_ORDER, l, tc), lambda c, b: (0, 0, c), pipeline_mode=once)] * 3
    return pl.pallas_call(
        functools.partial(_hyena_kernel, l=l),
        grid=(nc, st.n_seq // per_step),
        in_specs=p_specs + w_specs + b_specs + mat_specs + k_specs
        + [pl.BlockSpec((None, HYENA_ORDER, tc), lambda c, b: (e, 0, c))],
        out_specs=pl.BlockSpec((rows, tc), lambda c, b: (b, c)),
        out_shape=jax.ShapeDtypeStruct((st.n, width), BF16),
        compiler_params=_params("parallel", "arbitrary"),
        name="hyena",
    )(proj, proj, proj, w_short, w_short, w_short, *([b_short.reshape(b_short.shape[0], 1, -1)] * 3),
      *[a.astype(BF16) for a in mats], *filters, fbias)


FNET_ROW_CHUNK = 256


def _fnet_mix_kernel(cos_l_ref, sin_l_ref, cos_c_ref, sin_c_ref, h_ref, o_ref, *, gw):
    l, tn = h_ref.shape
    cos_c = cos_c_ref[...]
    sin_c = sin_c_ref[...]
    for g in range(tn // gw):
        cs = slice(g * gw, (g + 1) * gw)
        h = h_ref[:, cs]
        for r in range(0, l, FNET_ROW_CHUNK):
            rs = slice(r, min(r + FNET_ROW_CHUNK, l))
            a = _dot(cos_l_ref[rs, :], h).astype(BF16)
            b = _dot(sin_l_ref[rs, :], h).astype(BF16)
            o_ref[rs, cs] = (_dot(a, cos_c) - _dot(b, sin_c)).astype(o_ref.dtype)


def _fnet_mix(st, h):
    l = st.l
    d = h.shape[1]
    gw = d // FNET_GROUPS
    groups_per_step = min(FNET_GROUPS, max(1, 2048 // gw)) if l <= 256 else 1
    tn = gw * groups_per_step
    cos_l, sin_l = _cos_sin(l, l, l)
    cos_c, sin_c = _cos_sin(gw, gw, gw)
    mats = [(cos_l * l ** -0.5).astype(BF16), (sin_l * l ** -0.5).astype(BF16),
            (cos_c * gw ** -0.5).astype(BF16), (sin_c * gw ** -0.5).astype(BF16)]
    once = pl.Buffered(1)
    return pl.pallas_call(
        functools.partial(_fnet_mix_kernel, gw=gw),
        grid=(st.n_seq, d // tn),
        in_specs=[pl.BlockSpec((l, l), lambda b, j: (0, 0), pipeline_mode=once)] * 2
        + [pl.BlockSpec((gw, gw), lambda b, j: (0, 0), pipeline_mode=once)] * 2
        + [pl.BlockSpec((l, tn), lambda b, j: (b, j))],
        out_specs=pl.BlockSpec((l, tn), lambda b, j: (b, j)),
        out_shape=jax.ShapeDtypeStruct((st.n, d), BF16),
        compiler_params=_params("parallel", "arbitrary"),
        name="fnet_mix",
    )(*mats, h)


def kernel(x_prompt, x_sample, cache_k, cache_v, c, c_ctx, w_mod, b_mod, norm_g, ffn_w1, ffn_w3, ffn_w2,
           w_in, w_out, rpb, w_short, b_short, filt_w1, filt_b1, filt_w2, filt_b2, filt_w3, filt_freq,
           filt_bias, w_fnet, final_g):
    b_p, l_p, d = x_prompt.shape
    b_s, l_s, _ = x_sample.shape
    depth = w_mod.shape[0]
    attn_w = cache_k.shape[-2] * cache_k.shape[-1]
    hyena_w = filt_bias.shape[-1]
    n_heads = attn_w // HEAD_DIM
    past = cache_k.shape[2]
    assert 1 + b_s <= COND_ROWS

    streams = (_Stream(b_p, l_p, 0, True), _Stream(b_s, l_s, 1, False))
    xs = [x_prompt.reshape(b_p * l_p, d), x_sample.reshape(b_s * l_s, d)]
    conds = jnp.concatenate([c_ctx[None, :], c, jnp.zeros((COND_ROWS - 1 - b_s, d), F32)], axis=0)
    rfft_mats = {st.l: _rfft_matrices(st.l) for st in streams}

    grids = [_ffn_grid(st, ffn_w1.shape[-1]) for st in streams]
    ffn_w = {(0, 0): (ffn_w1[0, 0].astype(BF16), ffn_w3[0, 0].astype(BF16), ffn_w2[0, 0].astype(BF16))}

    def ffn_half_step(xs, mods, l, s, extra=()):
        nxt = (l, 1) if s == 0 else (l + 1, 0)
        last = nxt[0] == depth
        w = ffn_w[(l, s)]
        g, mi = norm_g[l, 2 * s], 6 * s
        jobs0 = [] if last else [(a, _cast_tiling(grids[0], a, nxt, a is not ffn_w2))
                                 for a in (ffn_w1, ffn_w3, ffn_w2)]
        if any(tiling is None for _, tiling in jobs0):
            ffn_w[nxt] = tuple(a[nxt].astype(BF16) for a in (ffn_w1, ffn_w3, ffn_w2))
            jobs0 = []
        jobs1 = [(a, _cast_tiling(grids[1], a, lead)) for a, lead in extra]
        x0, done0 = _ffn(streams[0], xs[0], mods, g, final_g, *w, mi, final_norm=last, casts=jobs0)
        x1, done1 = _ffn(streams[1], xs[1], mods, g, final_g, *w, mi, final_norm=last,
                         casts=[job for job in jobs1 if job[1] is not None])
        if jobs0:
            ffn_w[nxt] = tuple(done0)
        done1 = iter(done1)
        converted = [next(done1) if tiling is not None else a[lead].astype(BF16)
                     for (a, tiling), (_, lead) in zip(jobs1, extra)]
        return [x0, x1], converted

    new_k, new_v = [], []
    for l in range(depth):
        mods = _adaln(conds, w_mod, b_mod, l)
        mixer_w = [(w_in, (l // 2,)), (w_out, (l // 2,))] if l % 2 == 0 else [(w_fnet, (l // 2,))]
        xs, mixer_w = ffn_half_step(xs, mods, l, 0, mixer_w)
        if l % 2 == 0:
            e = l // 2
            w_in_b, w_out_b = (a[None] for a in mixer_w)
            kv_dtypes = (F32, BF16)
            qkvh = [_norm_mm(st, x, mods, norm_g[l, 1], w_in_b, 0, 3,
                             ((attn_w, BF16), (attn_w, kv), (attn_w, kv), (3 * hyena_w, F32)))
                    for st, x, kv in zip(streams, xs, kv_dtypes)]
            attn = [_ctx_attn(streams[0], *qkvh[0][:3]),
                    _na_attn(streams[1], *qkvh[1][:3], cache_k[:, e].reshape(b_s * past, attn_w),
                             cache_v[:, e].reshape(b_s * past, attn_w), rpb[e])]
            for i, st in enumerate(streams):
                mats = rfft_mats[st.l]
                filters = _hyena_filters(st.l, hyena_w, filt_w1[e], filt_b1[e], filt_w2[e], filt_b2[e],
                                         filt_w3[e], filt_freq[e], mats[0], mats[1])
                hy = _hyena(st, qkvh[i][3], e, w_short, b_short, mats, filters, filt_bias)
                xs[i] = _mm_res(st, attn[i], hy, w_out_b, 0, xs[i], mods, 5, "out_proj")
            new_k.append(qkvh[0][1].reshape(b_p, l_p, n_heads, HEAD_DIM))
            new_v.append(qkvh[0][2].reshape(b_p, l_p, n_heads, HEAD_DIM))
        else:
            w_fnet_b = mixer_w[0][None]
            for i, st in enumerate(streams):
                mixed = _fnet_mix(st, _mod_norm(st, xs[i], mods, norm_g[l, 1], 3))
                xs[i] = _mm_res(st, mixed, None, w_fnet_b, 0, xs[i], mods, 5, "fnet_out")
        xs, _ = ffn_half_step(xs, mods, l, 1)

    y_prompt = xs[0].reshape(b_p, l_p, d)
    y_sample = xs[1].reshape(b_s, l_s, d)
    return y_prompt, y_sample, jnp.stack(new_k, axis=1), jnp.stack(new_v, axis=1)
```
